```python
import math
import jax
import jax.numpy as jnp
from jax import lax
import numpy as np

D_MODEL = 1024
BATCH = 8
SEQ = 4096
DEPTH = 4

CHUNK = 64
Q_BLOCK = 128
HEAD_DIM = 64
ROPE_THETA = 10000.0
EPS = 1e-6
A_HEADS = 4
A_VDIM = 2 * HEAD_DIM
B_HEADS = 8
C_CHANNELS = 512
CONV_WIDTH = 31
D_HEADS = 8
LEFT_CHUNKS = 8
BAND_CHUNKS = LEFT_CHUNKS + 1
BAND = BAND_CHUNKS * CHUNK
MAX_REL = 128
D_FF = 2816
N_EXPERTS = 8
TOP_K = 2
D_FF_EXPERT = 3584
N_PAIR = DEPTH // 2

A_QK = A_HEADS * 2 * HEAD_DIM
A_V = A_HEADS * A_VDIM
B_W = B_HEADS * HEAD_DIM
D_W = D_HEADS * HEAD_DIM
EVEN_SPLITS = (A_QK, A_QK, A_V, B_W, B_W, B_W)
ODD_SPLITS = (C_CHANNELS, C_CHANNELS, D_W, D_W, D_W)
EVEN_IN = sum(EVEN_SPLITS)
ODD_IN = sum(ODD_SPLITS)
EVEN_MIX = A_V + B_W
ODD_MIX = C_CHANNELS + D_W

kernel_name = 'hybrid_streaming_diff_stickbreak_conformer_moe'


def _split(a, widths):
    cuts = [int(c) for c in np.cumsum(widths)[:-1]]
    return jnp.split(a, cuts, axis=-1)


def rmsnorm(x, g):
    xf = x.astype(jnp.float32)
    y = xf * lax.rsqrt(jnp.mean(xf * xf, axis=-1, keepdims=True) + EPS)
    return (y * g.astype(jnp.float32)).astype(x.dtype)


def layernorm(x, g, b):
    xf = x.astype(jnp.float32)
    mu = jnp.mean(xf, axis=-1, keepdims=True)
    xc = xf - mu
    var = jnp.mean(xc * xc, axis=-1, keepdims=True)
    return (xc * lax.rsqrt(var + EPS) * g.astype(jnp.float32) + b.astype(jnp.float32)).astype(x.dtype)


def rope(x):
    s, d = x.shape[1], x.shape[-1]
    half = d // 2
    inv_freq = ROPE_THETA ** (-jnp.arange(half, dtype=jnp.float32) * 2.0 / d)
    ang = jnp.arange(s, dtype=jnp.float32)[:, None] * inv_freq[None, :]
    cos = jnp.cos(ang)[None, :, None, :]
    sin = jnp.sin(ang)[None, :, None, :]
    xf = x.astype(jnp.float32)
    x1, x2 = xf[..., :half], xf[..., half:]
    return jnp.concatenate([x1 * cos - x2 * sin, x2 * cos + x1 * sin], axis=-1).astype(x.dtype)


def _block_split(a):
    b, s = a.shape[:2]
    return a.reshape(b, s // Q_BLOCK, Q_BLOCK, *a.shape[2:]).swapaxes(0, 1)


def _block_merge(a):
    nb, b, qb = a.shape[:3]
    return a.swapaxes(0, 1).reshape(b, nb * qb, *a.shape[3:])


def diff_attention(q, k, v, lam):
    b, s = q.shape[:2]
    scale = HEAD_DIM ** -0.5
    key_chunk = jnp.arange(s) // CHUNK

    def block(args):
        qb, start = args
        q_chunk = (start + jnp.arange(Q_BLOCK)) // CHUNK
        mask = key_chunk[None, :] <= q_chunk[:, None]
        scores = jnp.einsum('bqhd,bkhd->bhqk', qb, k, preferred_element_type=jnp.float32) * scale
        p = jax.nn.softmax(jnp.where(mask, scores, -jnp.inf), axis=-1)
        p = p.reshape(b, A_HEADS, 2, Q_BLOCK, s)
        p_diff = p[:, :, 0] - lam * p[:, :, 1]
        return jnp.einsum('bhqk,bkhd->bqhd', p_diff, v).astype(v.dtype)

    starts = jnp.arange(s // Q_BLOCK) * Q_BLOCK
    return _block_merge(lax.map(block, (_block_split(q), starts)))


def stick_breaking(q, k, v):
    b, s = q.shape[:2]
    scale = HEAD_DIM ** -0.5
    key_pos = jnp.arange(s)

    def block(args):
        qb, start = args
        q_pos = start + jnp.arange(Q_BLOCK)
        mask = key_pos[None, :] < q_pos[:, None]
        z = jnp.einsum('bqhd,bkhd->bhqk', qb, k, preferred_element_type=jnp.float32) * scale
        log_1m_beta = jnp.where(mask, jax.nn.log_sigmoid(-z), 0.0)
        rev = lax.cumsum(log_1m_beta, axis=3, reverse=True)
        after = jnp.pad(rev[..., 1:], ((0, 0), (0, 0), (0, 0), (0, 1)))
        weights = jnp.where(mask, jnp.exp(jax.nn.log_sigmoid(z) + after), 0.0)
        return jnp.einsum('bhqk,bkhd->bqhd', weights, v).astype(v.dtype)

    starts = jnp.arange(s // Q_BLOCK) * Q_BLOCK
    return _block_merge(lax.map(block, (_block_split(q), starts)))


def _rel_index():
    iq = np.arange(CHUNK)[:, None]
    key_off = np.arange(BAND)[None, :] - LEFT_CHUNKS * CHUNK
    return np.clip(iq - key_off, -MAX_REL, MAX_REL) + MAX_REL


def chunk_rel_attention(q, k, v, rel_table):
    b, s, h, dh = q.shape
    nc = s // CHUNK

    def band(a):
        a = a.reshape(b, nc, CHUNK, h, dh)
        a = jnp.pad(a, ((0, 0), (LEFT_CHUNKS, 0), (0, 0), (0, 0), (0, 0)))
        return jnp.stack([a[:, i:i + nc] for i in range(BAND_CHUNKS)], axis=2).reshape(b, nc, BAND, h, dh)

    k_band, v_band = band(k), band(v)
    qc = q.reshape(b, nc, CHUNK, h, dh)
    scores = jnp.einsum('bcqhd,bckhd->bhcqk', qc, k_band, preferred_element_type=jnp.float32) * (dh ** -0.5)
    bias = rel_table.astype(jnp.float32)[:, _rel_index()]
    scores = scores + bias[None, :, None]
    key_chunk = jnp.arange(nc)[:, None] - LEFT_CHUNKS + (jnp.arange(BAND) // CHUNK)[None, :]
    valid = (key_chunk >= 0)[:, None, :]
    p = jax.nn.softmax(jnp.where(valid, scores, -jnp.inf), axis=-1)
    out = jnp.einsum('bhcqk,bckhd->bcqhd', p, v_band)
    return out.reshape(b, s, h, dh).astype(q.dtype)


def even_mixer(h, w_in, lq1, lk1, lq2, lk2, subln_g, w_out, lambda_init):
    b, s, _ = h.shape
    qa, ka, va, qb, kb, vb = _split(h @ w_in, EVEN_SPLITS)
    qa = rope(qa.reshape(b, s, 2 * A_HEADS, HEAD_DIM))
    ka = rope(ka.reshape(b, s, 2 * A_HEADS, HEAD_DIM))
    va = va.reshape(b, s, A_HEADS, A_VDIM)
    f32 = jnp.float32
    lam = (jnp.exp(jnp.sum(lq1.astype(f32) * lk1.astype(f32)))
           - jnp.exp(jnp.sum(lq2.astype(f32) * lk2.astype(f32))) + lambda_init)
    oa = diff_attention(qa, ka, va, lam)
    oa = rmsnorm(oa, subln_g) * (1.0 - lambda_init)
    heads_b = lambda a: a.reshape(b, s, B_HEADS, HEAD_DIM)
    ob = stick_breaking(heads_b(qb), heads_b(kb), heads_b(vb))
    mixed = jnp.concatenate([oa.reshape(b, s, A_V), ob.reshape(b, s, B_W)], axis=-1)
    return mixed @ w_out


def odd_mixer(h, w_in, conv_w, conv_b, ln_g, ln_b, rel_table, w_out):
    b, s, _ = h.shape
    c_val, c_gate, qd, kd, vd = _split(h @ w_in, ODD_SPLITS)
    u = c_val * jax.nn.sigmoid(c_gate)
    c = lax.conv_general_dilated(
        u, conv_w.astype(u.dtype)[:, None, :], window_strides=(1,),
        padding=[(CONV_WIDTH - 1, 0)], dimension_numbers=('NWC', 'WIO', 'NWC'),
        feature_group_count=C_CHANNELS) + conv_b
    c = jax.nn.silu(layernorm(c, ln_g, ln_b))
    heads_d = lambda a: a.reshape(b, s, D_HEADS, HEAD_DIM)
    od = chunk_rel_attention(heads_d(qd), heads_d(kd), heads_d(vd), rel_table)
    mixed = jnp.concatenate([c, od.reshape(b, s, D_W)], axis=-1)
    return mixed @ w_out


def swiglu(h, w_gate, w_up, w_down):
    return (jax.nn.silu(h @ w_gate) * (h @ w_up)) @ w_down


def moe_swiglu(h, w_router, w_gate, w_up, w_down):
    b, s, d = h.shape
    t = h.reshape(b * s, d)
    logits = (t @ w_router).astype(jnp.float32)
    top_vals, top_idx = lax.top_k(logits, TOP_K)
    gates = jax.nn.softmax(top_vals, axis=-1)
    combine = jnp.sum(jax.nn.one_hot(top_idx, N_EXPERTS, dtype=jnp.float32) * gates[..., None], axis=1)
    combine = combine.astype(t.dtype)
    out = jnp.zeros_like(t)
    for e in range(N_EXPERTS):
        out = out + combine[:, e:e + 1] * swiglu(t, w_gate[e], w_up[e], w_down[e])
    return out.reshape(b, s, d)


def setup_inputs(seed: int = 0) -> dict:
    key = jax.random.key(seed)
    ks = iter(jax.random.split(key, 32))
    nrm = lambda shape, scale: jax.random.normal(next(ks), shape, jnp.float32) * scale
    gain = lambda shape: 1.0 + nrm(shape, 0.02)
    P = N_PAIR
    return {
        'x': nrm((BATCH, SEQ, D_MODEL), 1.0),
        'ev_norm_mix': gain((P, D_MODEL)),
        'ev_w_in': nrm((P, D_MODEL, EVEN_IN), D_MODEL ** -0.5),
        'ev_lambda_q1': nrm((P, HEAD_DIM), 0.1),
        'ev_lambda_k1': nrm((P, HEAD_DIM), 0.1),
        'ev_lambda_q2': nrm((P, HEAD_DIM), 0.1),
        'ev_lambda_k2': nrm((P, HEAD_DIM), 0.1),
        'ev_subln': gain((P, A_VDIM)),
        'ev_w_out': nrm((P, EVEN_MIX, D_MODEL), EVEN_MIX ** -0.5),
        'ev_norm_ffn': gain((P, D_MODEL)),
        'ev_ffn_gate': nrm((P, D_MODEL, D_FF), D_MODEL ** -0.5),
        'ev_ffn_up': nrm((P, D_MODEL, D_FF), D_MODEL ** -0.5),
        'ev_ffn_down': nrm((P, D_FF, D_MODEL), D_FF ** -0.5),
        'od_norm_mix': gain((P, D_MODEL)),
        'od_w_in': nrm((P, D_MODEL, ODD_IN), D_MODEL ** -0.5),
        'od_conv_w': nrm((P, CONV_WIDTH, C_CHANNELS), CONV_WIDTH ** -0.5),
        'od_conv_b': nrm((P, C_CHANNELS), 0.02),
        'od_conv_ln_g': gain((P, C_CHANNELS)),
        'od_conv_ln_b': nrm((P, C_CHANNELS), 0.02),
        'od_rel_bias': nrm((P, D_HEADS, 2 * MAX_REL + 1), 0.2),
        'od_w_out': nrm((P, ODD_MIX, D_MODEL), ODD_MIX ** -0.5),
        'od_norm_ffn': gain((P, D_MODEL)),
        'od_router': nrm((P, D_MODEL, N_EXPERTS), D_MODEL ** -0.5),
        'od_exp_gate': nrm((P, N_EXPERTS, D_MODEL, D_FF_EXPERT), D_MODEL ** -0.5),
        'od_exp_up': nrm((P, N_EXPERTS, D_MODEL, D_FF_EXPERT), D_MODEL ** -0.5),
        'od_exp_down': nrm((P, N_EXPERTS, D_FF_EXPERT, D_MODEL), D_FF_EXPERT ** -0.5),
        'final_norm': gain((D_MODEL,)),
    }


def reference(x, ev_norm_mix, ev_w_in, ev_lambda_q1, ev_lambda_k1, ev_lambda_q2, ev_lambda_k2,
              ev_subln, ev_w_out, ev_norm_ffn, ev_ffn_gate, ev_ffn_up, ev_ffn_down,
              od_norm_mix, od_w_in, od_conv_w, od_conv_b, od_conv_ln_g, od_conv_ln_b, od_rel_bias,
              od_w_out, od_norm_ffn, od_router, od_exp_gate, od_exp_up, od_exp_down, final_norm):
    for layer in range(DEPTH):
        i = layer // 2
        if layer % 2 == 0:
            lambda_init = 0.8 - 0.6 * math.exp(-0.3 * layer)
            x = x + even_mixer(rmsnorm(x, ev_norm_mix[i]), ev_w_in[i], ev_lambda_q1[i], ev_lambda_k1[i],
                               ev_lambda_q2[i], ev_lambda_k2[i], ev_subln[i], ev_w_out[i], lambda_init)
            x = x + swiglu(rmsnorm(x, ev_norm_ffn[i]), ev_ffn_gate[i], ev_ffn_up[i], ev_ffn_down[i])
        else:
            x = x + odd_mixer(rmsnorm(x, od_norm_mix[i]), od_w_in[i], od_conv_w[i], od_conv_b[i],
                              od_conv_ln_g[i], od_conv_ln_b[i], od_rel_bias[i], od_w_out[i])
            x = x + moe_swiglu(rmsnorm(x, od_norm_ffn[i]), od_router[i], od_exp_gate[i],
                               od_exp_up[i], od_exp_down[i])
    return rmsnorm(x, final_norm)
```

```python
import functools
import math

import numpy as np
import jax
import jax.numpy as jnp
from jax import lax
from jax.experimental import pallas as pl
from jax.experimental.pallas import tpu as pltpu

F32 = jnp.float32
BF16 = jnp.bfloat16
I32 = jnp.int32

HEAD_DIM = 64
CHUNK = 64
ROPE_THETA = 10000.0
EPS = 1e-6
A_HEADS = 4
B_HEADS = 8
C_CHANNELS = 512
CONV_WIDTH = 31
D_HEADS = 8
LEFT_CHUNKS = 8
BAND = (LEFT_CHUNKS + 1) * CHUNK
MAX_REL = 128
N_EXPERTS = 8
TOP_K = 2

LANES = 128
NEG = -1e30
VMEM_LIMIT = 56 * 1024 * 1024


def _cparams(sem):
    return pltpu.CompilerParams(dimension_semantics=sem, vmem_limit_bytes=VMEM_LIMIT)


def _nt_dot(a, b):
    return lax.dot_general(a, b, (((1,), (1,)), ((), ())), preferred_element_type=F32)


def _norm_proj_kernel(x_ref, g_ref, w_ref, cos_ref, sin_ref, o_ref, *, col_chunk, rope_cols):
    x = x_ref[...]
    xn = (x * lax.rsqrt(jnp.mean(x * x, axis=-1, keepdims=True) + EPS) * g_ref[...]).astype(BF16)
    n_out = o_ref.shape[1]
    if rope_cols:
        cos = cos_ref[...]
        sin = sin_ref[...]
        lane = lax.broadcasted_iota(I32, cos.shape, 1)
        first_half = (lane % HEAD_DIM) < (HEAD_DIM // 2)
    for c0 in range(0, n_out, col_chunk):
        r = jnp.dot(xn, w_ref[:, c0:c0 + col_chunk], preferred_element_type=F32)
        if c0 < rope_cols:
            parts = []
            for l0 in range(0, col_chunk, LANES):
                seg = r[:, l0:l0 + LANES]
                partner = jnp.where(first_half,
                                    pltpu.roll(seg, LANES - HEAD_DIM // 2, 1),
                                    pltpu.roll(seg, HEAD_DIM // 2, 1))
                parts.append(seg * cos + partner * sin)
            r = jnp.concatenate(parts, axis=1)
        o_ref[:, c0:c0 + col_chunk] = r.astype(o_ref.dtype)


def _rope_tables(seq):
    half = HEAD_DIM // 2
    inv_freq = ROPE_THETA ** (-jnp.arange(half, dtype=F32) * 2.0 / HEAD_DIM)
    ang = jnp.arange(seq, dtype=F32)[:, None] * inv_freq[None, :]
    cos, sin = jnp.cos(ang), jnp.sin(ang)
    cos_t = jnp.tile(jnp.concatenate([cos, cos], axis=1), (1, LANES // HEAD_DIM))
    sin_t = jnp.tile(jnp.concatenate([-sin, sin], axis=1), (1, LANES // HEAD_DIM))
    return cos_t, sin_t


def norm_proj(x2, g, w_bf16, seq, rope_cols=0, tm=512):
    t, d = x2.shape
    n_out = w_bf16.shape[1]
    tm = min(tm, seq)
    col_chunk = 512
    assert t % tm == 0 and seq % tm == 0 and n_out % col_chunk == 0 and rope_cols % col_chunk == 0
    cos_t, sin_t = _rope_tables(seq)
    tiles_per_seq = seq // tm
    return pl.pallas_call(
        functools.partial(_norm_proj_kernel, col_chunk=col_chunk, rope_cols=rope_cols),
        out_shape=jax.ShapeDtypeStruct((t, n_out), BF16),
        grid=(t // tm,),
        in_specs=[
            pl.BlockSpec((tm, d), lambda i: (i, 0)),
            pl.BlockSpec((1, d), lambda i: (0, 0)),
            pl.BlockSpec((d, n_out), lambda i: (0, 0)),
            pl.BlockSpec((tm, LANES), lambda i: (i % tiles_per_seq, 0)),
            pl.BlockSpec((tm, LANES), lambda i: (i % tiles_per_seq, 0)),
        ],
        out_specs=pl.BlockSpec((tm, n_out), lambda i: (i, 0)),
        compiler_params=_cparams(("parallel",)),
        name="norm_proj",
    )(x2, g.reshape(1, d), w_bf16, cos_t, sin_t)


def _diff_attn_kernel(q_ref, k_ref, v_ref, lq1_ref, lk1_ref, lq2_ref, lk2_ref, g_ref, o_ref,
                      q2_scr, m_scr, l_scr, acc_scr, *, tq, lambda_init):
    i = pl.program_id(2)
    lane = lax.broadcasted_iota(I32, (tq, LANES), 1)
    qs = q_ref[...] * (HEAD_DIM ** -0.5)
    zero = jnp.zeros_like(qs)
    q2_scr[0:tq, :] = jnp.where(lane < HEAD_DIM, qs, zero)
    q2_scr[tq:, :] = jnp.where(lane >= HEAD_DIM, qs, zero)
    m_scr[...] = jnp.full(m_scr.shape, NEG, F32)
    l_scr[...] = jnp.zeros(l_scr.shape, F32)
    acc_scr[...] = jnp.zeros(acc_scr.shape, F32)

    def step(j, masked):
        start = pl.multiple_of(j * tq, tq)
        k = k_ref[pl.ds(start, tq), :]
        v = v_ref[pl.ds(start, tq), :]
        s = _nt_dot(q2_scr[...], k)
        if masked:
            row = lax.broadcasted_iota(I32, s.shape, 0)
            col = lax.broadcasted_iota(I32, s.shape, 1)
            s = jnp.where(col // CHUNK <= (row % tq) // CHUNK, s, NEG)
        m_old = m_scr[...]
        m_new = jnp.maximum(m_old, jnp.max(s, axis=1, keepdims=True))
        alpha = jnp.exp(m_old - m_new)
        p = jnp.exp(s - m_new)
        l_scr[...] = alpha * l_scr[...] + jnp.sum(p, axis=1, keepdims=True)
        acc_scr[...] = alpha * acc_scr[...] + jnp.dot(p.astype(BF16), v, preferred_element_type=F32)
        m_scr[...] = m_new

    def body(j, carry):
        step(j, False)
        return carry

    lax.fori_loop(0, i, body, 0)
    step(i, True)

    o = acc_scr[...] / l_scr[...]
    lam = (jnp.exp(jnp.sum(lq1_ref[...] * lk1_ref[...], axis=1, keepdims=True))
           - jnp.exp(jnp.sum(lq2_ref[...] * lk2_ref[...], axis=1, keepdims=True)) + lambda_init)
    od = o[0:tq, :] - lam * o[tq:, :]
    y = od * lax.rsqrt(jnp.mean(od * od, axis=-1, keepdims=True) + EPS) * g_ref[...]
    o_ref[...] = (y * (1.0 - lambda_init)).astype(o_ref.dtype)


def diff_attention(proj, lq1, lk1, lq2, lk2, subln_g, batch, seq, lambda_init, tq=256):
    t = proj.shape[0]
    tq = min(tq, seq)
    nq = seq // tq
    k_off = (A_HEADS * 2 * HEAD_DIM) // LANES
    v_off = 2 * k_off
    vec = lambda a: a.reshape(1, -1).astype(F32)
    small = lambda n: pl.BlockSpec((1, n), lambda b, h, i: (0, 0))
    return pl.pallas_call(
        functools.partial(_diff_attn_kernel, tq=tq, lambda_init=lambda_init),
        out_shape=jax.ShapeDtypeStruct((t, A_HEADS * 2 * HEAD_DIM), BF16),
        grid=(batch, A_HEADS, nq),
        in_specs=[
            pl.BlockSpec((tq, LANES), lambda b, h, i: (b * nq + i, h)),
            pl.BlockSpec((seq, LANES), lambda b, h, i: (b, k_off + h)),
            pl.BlockSpec((seq, LANES), lambda b, h, i: (b, v_off + h)),
            small(HEAD_DIM), small(HEAD_DIM), small(HEAD_DIM), small(HEAD_DIM), small(2 * HEAD_DIM),
        ],
        out_specs=pl.BlockSpec((tq, LANES), lambda b, h, i: (b * nq + i, h)),
        scratch_shapes=[
            pltpu.VMEM((2 * tq, LANES), BF16),
            pltpu.VMEM((2 * tq, 1), F32),
            pltpu.VMEM((2 * tq, 1), F32),
            pltpu.VMEM((2 * tq, LANES), F32),
        ],
        compiler_params=_cparams(("parallel", "parallel", "arbitrary")),
        name="diff_attn",
    )(proj, proj, proj, vec(lq1), vec(lk1), vec(lq2), vec(lk2), vec(subln_g))


def _stick_kernel(q_ref, k_ref, v_ref, o_ref, q2_scr, tri_scr, c_scr, acc_scr, *, tq):
    i = pl.program_id(2)
    lane = lax.broadcasted_iota(I32, (tq, LANES), 1)
    qs = q_ref[...] * (HEAD_DIM ** -0.5)
    zero = jnp.zeros_like(qs)
    q2_scr[0:tq, :] = jnp.where(lane < HEAD_DIM, qs, zero)
    q2_scr[tq:, :] = jnp.where(lane >= HEAD_DIM, qs, zero)
    r_i = lax.broadcasted_iota(I32, (tq, tq), 0)
    c_i = lax.broadcasted_iota(I32, (tq, tq), 1)
    tri_scr[...] = jnp.where(r_i > c_i, 1.0, 0.0).astype(BF16)
    c_scr[...] = jnp.zeros(c_scr.shape, F32)
    acc_scr[...] = jnp.zeros(acc_scr.shape, F32)

    def step(j, masked):
        start = pl.multiple_of(j * tq, tq)
        k = k_ref[pl.ds(start, tq), :]
        v = v_ref[pl.ds(start, tq), :]
        z = _nt_dot(q2_scr[...], k)
        sp = jnp.maximum(z, 0.0) + jnp.log(1.0 + jnp.exp(-jnp.abs(z)))
        if masked:
            row = lax.broadcasted_iota(I32, z.shape, 0)
            col = lax.broadcasted_iota(I32, z.shape, 1)
            valid = col < (row % tq)
            l = jnp.where(valid, -sp, 0.0)
        else:
            l = -sp
        l_hi = l.astype(BF16)
        l_lo = (l - l_hi.astype(F32)).astype(BF16)
        tri = tri_scr[...]
        after = (jnp.dot(l_hi, tri, preferred_element_type=F32)
                 + jnp.dot(l_lo, tri, preferred_element_type=F32))
        w = jnp.exp(z - sp + after + c_scr[...])
        if masked:
            w = jnp.where(valid, w, 0.0)
        acc_scr[...] += jnp.dot(w.astype(BF16), v, preferred_element_type=F32)
        c_scr[...] += jnp.sum(l, axis=1, keepdims=True)

    step(i, True)

    def body(n, carry):
        step(i - 1 - n, False)
        return carry

    lax.fori_loop(0, i, body, 0)
    acc = acc_scr[...]
    o_ref[...] = jnp.where(lane < HEAD_DIM, acc[0:tq, :], acc[tq:, :]).astype(o_ref.dtype)


def stick_breaking(proj, batch, seq, tq=256):
    t = proj.shape[0]
    tq = min(tq, seq)
    nq = seq // tq
    pairs = (B_HEADS * HEAD_DIM) // LANES
    q_off = 3 * pairs
    k_off = 4 * pairs
    v_off = 5 * pairs
    return pl.pallas_call(
        functools.partial(_stick_kernel, tq=tq),
        out_shape=jax.ShapeDtypeStruct((t, B_HEADS * HEAD_DIM), BF16),
        grid=(batch, pairs, nq),
        in_specs=[
            pl.BlockSpec((tq, LANES), lambda b, p, i: (b * nq + i, q_off + p)),
            pl.BlockSpec((seq, LANES), lambda b, p, i: (b, k_off + p)),
            pl.BlockSpec((seq, LANES), lambda b, p, i: (b, v_off + p)),
        ],
        out_specs=pl.BlockSpec((tq, LANES), lambda b, p, i: (b * nq + i, p)),
        scratch_shapes=[
            pltpu.VMEM((2 * tq, LANES), BF16),
            pltpu.VMEM((tq, tq), BF16),
            pltpu.VMEM((2 * tq, 1), F32),
            pltpu.VMEM((2 * tq, LANES), F32),
        ],
        compiler_params=_cparams(("parallel", "parallel", "arbitrary")),
        name="stick_attn",
    )(proj, proj, proj)


def _out_proj_kernel(x_ref, a_ref, b_ref, w_ref, o_ref):
    half = a_ref.shape[1]
    o_ref[...] = (x_ref[...]
                  + jnp.dot(a_ref[...], w_ref[0:half, :], preferred_element_type=F32)
                  + jnp.dot(b_ref[...], w_ref[half:, :], preferred_element_type=F32))


def out_proj_residual(x2, a, b, w_bf16, tm=512):
    t, d = x2.shape
    half = a.shape[1]
    tm = min(tm, t)
    return pl.pallas_call(
        _out_proj_kernel,
        out_shape=jax.ShapeDtypeStruct((t, d), F32),
        grid=(t // tm,),
        in_specs=[
            pl.BlockSpec((tm, d), lambda i: (i, 0)),
            pl.BlockSpec((tm, half), lambda i: (i, 0)),
            pl.BlockSpec((tm, half), lambda i: (i, 0)),
            pl.BlockSpec((2 * half, d), lambda i: (0, 0)),
        ],
        out_specs=pl.BlockSpec((tm, d), lambda i: (i, 0)),
        compiler_params=_cparams(("parallel",)),
        name="out_proj",
    )(x2, a, b, w_bf16)


def _swiglu_chunks(xn, wg_ref, wu_ref, h_scr, ff_chunk):
    d_ff = h_scr.shape[1]
    for c0 in range(0, d_ff, ff_chunk):
        g = jnp.dot(xn, wg_ref[:, c0:c0 + ff_chunk], preferred_element_type=F32)
        u = jnp.dot(xn, wu_ref[:, c0:c0 + ff_chunk], preferred_element_type=F32)
        h_scr[:, c0:c0 + ff_chunk] = (g * jax.nn.sigmoid(g) * u).astype(BF16)


def _ffn_kernel(x_ref, g_ref, wg_ref, wu_ref, wd_ref, o_ref, h_scr, *, ff_chunk):
    x = x_ref[...]
    xn = (x * lax.rsqrt(jnp.mean(x * x, axis=-1, keepdims=True) + EPS) * g_ref[...]).astype(BF16)
    _swiglu_chunks(xn, wg_ref, wu_ref, h_scr, ff_chunk)
    o_ref[...] = x + jnp.dot(h_scr[...], wd_ref[...], preferred_element_type=F32)


def ffn_residual(x2, g, wg, wu, wd, tm=512):
    t, d = x2.shape
    d_ff = wg.shape[1]
    tm = min(tm, t)
    ff_chunk = 256
    assert d_ff % ff_chunk == 0
    resident = lambda shape: pl.BlockSpec(shape, lambda i: (0, 0), pipeline_mode=pl.Buffered(1))
    return pl.pallas_call(
        functools.partial(_ffn_kernel, ff_chunk=ff_chunk),
        out_shape=jax.ShapeDtypeStruct((t, d), F32),
        grid=(t // tm,),
        in_specs=[
            pl.BlockSpec((tm, d), lambda i: (i, 0)),
            pl.BlockSpec((1, d), lambda i: (0, 0)),
            resident((d, d_ff)), resident((d, d_ff)), resident((d_ff, d)),
        ],
        out_specs=pl.BlockSpec((tm, d), lambda i: (i, 0)),
        scratch_shapes=[pltpu.VMEM((tm, d_ff), BF16)],
        compiler_params=_cparams(("parallel",)),
        name="ffn",
    )(x2, g.reshape(1, d), wg, wu, wd)


CONV_HALO = 32


def _conv_kernel(val_ref, gate_ref, pval_ref, pgate_ref, w_ref, b_ref, lg_ref, lb_ref, o_ref,
                 u_scr, c_scr, *, ts, tiles_per_seq, row_blk):
    i = pl.program_id(0)
    glu = lambda v, g: v.astype(F32) * jax.nn.sigmoid(g.astype(F32))
    keep = jnp.where(i % tiles_per_seq == 0, 0.0, 1.0)
    u_scr[0:CONV_HALO, :] = glu(pval_ref[...], pgate_ref[...]) * keep
    u_scr[CONV_HALO:, :] = glu(val_ref[...], gate_ref[...])
    shift = CONV_HALO - (CONV_WIDTH - 1)
    n_ch = val_ref.shape[1]
    for c0 in range(0, n_ch, LANES):
        for r0 in range(0, ts, row_blk):
            acc = jnp.broadcast_to(b_ref[:, c0:c0 + LANES], (row_blk, LANES))
            for j in range(CONV_WIDTH):
                acc = acc + w_ref[j:j + 1, c0:c0 + LANES] * u_scr[r0 + shift + j:r0 + shift + j + row_blk,
                                                                   c0:c0 + LANES]
            c_scr[r0:r0 + row_blk, c0:c0 + LANES] = acc
    c = c_scr[...]
    mu = jnp.mean(c, axis=-1, keepdims=True)
    xc = c - mu
    var = jnp.mean(xc * xc, axis=-1, keepdims=True)
    y = xc * lax.rsqrt(var + EPS) * lg_ref[...] + lb_ref[...]
    o_ref[...] = (y * jax.nn.sigmoid(y)).astype(o_ref.dtype)


def conv_module(proj, conv_w, conv_b, ln_g, ln_b, seq, ts=256):
    t = proj.shape[0]
    ts = min(ts, seq)
    n_ch = C_CHANNELS
    halo_blocks = ts // CONV_HALO
    row = lambda a: a.reshape(1, n_ch).astype(F32)
    small = pl.BlockSpec((1, n_ch), lambda i: (0, 0))
    prev = lambda col: pl.BlockSpec((CONV_HALO, n_ch), lambda i: (jnp.maximum(i * halo_blocks - 1, 0), col))
    return pl.pallas_call(
        functools.partial(_conv_kernel, ts=ts, tiles_per_seq=seq // ts, row_blk=64),
        out_shape=jax.ShapeDtypeStruct((t, n_ch), BF16),
        grid=(t // ts,),
        in_specs=[
            pl.BlockSpec((ts, n_ch), lambda i: (i, 0)),
            pl.BlockSpec((ts, n_ch), lambda i: (i, 1)),
            prev(0), prev(1),
            pl.BlockSpec((CONV_WIDTH, n_ch), lambda i: (0, 0)),
            small, small, small,
        ],
        out_specs=pl.BlockSpec((ts, n_ch), lambda i: (i, 0)),
        scratch_shapes=[pltpu.VMEM((ts + CONV_HALO, n_ch), F32), pltpu.VMEM((ts, n_ch), F32)],
        compiler_params=_cparams(("parallel",)),
        name="conv_module",
    )(proj, proj, proj, proj, conv_w.astype(F32), row(conv_b), row(ln_g), row(ln_b))


def _rel_bias_kernel(tbl_ref, o_ref):
    p = pl.program_id(0)
    iq = lax.broadcasted_iota(I32, (CHUNK, BAND), 0)
    kk = lax.broadcasted_iota(I32, (CHUNK, BAND), 1)
    idx = jnp.clip(iq - (kk - LEFT_CHUNKS * CHUNK), -MAX_REL, MAX_REL) + MAX_REL
    for hh in range(2):
        h = 2 * p + hh

        def body(j, acc):
            return acc + jnp.where(idx == j, tbl_ref[h, j], 0.0)

        o_ref[0, hh * CHUNK:(hh + 1) * CHUNK, :] = lax.fori_loop(
            0, 2 * MAX_REL + 1, body, jnp.zeros((CHUNK, BAND), F32))


def rel_bias(rel_table):
    pairs = D_HEADS // 2
    return pl.pallas_call(
        _rel_bias_kernel,
        out_shape=jax.ShapeDtypeStruct((pairs, 2 * CHUNK, BAND), F32),
        grid=(pairs,),
        in_specs=[pl.BlockSpec(memory_space=pltpu.SMEM)],
        out_specs=pl.BlockSpec((1, 2 * CHUNK, BAND), lambda p: (p, 0, 0)),
        compiler_params=_cparams(("parallel",)),
        name="rel_bias",
    )(rel_table.astype(F32))


def _chunk_attn_kernel(q_ref, k_ref, v_ref, bias_ref, o_ref, kpad, vpad, *, group, seq):
    i = pl.program_id(2)
    pad = LEFT_CHUNKS * CHUNK

    @pl.when(i == 0)
    def _():
        kpad[0:pad, :] = jnp.zeros((pad, LANES), BF16)
        vpad[0:pad, :] = jnp.zeros((pad, LANES), BF16)
        kpad[pad:, :] = k_ref[...]
        vpad[pad:, :] = v_ref[...]

    lane = lax.broadcasted_iota(I32, (CHUNK, LANES), 1)
    col_chunk = lax.broadcasted_iota(I32, (2 * CHUNK, BAND), 1) // CHUNK
    bias = bias_ref[0]
    for g in range(group):
        c = i * group + g
        qs = q_ref[g * CHUNK:(g + 1) * CHUNK, :] * (HEAD_DIM ** -0.5)
        zero = jnp.zeros_like(qs)
        q2 = jnp.concatenate([jnp.where(lane < HEAD_DIM, qs, zero),
                              jnp.where(lane >= HEAD_DIM, qs, zero)], axis=0)
        start = pl.multiple_of(c * CHUNK, CHUNK)
        kw = kpad[pl.ds(start, BAND), :]
        vw = vpad[pl.ds(start, BAND), :]
        s = _nt_dot(q2, kw) + bias
        s = jnp.where(col_chunk >= LEFT_CHUNKS - c, s, NEG)
        m = jnp.max(s, axis=1, keepdims=True)
        p = jnp.exp(s - m)
        denom = jnp.sum(p, axis=1, keepdims=True)
        o = jnp.dot(p.astype(BF16), vw, preferred_element_type=F32) / denom
        o_ref[g * CHUNK:(g + 1) * CHUNK, :] = jnp.where(lane < HEAD_DIM, o[0:CHUNK, :],
                                                        o[CHUNK:, :]).astype(o_ref.dtype)


def chunk_rel_attention(proj, bias, batch, seq, group=4):
    t = proj.shape[0]
    pairs = D_HEADS // 2
    q_off = (2 * C_CHANNELS) // LANES
    k_off = q_off + pairs
    v_off = k_off + pairs
    tq = group * CHUNK
    nq = seq // tq
    return pl.pallas_call(
        functools.partial(_chunk_attn_kernel, group=group, seq=seq),
        out_shape=jax.ShapeDtypeStruct((t, D_HEADS * HEAD_DIM), BF16),
        grid=(batch, pairs, nq),
        in_specs=[
            pl.BlockSpec((tq, LANES), lambda b, p, i: (b * nq + i, q_off + p)),
            pl.BlockSpec((seq, LANES), lambda b, p, i: (b, k_off + p)),
            pl.BlockSpec((seq, LANES), lambda b, p, i: (b, v_off + p)),
            pl.BlockSpec((1, 2 * CHUNK, BAND), lambda b, p, i: (p, 0, 0)),
        ],
        out_specs=pl.BlockSpec((tq, LANES), lambda b, p, i: (b * nq + i, p)),
        scratch_shapes=[pltpu.VMEM((seq + LEFT_CHUNKS * CHUNK, LANES), BF16),
                        pltpu.VMEM((seq + LEFT_CHUNKS * CHUNK, LANES), BF16)],
        compiler_params=_cparams(("parallel", "parallel", "arbitrary")),
        name="chunk_attn",
    )(proj, proj, proj, bias)


def _router_kernel(x_ref, g_ref, wr_ref, h_ref, meta_ref, cnt_ref, carry_scr, tri_scr, *, tr):
    i = pl.program_id(0)

    @pl.when(i == 0)
    def _():
        carry_scr[...] = jnp.zeros(carry_scr.shape, F32)
        r_i = lax.broadcasted_iota(I32, (tr, tr), 0)
        c_i = lax.broadcasted_iota(I32, (tr, tr), 1)
        tri_scr[...] = jnp.where(r_i < c_i, 1.0, 0.0).astype(BF16)

    x = x_ref[...]
    hn = x * lax.rsqrt(jnp.mean(x * x, axis=-1, keepdims=True) + EPS) * g_ref[...]
    h_ref[...] = hn
    h_hi = hn.astype(BF16)
    h_lo = (hn - h_hi.astype(F32)).astype(BF16)
    w = wr_ref[...]
    w_hi = w.astype(BF16)
    w_lo = (w - w_hi.astype(F32)).astype(BF16)
    logits = _nt_dot(w_hi, h_hi) + (_nt_dot(w_hi, h_lo) + _nt_dot(w_lo, h_hi))
    eidx = lax.broadcasted_iota(I32, logits.shape, 0).astype(F32)
    m1 = jnp.max(logits, axis=0, keepdims=True)
    i1 = jnp.min(jnp.where(logits == m1, eidx, float(N_EXPERTS)), axis=0, keepdims=True)
    rest = jnp.where(eidx == i1, -jnp.inf, logits)
    m2 = jnp.max(rest, axis=0, keepdims=True)
    i2 = jnp.min(jnp.where(rest == m2, eidx, float(N_EXPERTS)), axis=0, keepdims=True)
    e = jnp.exp(m2 - m1)
    g1 = 1.0 / (1.0 + e)
    g2 = e / (1.0 + e)
    sel1 = eidx == i1
    sel2 = eidx == i2
    onehot = jnp.where(sel1 | sel2, 1.0, 0.0)
    rank = jnp.dot(onehot.astype(BF16), tri_scr[...], preferred_element_type=F32) + carry_scr[...]
    r1 = jnp.sum(jnp.where(sel1, rank, 0.0), axis=0, keepdims=True)
    r2 = jnp.sum(jnp.where(sel2, rank, 0.0), axis=0, keepdims=True)
    carry = carry_scr[...] + jnp.sum(onehot, axis=1, keepdims=True)
    carry_scr[...] = carry
    zero = jnp.zeros_like(g1)
    meta_ref[0] = jnp.concatenate([i1, i2, r1, r2, g1, g2, zero, zero], axis=0)
    cnt_ref[...] = jnp.broadcast_to(carry, cnt_ref.shape)


def router(x2, g, w_router, tr=512):
    t, d = x2.shape
    tr = min(tr, t)
    return pl.pallas_call(
        functools.partial(_router_kernel, tr=tr),
        out_shape=(jax.ShapeDtypeStruct((t, d), F32),
                   jax.ShapeDtypeStruct((t // tr, 8, tr), F32),
                   jax.ShapeDtypeStruct((N_EXPERTS, LANES), F32)),
        grid=(t // tr,),
        in_specs=[
            pl.BlockSpec((tr, d), lambda i: (i, 0)),
            pl.BlockSpec((1, d), lambda i: (0, 0)),
            pl.BlockSpec((N_EXPERTS, d), lambda i: (0, 0)),
        ],
        out_specs=(pl.BlockSpec((tr, d), lambda i: (i, 0)),
                   pl.BlockSpec((1, 8, tr), lambda i: (i, 0, 0)),
                   pl.BlockSpec((N_EXPERTS, LANES), lambda i: (0, 0))),
        scratch_shapes=[pltpu.VMEM((N_EXPERTS, 1), F32), pltpu.VMEM((tr, tr), BF16)],
        compiler_params=_cparams(("arbitrary",)),
        name="router",
    )(x2, g.reshape(1, d), w_router.T.astype(F32))


def _row_move_kernel(idx_ref, src_ref, init_ref, dst_ref, sem, *, rows, scatter):
    del init_ref
    base = pl.program_id(0) * rows

    def copy(r):
        j = idx_ref[0, 0, r]
        if scatter:
            return pltpu.make_async_copy(src_ref.at[pl.ds(base + r, 1)], dst_ref.at[pl.ds(j, 1)], sem)
        return pltpu.make_async_copy(src_ref.at[pl.ds(j, 1)], dst_ref.at[pl.ds(base + r, 1)], sem)

    def issue(r, carry):
        copy(r).start()
        return carry

    lax.fori_loop(0, rows, issue, 0)
    pltpu.make_async_copy(dst_ref.at[pl.ds(0, rows)], dst_ref.at[pl.ds(0, rows)], sem).wait()


def row_move(src, idx, dst_init, scatter, rows=512):
    n = idx.shape[0]
    rows = min(rows, n)
    assert n % rows == 0
    return pl.pallas_call(
        functools.partial(_row_move_kernel, rows=rows, scatter=scatter),
        out_shape=jax.ShapeDtypeStruct(dst_init.shape, dst_init.dtype),
        grid=(n // rows,),
        in_specs=[
            pl.BlockSpec((1, 1, rows), lambda i: (i, 0, 0), memory_space=pltpu.SMEM),
            pl.BlockSpec(memory_space=pl.ANY),
            pl.BlockSpec(memory_space=pl.ANY),
        ],
        out_specs=pl.BlockSpec(memory_space=pl.ANY),
        scratch_shapes=[pltpu.SemaphoreType.DMA(())],
        input_output_aliases={2: 0},
        compiler_params=_cparams(("arbitrary",)),
        name="row_scatter" if scatter else "row_gather",
    )(idx.reshape(n // rows, 1, rows), src, dst_init)


def _expert_ffn_kernel(te_ref, nt_ref, x_ref, wg_ref, wu_ref, wd_ref, o_ref, h_scr, acc_scr, *, ff_chunk):
    n = pl.program_id(0)
    f = pl.program_id(1)

    @pl.when(n < nt_ref[0])
    def _():
        xn = x_ref[...].astype(BF16)
        _swiglu_chunks(xn, wg_ref.at[0], wu_ref.at[0], h_scr, ff_chunk)
        part = jnp.dot(h_scr[...], wd_ref[0], preferred_element_type=F32)

        @pl.when(f == 0)
        def _():
            acc_scr[...] = part

        @pl.when(f > 0)
        def _():
            acc_scr[...] += part

        @pl.when(f == pl.num_programs(1) - 1)
        def _():
            o_ref[...] = acc_scr[...]

    @pl.when((n >= nt_ref[0]) & (f == pl.num_programs(1) - 1))
    def _():
        o_ref[...] = jnp.zeros(o_ref.shape, o_ref.dtype)


def expert_ffn(xs, tile_expert, n_tiles_used, wg, wu, wd, tm, ff_split=2):
    nr, d = xs.shape
    d_ff = wg.shape[2]
    ff_blk = d_ff // ff_split
    ff_chunk = 256
    assert nr % tm == 0 and ff_blk % ff_chunk == 0
    grid_spec = pltpu.PrefetchScalarGridSpec(
        num_scalar_prefetch=2,
        grid=(nr // tm, ff_split),
        in_specs=[
            pl.BlockSpec((tm, d), lambda n, f, te, nt: (n, 0)),
            pl.BlockSpec((1, d, ff_blk), lambda n, f, te, nt: (te[n], 0, f)),
            pl.BlockSpec((1, d, ff_blk), lambda n, f, te, nt: (te[n], 0, f)),
            pl.BlockSpec((1, ff_blk, d), lambda n, f, te, nt: (te[n], f, 0)),
        ],
        out_specs=pl.BlockSpec((tm, d), lambda n, f, te, nt: (n, 0)),
        scratch_shapes=[pltpu.VMEM((tm, ff_blk), BF16), pltpu.VMEM((tm, d), F32)],
    )
    return pl.pallas_call(
        functools.partial(_expert_ffn_kernel, ff_chunk=ff_chunk),
        out_shape=jax.ShapeDtypeStruct((nr, d), F32),
        grid_spec=grid_spec,
        compiler_params=_cparams(("arbitrary", "arbitrary")),
        name="expert_ffn",
    )(tile_expert, n_tiles_used, xs, wg, wu, wd)


def _combine_kernel(x_ref, y1_ref, y2_ref, g1_ref, g2_ref, gn_ref, o_ref, *, final_norm):
    y = x_ref[...] + g1_ref[...] * y1_ref[...] + g2_ref[...] * y2_ref[...]
    if final_norm:
        y = y * lax.rsqrt(jnp.mean(y * y, axis=-1, keepdims=True) + EPS) * gn_ref[...]
    o_ref[...] = y


def combine(x2, y1, y2, g1, g2, gn, final_norm, tm=512):
    t, d = x2.shape
    tm = min(tm, t)
    big = pl.BlockSpec((tm, d), lambda i: (i, 0))
    col = pl.BlockSpec((tm, 1), lambda i: (i, 0))
    return pl.pallas_call(
        functools.partial(_combine_kernel, final_norm=final_norm),
        out_shape=jax.ShapeDtypeStruct((t, d), F32),
        grid=(t // tm,),
        in_specs=[big, big, big, col, col, pl.BlockSpec((1, d), lambda i: (0, 0))],
        out_specs=big,
        compiler_params=_cparams(("parallel",)),
        name="moe_combine",
    )(x2, y1, y2, g1, g2, gn.reshape(1, d))


def moe_residual(x2, g, w_router, wg, wu, wd, final_gain, final_norm, tm=512):
    t, d = x2.shape
    tm = min(tm, t)
    hn, meta, counts = router(x2, g, w_router)
    tr = meta.shape[2]
    field = lambda r: meta[:, r, :].reshape(t)
    idx1, idx2 = field(0).astype(I32), field(1).astype(I32)
    rank1, rank2 = field(2).astype(I32), field(3).astype(I32)
    gate1, gate2 = field(4), field(5)
    cnt = counts[:, 0].astype(I32)
    padded = ((cnt + tm - 1) // tm) * tm
    ends = jnp.cumsum(padded)
    offs = ends - padded
    pos1 = offs[idx1] + rank1
    pos2 = offs[idx2] + rank2
    n_rows = TOP_K * t + N_EXPERTS * tm
    tile_start = jnp.arange(n_rows // tm, dtype=I32) * tm
    tile_expert = jnp.minimum(jnp.sum(tile_start[:, None] >= ends[None, :], axis=1), N_EXPERTS - 1).astype(I32)
    n_tiles_used = (ends[-1:] // tm).astype(I32)

    xs = row_move(hn, pos1, jnp.zeros((n_rows, d), F32), scatter=True)
    xs = row_move(hn, pos2, xs, scatter=True)
    ys = expert_ffn(xs, tile_expert, n_tiles_used, wg, wu, wd, tm)
    y1 = row_move(ys, pos1, jnp.zeros((t, d), F32), scatter=False)
    y2 = row_move(ys, pos2, jnp.zeros((t, d), F32), scatter=False)
    return combine(x2, y1, y2, gate1.reshape(t, 1), gate2.reshape(t, 1), final_gain, final_norm)


def kernel(x, ev_norm_mix, ev_w_in, ev_lambda_q1, ev_lambda_k1, ev_lambda_q2, ev_lambda_k2, ev_subln, ev_w_out,
           ev_norm_ffn, ev_ffn_gate, ev_ffn_up, ev_ffn_down, od_norm_mix, od_w_in, od_conv_w, od_conv_b,
           od_conv_ln_g, od_conv_ln_b, od_rel_bias, od_w_out, od_norm_ffn, od_router, od_exp_gate, od_exp_up,
           od_exp_down, final_norm):
    batch, seq, d = x.shape
    depth = 2 * ev_w_in.shape[0]
    assert od_w_in.shape[0] * 2 == depth
    x2 =x.reshape(batch * seq, d)
    bf = lambda a: a.astype(BF16)
    for layer in range(depth):
        i = layer // 2
        if layer % 2 == 0:
            lambda_init = 0.8 - 0.6 * math.exp(-0.3 * layer)
            proj = norm_proj(x2, ev_norm_mix[i], bf(ev_w_in[i]), seq, rope_cols=2 * A_HEADS * 2 * HEAD_DIM)
            oa = diff_attention(proj, ev_lambda_q1[i], ev_lambda_k1[i], ev_lambda_q2[i], ev_lambda_k2[i],
                                ev_subln[i], batch, seq, lambda_init)
            ob = stick_breaking(proj, batch, seq)
            x2 = out_proj_residual(x2, oa, ob, bf(ev_w_out[i]))
            x2 = ffn_residual(x2, ev_norm_ffn[i], bf(ev_ffn_gate[i]), bf(ev_ffn_up[i]), bf(ev_ffn_down[i]))
        else:
            proj = norm_proj(x2, od_norm_mix[i], bf(od_w_in[i]), seq)
            c = conv_module(proj, od_conv_w[i], od_conv_b[i], od_conv_ln_g[i], od_conv_ln_b[i], seq)
            od = chunk_rel_attention(proj, rel_bias(od_rel_bias[i]), batch, seq)
            x2 = out_proj_residual(x2, c, od, bf(od_w_out[i]))
            x2 = moe_residual(x2, od_norm_ffn[i], od_router[i], bf(od_exp_gate[i]), bf(od_exp_up[i]),
                              bf(od_exp_down[i]), final_norm, final_norm=(layer == depth - 1))
    return x2.reshape(batch, seq, d)
```

```python
import functools
import math

import numpy as np
import jax
import jax.numpy as jnp
from jax import lax
from jax.experimental import pallas as pl
from jax.experimental.pallas import tpu as pltpu

F32 = jnp.float32
BF16 = jnp.bfloat16
I32 = jnp.int32

HEAD_DIM = 64
CHUNK = 64
ROPE_THETA = 10000.0
EPS = 1e-6
A_HEADS = 4
B_HEADS = 8
C_CHANNELS = 512
CONV_WIDTH = 31
D_HEADS = 8
LEFT_CHUNKS = 8
BAND = (LEFT_CHUNKS + 1) * CHUNK
MAX_REL = 128
N_EXPERTS = 8
TOP_K = 2

LANES = 128
NEG = -1e30
VMEM_LIMIT = 56 * 1024 * 1024


def _cparams(sem):
    return pltpu.CompilerParams(dimension_semantics=sem, vmem_limit_bytes=VMEM_LIMIT)


def _nt_dot(a, b):
    return lax.dot_general(a, b, (((1,), (1,)), ((), ())), preferred_element_type=F32)


def _norm_proj_kernel(x_ref, g_ref, w_ref, cos_ref, sin_ref, *rest, col_chunk, rope_cols, tk):
    if tk:
        wvt_ref, o_ref, vt_ref = rest
    else:
        (o_ref,) = rest
    x = x_ref[...]
    xn = (x * lax.rsqrt(jnp.mean(x * x, axis=-1, keepdims=True) + EPS) * g_ref[...]).astype(BF16)
    n_out = o_ref.shape[1]
    if tk:
        for r0 in range(0, wvt_ref.shape[0], col_chunk):
            res = _nt_dot(wvt_ref[r0:r0 + col_chunk, :], xn)
            for b in range(vt_ref.shape[0]):
                vt_ref[b, r0:r0 + col_chunk, :] = res[:, b * tk:(b + 1) * tk].astype(BF16)
    if rope_cols:
        cos = cos_ref[...]
        sin = sin_ref[...]
        lane = lax.broadcasted_iota(I32, cos.shape, 1)
        first_half = (lane % HEAD_DIM) < (HEAD_DIM // 2)
    for c0 in range(0, n_out, col_chunk):
        r = jnp.dot(xn, w_ref[:, c0:c0 + col_chunk], preferred_element_type=F32)
        if c0 < rope_cols:
            parts = []
            for l0 in range(0, col_chunk, LANES):
                seg = r[:, l0:l0 + LANES]
                partner = jnp.where(first_half,
                                    pltpu.roll(seg, LANES - HEAD_DIM // 2, 1),
                                    pltpu.roll(seg, HEAD_DIM // 2, 1))
                parts.append(seg * cos + partner * sin)
            r = jnp.concatenate(parts, axis=1)
        o_ref[:, c0:c0 + col_chunk] = r.astype(o_ref.dtype)


def _rope_tables(seq):
    half = HEAD_DIM // 2
    inv_freq = ROPE_THETA ** (-jnp.arange(half, dtype=F32) * 2.0 / HEAD_DIM)
    ang = jnp.arange(seq, dtype=F32)[:, None] * inv_freq[None, :]
    cos, sin = jnp.cos(ang), jnp.sin(ang)
    cos_t = jnp.tile(jnp.concatenate([cos, cos], axis=1), (1, LANES // HEAD_DIM))
    sin_t = jnp.tile(jnp.concatenate([-sin, sin], axis=1), (1, LANES // HEAD_DIM))
    return cos_t, sin_t


def norm_proj(x2, g, w_bf16, seq, rope_cols=0, wvt_bf16=None, tk=0, tm=512):
    t, d = x2.shape
    n_out = w_bf16.shape[1]
    tm = min(tm, seq)
    col_chunk = 512
    assert t % tm == 0 and seq % tm == 0 and n_out % col_chunk == 0 and rope_cols % col_chunk == 0
    cos_t, sin_t = _rope_tables(seq)
    tiles_per_seq = seq // tm
    in_specs = [
        pl.BlockSpec((tm, d), lambda i: (i, 0)),
        pl.BlockSpec((1, d), lambda i: (0, 0)),
        pl.BlockSpec((d, n_out), lambda i: (0, 0)),
        pl.BlockSpec((tm, LANES), lambda i: (i % tiles_per_seq, 0)),
        pl.BlockSpec((tm, LANES), lambda i: (i % tiles_per_seq, 0)),
    ]
    args = [x2, g.reshape(1, d), w_bf16, cos_t, sin_t]
    out_shape = jax.ShapeDtypeStruct((t, n_out), BF16)
    out_specs = pl.BlockSpec((tm, n_out), lambda i: (i, 0))
    if wvt_bf16 is not None:
        n_v = wvt_bf16.shape[0]
        tk = min(tk, tm)
        assert tm % tk == 0 and n_v % col_chunk == 0
        in_specs.append(pl.BlockSpec((n_v, d), lambda i: (0, 0)))
        args.append(wvt_bf16)
        out_shape = (out_shape, jax.ShapeDtypeStruct((t // tk, n_v, tk), BF16))
        out_specs = (out_specs, pl.BlockSpec((tm // tk, n_v, tk), lambda i: (i, 0, 0)))
    return pl.pallas_call(
        functools.partial(_norm_proj_kernel, col_chunk=col_chunk, rope_cols=rope_cols,
                          tk=tk if wvt_bf16 is not None else 0),
        out_shape=out_shape,
        grid=(t // tm,),
        in_specs=in_specs,
        out_specs=out_specs,
        compiler_params=_cparams(("parallel",)),
        name="norm_proj",
    )(*args)


def _stack_query_pair(q_ref, q2_scr, tq):
    lane = lax.broadcasted_iota(I32, (tq, LANES), 1)
    qs = q_ref[...] * (HEAD_DIM ** -0.5)
    zero = jnp.zeros_like(qs)
    q2_scr[0:tq, :] = jnp.where(lane < HEAD_DIM, qs, zero)
    q2_scr[tq:, :] = jnp.where(lane >= HEAD_DIM, qs, zero)


def _diff_attn_kernel(q_ref, k_ref, vt_ref, lq1_ref, lk1_ref, lq2_ref, lk2_ref, g_ref, o_ref,
                      q2_scr, *state, tq, q_cols, lambda_init):
    i = pl.program_id(2)
    n_grp = len(state) // 2
    m_scrs, acc_scrs = state[:n_grp], state[n_grp:]
    _stack_query_pair(q_ref, q2_scr, tq)
    for m_scr, acc_scr in zip(m_scrs, acc_scrs):
        m_scr[...] = jnp.full(m_scr.shape, NEG, F32)
        acc_scr[...] = jnp.zeros(acc_scr.shape, F32)
    ones = jnp.ones((LANES, tq), BF16)

    def step(j, masked):
        start = pl.multiple_of(j * tq, tq)
        k = k_ref[pl.ds(start, tq), :]
        n_sub = tq // vt_ref.shape[2]
        vt = jnp.concatenate([vt_ref[j * n_sub + s] for s in range(n_sub)], axis=1)
        vt_ext = jnp.concatenate([vt, ones], axis=0)
        for grp, (m_scr, acc_scr) in enumerate(zip(m_scrs, acc_scrs)):
            c0 = grp * q_cols
            st = _nt_dot(k, q2_scr[c0:c0 + q_cols, :])
            if masked:
                key = lax.broadcasted_iota(I32, st.shape, 0)
                qry = lax.broadcasted_iota(I32, st.shape, 1) + c0
                st = jnp.where(key // CHUNK <= (qry % tq) // CHUNK, st, NEG)
            m_old = m_scr[...]
            m_new = jnp.maximum(m_old, jnp.max(st, axis=0, keepdims=True))
            alpha = jnp.exp(m_old - m_new)
            pt = jnp.exp(st - m_new).astype(BF16)
            acc_scr[...] = alpha * acc_scr[...] + jnp.dot(vt_ext, pt, preferred_element_type=F32)
            m_scr[...] = m_new

    def body(j, carry):
        step(j, False)
        return carry

    lax.fori_loop(0, i, body, 0)
    step(i, True)

    acc = jnp.concatenate([acc_scr[...] for acc_scr in acc_scrs], axis=1)
    ot = acc[0:LANES, :] / acc[LANES:, :]
    lam = (jnp.exp(jnp.sum(lq1_ref[...] * lk1_ref[...], axis=1, keepdims=True))
           - jnp.exp(jnp.sum(lq2_ref[...] * lk2_ref[...], axis=1, keepdims=True)) + lambda_init)
    od = ot[:, 0:tq].T - lam * ot[:, tq:].T
    y = od * lax.rsqrt(jnp.mean(od * od, axis=-1, keepdims=True) + EPS) * g_ref[...]
    o_ref[...] = (y * (1.0 - lambda_init)).astype(o_ref.dtype)


def diff_attention(proj, vt, lq1, lk1, lq2, lk2, subln_g, batch, seq, lambda_init, tq=512, q_cols=1024):
    t = proj.shape[0]
    tk = vt.shape[2]
    tq = max(min(tq, seq), tk)
    nq = seq // tq
    assert tq % tk == 0
    k_off = (A_HEADS * 2 * HEAD_DIM) // LANES
    vec = lambda a: a.reshape(1, -1).astype(F32)
    small = lambda n: pl.BlockSpec((1, n), lambda b, h, i: (0, 0))
    q_cols = min(q_cols, 2 * tq)
    n_grp = 2 * tq // q_cols
    return pl.pallas_call(
        functools.partial(_diff_attn_kernel, tq=tq, q_cols=q_cols, lambda_init=lambda_init),
        out_shape=jax.ShapeDtypeStruct((t, A_HEADS * 2 * HEAD_DIM), BF16),
        grid=(batch, A_HEADS, nq),
        in_specs=[
            pl.BlockSpec((tq, LANES), lambda b, h, i: (b * nq + i, h)),
            pl.BlockSpec((seq, LANES), lambda b, h, i: (b, k_off + h)),
            pl.BlockSpec((seq // tk, LANES, tk), lambda b, h, i: (b, h, 0)),
            small(HEAD_DIM), small(HEAD_DIM), small(HEAD_DIM), small(HEAD_DIM), small(2 * HEAD_DIM),
        ],
        out_specs=pl.BlockSpec((tq, LANES), lambda b, h, i: (b * nq + i, h)),
        scratch_shapes=([pltpu.VMEM((2 * tq, LANES), BF16)]
                        + [pltpu.VMEM((1, q_cols), F32)] * n_grp
                        + [pltpu.VMEM((2 * LANES, q_cols), F32)] * n_grp),
        compiler_params=_cparams(("parallel", "parallel", "arbitrary")),
        name="diff_attn",
    )(proj, proj, vt, vec(lq1), vec(lk1), vec(lq2), vec(lk2), vec(subln_g))


def _stick_kernel(q_ref, k_ref, vt_ref, o_ref, q2_scr, tri_scr, c_scr, acc_scr, *, tq, tk):
    i = pl.program_id(2)
    n_sub = tq // tk
    _stack_query_pair(q_ref, q2_scr, tq)
    r_i = lax.broadcasted_iota(I32, (tk, tk), 0)
    c_i = lax.broadcasted_iota(I32, (tk, tk), 1)
    tri_scr[...] = jnp.where(c_i > r_i, 1.0, 0.0).astype(BF16)
    c_scr[...] = jnp.zeros(c_scr.shape, F32)
    acc_scr[...] = jnp.zeros(acc_scr.shape, F32)

    def step(j, masked):
        k = k_ref[pl.ds(pl.multiple_of(j * tk, tk), tk), :]
        zt = _nt_dot(k, q2_scr[...])
        sp = jnp.maximum(zt, 0.0) + jnp.log(1.0 + jnp.exp(-jnp.abs(zt)))
        if masked:
            key = lax.broadcasted_iota(I32, zt.shape, 0) + j * tk
            qry = lax.broadcasted_iota(I32, zt.shape, 1) % tq + i * tq
            valid = key < qry
            l = jnp.where(valid, -sp, 0.0)
        else:
            l = -sp
        l_hi = l.astype(BF16)
        l_lo = (l - l_hi.astype(F32)).astype(BF16)
        res = jnp.dot(tri_scr[...], jnp.concatenate([l_hi, l_lo], axis=1), preferred_element_type=F32)
        after = res[:, 0:2 * tq] + res[:, 2 * tq:]
        w = jnp.exp(zt - sp + after + c_scr[...])
        if masked:
            w = jnp.where(valid, w, 0.0)
        acc_scr[...] += jnp.dot(vt_ref[j], w.astype(BF16), preferred_element_type=F32)
        c_scr[...] += after[0:1, :] + l[0:1, :]

    for s in reversed(range(n_sub)):
        step(i * n_sub + s, True)

    def body(n, carry):
        step(i * n_sub - 1 - n, False)
        return carry

    lax.fori_loop(0, i * n_sub, body, 0)
    acc = acc_scr[...]
    vrow = lax.broadcasted_iota(I32, (LANES, tq), 0)
    ot = jnp.where(vrow < HEAD_DIM, acc[:, 0:tq], acc[:, tq:])
    o_ref[...] = ot.T.astype(o_ref.dtype)


def stick_breaking(proj, vt, batch, seq, tq=512):
    t = proj.shape[0]
    tk = vt.shape[2]
    tq = max(min(tq, seq), tk)
    nq = seq // tq
    assert tq % tk == 0
    pairs = (B_HEADS * HEAD_DIM) // LANES
    a_blocks = (A_HEADS * 2 * HEAD_DIM) // LANES
    q_off = 2 * a_blocks
    k_off = q_off + pairs
    v_off = a_blocks
    return pl.pallas_call(
        functools.partial(_stick_kernel, tq=tq, tk=tk),
        out_shape=jax.ShapeDtypeStruct((t, B_HEADS * HEAD_DIM), BF16),
        grid=(batch, pairs, nq),
        in_specs=[
            pl.BlockSpec((tq, LANES), lambda b, p, i: (b * nq + i, q_off + p)),
            pl.BlockSpec((seq, LANES), lambda b, p, i: (b, k_off + p)),
            pl.BlockSpec((seq // tk, LANES, tk), lambda b, p, i: (b, v_off + p, 0)),
        ],
        out_specs=pl.BlockSpec((tq, LANES), lambda b, p, i: (b * nq + i, p)),
        scratch_shapes=[
            pltpu.VMEM((2 * tq, LANES), BF16),
            pltpu.VMEM((tk, tk), BF16),
            pltpu.VMEM((1, 2 * tq), F32),
            pltpu.VMEM((LANES, 2 * tq), F32),
        ],
        compiler_params=_cparams(("parallel", "parallel", "arbitrary")),
        name="stick_attn",
    )(proj, proj, vt)


def _out_proj_kernel(x_ref, a_ref, b_ref, w_ref, o_ref):
    half = a_ref.shape[1]
    o_ref[...] = (x_ref[...]
                  + jnp.dot(a_ref[...], w_ref[0:half, :], preferred_element_type=F32)
                  + jnp.dot(b_ref[...], w_ref[half:, :], preferred_element_type=F32))


def out_proj_residual(x2, a, b, w_bf16, tm=512):
    t, d = x2.shape
    half = a.shape[1]
    tm = min(tm, t)
    return pl.pallas_call(
        _out_proj_kernel,
        out_shape=jax.ShapeDtypeStruct((t, d), F32),
        grid=(t // tm,),
        in_specs=[
            pl.BlockSpec((tm, d), lambda i: (i, 0)),
            pl.BlockSpec((tm, half), lambda i: (i, 0)),
            pl.BlockSpec((tm, half), lambda i: (i, 0)),
            pl.BlockSpec((2 * half, d), lambda i: (0, 0)),
        ],
        out_specs=pl.BlockSpec((tm, d), lambda i: (i, 0)),
        compiler_params=_cparams(("parallel",)),
        name="out_proj",
    )(x2, a, b, w_bf16)


def _swiglu_chunks(xn, wg_ref, wu_ref, h_scr, ff_chunk):
    d_ff = h_scr.shape[1]
    for c0 in range(0, d_ff, ff_chunk):
        g = jnp.dot(xn, wg_ref[:, c0:c0 + ff_chunk], preferred_element_type=F32)
        u = jnp.dot(xn, wu_ref[:, c0:c0 + ff_chunk], preferred_element_type=F32)
        h_scr[:, c0:c0 + ff_chunk] = (g * jax.nn.sigmoid(g) * u).astype(BF16)


def _ffn_kernel(x_ref, g_ref, wg_ref, wu_ref, wd_ref, o_ref, h_scr, *, ff_chunk):
    x = x_ref[...]
    xn = (x * lax.rsqrt(jnp.mean(x * x, axis=-1, keepdims=True) + EPS) * g_ref[...]).astype(BF16)
    _swiglu_chunks(xn, wg_ref, wu_ref, h_scr, ff_chunk)
    o_ref[...] = x + jnp.dot(h_scr[...], wd_ref[...], preferred_element_type=F32)


def ffn_residual(x2, g, wg, wu, wd, tm=512):
    t, d = x2.shape
    d_ff = wg.shape[1]
    tm = min(tm, t)
    ff_chunk = 256
    assert d_ff % ff_chunk == 0
    resident = lambda shape: pl.BlockSpec(shape, lambda i: (0, 0), pipeline_mode=pl.Buffered(1))
    return pl.pallas_call(
        functools.partial(_ffn_kernel, ff_chunk=ff_chunk),
        out_shape=jax.ShapeDtypeStruct((t, d), F32),
        grid=(t // tm,),
        in_specs=[
            pl.BlockSpec((tm, d), lambda i: (i, 0)),
            pl.BlockSpec((1, d), lambda i: (0, 0)),
            resident((d, d_ff)), resident((d, d_ff)), resident((d_ff, d)),
        ],
        out_specs=pl.BlockSpec((tm, d), lambda i: (i, 0)),
        scratch_shapes=[pltpu.VMEM((tm, d_ff), BF16)],
        compiler_params=_cparams(("parallel",)),
        name="ffn",
    )(x2, g.reshape(1, d), wg, wu, wd)


CONV_HALO = 32


def _conv_kernel(val_ref, gate_ref, pval_ref, pgate_ref, w_ref, b_ref, lg_ref, lb_ref, o_ref,
                 u_scr, c_scr, *, ts, tiles_per_seq, row_blk):
    i = pl.program_id(0)
    glu = lambda v, g: v.astype(F32) * jax.nn.sigmoid(g.astype(F32))
    keep = jnp.where(i % tiles_per_seq == 0, 0.0, 1.0)
    u_scr[0:CONV_HALO, :] = glu(pval_ref[...], pgate_ref[...]) * keep
    u_scr[CONV_HALO:, :] = glu(val_ref[...], gate_ref[...])
    shift = CONV_HALO - (CONV_WIDTH - 1)
    n_ch = val_ref.shape[1]
    for c0 in range(0, n_ch, LANES):
        for r0 in range(0, ts, row_blk):
            acc = jnp.broadcast_to(b_ref[:, c0:c0 + LANES], (row_blk, LANES))
            for j in range(CONV_WIDTH):
                acc = acc + w_ref[j:j + 1, c0:c0 + LANES] * u_scr[r0 + shift + j:r0 + shift + j + row_blk,
                                                                   c0:c0 + LANES]
            c_scr[r0:r0 + row_blk, c0:c0 + LANES] = acc
    c = c_scr[...]
    mu = jnp.mean(c, axis=-1, keepdims=True)
    xc = c - mu
    var = jnp.mean(xc * xc, axis=-1, keepdims=True)
    y = xc * lax.rsqrt(var + EPS) * lg_ref[...] + lb_ref[...]
    o_ref[...] = (y * jax.nn.sigmoid(y)).astype(o_ref.dtype)


def conv_module(proj, conv_w, conv_b, ln_g, ln_b, seq, ts=256):
    t = proj.shape[0]
    ts = min(ts, seq)
    n_ch = C_CHANNELS
    halo_blocks = ts // CONV_HALO
    row = lambda a: a.reshape(1, n_ch).astype(F32)
    small = pl.BlockSpec((1, n_ch), lambda i: (0, 0))
    prev = lambda col: pl.BlockSpec((CONV_HALO, n_ch), lambda i: (jnp.maximum(i * halo_blocks - 1, 0), col))
    return pl.pallas_call(
        functools.partial(_conv_kernel, ts=ts, tiles_per_seq=seq // ts, row_blk=64),
        out_shape=jax.ShapeDtypeStruct((t, n_ch), BF16),
        grid=(t // ts,),
        in_specs=[
            pl.BlockSpec((ts, n_ch), lambda i: (i, 0)),
            pl.BlockSpec((ts, n_ch), lambda i: (i, 1)),
            prev(0), prev(1),
            pl.BlockSpec((CONV_WIDTH, n_ch), lambda i: (0, 0)),
            small, small, small,
        ],
        out_specs=pl.BlockSpec((ts, n_ch), lambda i: (i, 0)),
        scratch_shapes=[pltpu.VMEM((ts + CONV_HALO, n_ch), F32), pltpu.VMEM((ts, n_ch), F32)],
        compiler_params=_cparams(("parallel",)),
        name="conv_module",
    )(proj, proj, proj, proj, conv_w.astype(F32), row(conv_b), row(ln_g), row(ln_b))


def _rel_bias_kernel(tbl_ref, o_ref):
    p = pl.program_id(0)
    iq = lax.broadcasted_iota(I32, (CHUNK, BAND), 0)
    kk = lax.broadcasted_iota(I32, (CHUNK, BAND), 1)
    idx = jnp.clip(iq - (kk - LEFT_CHUNKS * CHUNK), -MAX_REL, MAX_REL) + MAX_REL
    for hh in range(2):
        h = 2 * p + hh

        def body(j, acc):
            return acc + jnp.where(idx == j, tbl_ref[h, j], 0.0)

        o_ref[0, hh * CHUNK:(hh + 1) * CHUNK, :] = lax.fori_loop(
            0, 2 * MAX_REL + 1, body, jnp.zeros((CHUNK, BAND), F32))


def rel_bias(rel_table):
    pairs = D_HEADS // 2
    return pl.pallas_call(
        _rel_bias_kernel,
        out_shape=jax.ShapeDtypeStruct((pairs, 2 * CHUNK, BAND), F32),
        grid=(pairs,),
        in_specs=[pl.BlockSpec(memory_space=pltpu.SMEM)],
        out_specs=pl.BlockSpec((1, 2 * CHUNK, BAND), lambda p: (p, 0, 0)),
        compiler_params=_cparams(("parallel",)),
        name="rel_bias",
    )(rel_table.astype(F32))


def _chunk_attn_kernel(q_ref, k_ref, v_ref, bias_ref, o_ref, kpad, vpad, *, group, seq):
    i = pl.program_id(2)
    pad = LEFT_CHUNKS * CHUNK

    @pl.when(i == 0)
    def _():
        kpad[0:pad, :] = jnp.zeros((pad, LANES), BF16)
        vpad[0:pad, :] = jnp.zeros((pad, LANES), BF16)
        kpad[pad:, :] = k_ref[...]
        vpad[pad:, :] = v_ref[...]

    lane = lax.broadcasted_iota(I32, (CHUNK, LANES), 1)
    col_chunk = lax.broadcasted_iota(I32, (2 * CHUNK, BAND), 1) // CHUNK
    bias = bias_ref[0]
    for g in range(group):
        c = i * group + g
        qs = q_ref[g * CHUNK:(g + 1) * CHUNK, :] * (HEAD_DIM ** -0.5)
        zero = jnp.zeros_like(qs)
        q2 = jnp.concatenate([jnp.where(lane < HEAD_DIM, qs, zero),
                              jnp.where(lane >= HEAD_DIM, qs, zero)], axis=0)
        start = pl.multiple_of(c * CHUNK, CHUNK)
        kw = kpad[pl.ds(start, BAND), :]
        vw = vpad[pl.ds(start, BAND), :]
        s = _nt_dot(q2, kw) + bias
        s = jnp.where(col_chunk >= LEFT_CHUNKS - c, s, NEG)
        m = jnp.max(s, axis=1, keepdims=True)
        p = jnp.exp(s - m)
        denom = jnp.sum(p, axis=1, keepdims=True)
        o = jnp.dot(p.astype(BF16), vw, preferred_element_type=F32) / denom
        o_ref[g * CHUNK:(g + 1) * CHUNK, :] = jnp.where(lane < HEAD_DIM, o[0:CHUNK, :],
                                                        o[CHUNK:, :]).astype(o_ref.dtype)


def chunk_rel_attention(proj, bias, batch, seq, group=4):
    t = proj.shape[0]
    pairs = D_HEADS // 2
    q_off = (2 * C_CHANNELS) // LANES
    k_off = q_off + pairs
    v_off = k_off + pairs
    tq = group * CHUNK
    nq = seq // tq
    return pl.pallas_call(
        functools.partial(_chunk_attn_kernel, group=group, seq=seq),
        out_shape=jax.ShapeDtypeStruct((t, D_HEADS * HEAD_DIM), BF16),
        grid=(batch, pairs, nq),
        in_specs=[
            pl.BlockSpec((tq, LANES), lambda b, p, i: (b * nq + i, q_off + p)),
            pl.BlockSpec((seq, LANES), lambda b, p, i: (b, k_off + p)),
            pl.BlockSpec((seq, LANES), lambda b, p, i: (b, v_off + p)),
            pl.BlockSpec((1, 2 * CHUNK, BAND), lambda b, p, i: (p, 0, 0)),
        ],
        out_specs=pl.BlockSpec((tq, LANES), lambda b, p, i: (b * nq + i, p)),
        scratch_shapes=[pltpu.VMEM((seq + LEFT_CHUNKS * CHUNK, LANES), BF16),
                        pltpu.VMEM((seq + LEFT_CHUNKS * CHUNK, LANES), BF16)],
        compiler_params=_cparams(("parallel", "parallel", "arbitrary")),
        name="chunk_attn",
    )(proj, proj, proj, bias)


def _store_row_tiles(ref, val):
    n, d = val.shape
    sub = d // LANES
    for k in range(sub):
        ref[pl.ds(k, n, stride=sub), :] = val[:, k * LANES:(k + 1) * LANES]


def _load_row_tiles(ref, n, d):
    sub = d // LANES
    return jnp.concatenate([ref[pl.ds(k, n, stride=sub), :] for k in range(sub)], axis=1)


def _router_kernel(x_ref, g_ref, wr_ref, h_ref, meta_ref, cnt_ref, carry_scr, tri_scr, *, tr):
    i = pl.program_id(0)

    @pl.when(i == 0)
    def _():
        carry_scr[...] = jnp.zeros(carry_scr.shape, F32)
        r_i = lax.broadcasted_iota(I32, (tr, tr), 0)
        c_i = lax.broadcasted_iota(I32, (tr, tr), 1)
        tri_scr[...] = jnp.where(r_i < c_i, 1.0, 0.0).astype(BF16)

    x = x_ref[...]
    hn = x * lax.rsqrt(jnp.mean(x * x, axis=-1, keepdims=True) + EPS) * g_ref[...]
    _store_row_tiles(h_ref, hn)
    h_hi = hn.astype(BF16)
    h_lo = (hn - h_hi.astype(F32)).astype(BF16)
    w = wr_ref[...]
    w_hi = w.astype(BF16)
    w_lo = (w - w_hi.astype(F32)).astype(BF16)
    logits = _nt_dot(w_hi, h_hi) + (_nt_dot(w_hi, h_lo) + _nt_dot(w_lo, h_hi))
    eidx = lax.broadcasted_iota(I32, logits.shape, 0).astype(F32)
    m1 = jnp.max(logits, axis=0, keepdims=True)
    i1 = jnp.min(jnp.where(logits == m1, eidx, float(N_EXPERTS)), axis=0, keepdims=True)
    rest = jnp.where(eidx == i1, -jnp.inf, logits)
    m2 = jnp.max(rest, axis=0, keepdims=True)
    i2 = jnp.min(jnp.where(rest == m2, eidx, float(N_EXPERTS)), axis=0, keepdims=True)
    e = jnp.exp(m2 - m1)
    g1 = 1.0 / (1.0 + e)
    g2 = e / (1.0 + e)
    sel1 = eidx == i1
    sel2 = eidx == i2
    onehot = jnp.where(sel1 | sel2, 1.0, 0.0)
    rank = jnp.dot(onehot.astype(BF16), tri_scr[...], preferred_element_type=F32) + carry_scr[...]
    r1 = jnp.sum(jnp.where(sel1, rank, 0.0), axis=0, keepdims=True)
    r2 = jnp.sum(jnp.where(sel2, rank, 0.0), axis=0, keepdims=True)
    carry = carry_scr[...] + jnp.sum(onehot, axis=1, keepdims=True)
    carry_scr[...] = carry
    zero = jnp.zeros_like(g1)
    meta_ref[0] = jnp.concatenate([i1, i2, r1, r2, g1, g2, zero, zero], axis=0)
    cnt_ref[...] = jnp.broadcast_to(carry, cnt_ref.shape)


def router(x2, g, w_router, tr=512):
    t, d = x2.shape
    tr = min(tr, t)
    sub = d // LANES
    return pl.pallas_call(
        functools.partial(_router_kernel, tr=tr),
        out_shape=(jax.ShapeDtypeStruct((t * sub, LANES), F32),
                   jax.ShapeDtypeStruct((t // tr, 8, tr), F32),
                   jax.ShapeDtypeStruct((N_EXPERTS, LANES), F32)),
        grid=(t // tr,),
        in_specs=[
            pl.BlockSpec((tr, d), lambda i: (i, 0)),
            pl.BlockSpec((1, d), lambda i: (0, 0)),
            pl.BlockSpec((N_EXPERTS, d), lambda i: (0, 0)),
        ],
        out_specs=(pl.BlockSpec((tr * sub, LANES), lambda i: (i, 0)),
                   pl.BlockSpec((1, 8, tr), lambda i: (i, 0, 0)),
                   pl.BlockSpec((N_EXPERTS, LANES), lambda i: (0, 0))),
        scratch_shapes=[pltpu.VMEM((N_EXPERTS, 1), F32), pltpu.VMEM((tr, tr), BF16)],
        compiler_params=_cparams(("arbitrary",)),
        name="router",
    )(x2, g.reshape(1, d), w_router.T.astype(F32))


def _row_copy(src_ref, src_row, dst_ref, dst_row, sem, sub):
    src = src_ref.at[pl.ds(pl.multiple_of(src_row * sub, sub), sub)]
    dst = dst_ref.at[pl.ds(pl.multiple_of(dst_row * sub, sub), sub)]
    return pltpu.make_async_copy(src, dst, sem)


def _wait_rows(ref, sem, n_rows, sub):
    pltpu.make_async_copy(ref.at[pl.ds(0, n_rows * sub)], ref.at[pl.ds(0, n_rows * sub)], sem).wait()


def _scatter_rows_kernel(idx1_ref, idx2_ref, src_ref, init_ref, dst_ref, sem, *, rows, sub):
    del init_ref
    base = pl.program_id(0) * rows

    def issue(r, carry):
        _row_copy(src_ref, base + r, dst_ref, idx1_ref[0, 0, r], sem, sub).start()
        _row_copy(src_ref, base + r, dst_ref, idx2_ref[0, 0, r], sem, sub).start()
        return carry

    lax.fori_loop(0, rows, issue, 0)
    _wait_rows(dst_ref, sem, 2 * rows, sub)


def scatter_rows(src, idx1, idx2, dst_init, sub, rows=512):
    n = idx1.shape[0]
    rows = min(rows, n)
    assert n % rows == 0
    idx_spec = pl.BlockSpec((1, 1, rows), lambda i: (i, 0, 0), memory_space=pltpu.SMEM)
    any_spec = pl.BlockSpec(memory_space=pl.ANY)
    return pl.pallas_call(
        functools.partial(_scatter_rows_kernel, rows=rows, sub=sub),
        out_shape=jax.ShapeDtypeStruct(dst_init.shape, dst_init.dtype),
        grid=(n // rows,),
        in_specs=[idx_spec, idx_spec, any_spec, any_spec],
        out_specs=any_spec,
        scratch_shapes=[pltpu.SemaphoreType.DMA(())],
        input_output_aliases={3: 0},
        compiler_params=_cparams(("arbitrary",)),
        name="row_scatter",
    )(idx1.reshape(n // rows, 1, rows), idx2.reshape(n // rows, 1, rows), src, dst_init)


def _gather_rows_kernel(idx1_ref, idx2_ref, src_ref, dst1_ref, dst2_ref, sem, *, rows, sub):
    base = pl.program_id(0) * rows

    def issue(r, carry):
        _row_copy(src_ref, idx1_ref[0, 0, r], dst1_ref, base + r, sem, sub).start()
        _row_copy(src_ref, idx2_ref[0, 0, r], dst2_ref, base + r, sem, sub).start()
        return carry

    lax.fori_loop(0, rows, issue, 0)
    _wait_rows(dst1_ref, sem, 2 * rows, sub)


def gather_rows(src, idx1, idx2, sub, rows=512):
    n = idx1.shape[0]
    rows = min(rows, n)
    assert n % rows == 0 and 2 * rows <= n
    idx_spec = pl.BlockSpec((1, 1, rows), lambda i: (i, 0, 0), memory_space=pltpu.SMEM)
    any_spec = pl.BlockSpec(memory_space=pl.ANY)
    out = jax.ShapeDtypeStruct((n * sub, LANES), src.dtype)
    return pl.pallas_call(
        functools.partial(_gather_rows_kernel, rows=rows, sub=sub),
        out_shape=(out, out),
        grid=(n // rows,),
        in_specs=[idx_spec, idx_spec, any_spec],
        out_specs=(any_spec, any_spec),
        scratch_shapes=[pltpu.SemaphoreType.DMA(())],
        compiler_params=_cparams(("arbitrary",)),
        name="row_gather",
    )(idx1.reshape(n // rows, 1, rows), idx2.reshape(n // rows, 1, rows), src)


def _expert_ffn_kernel(te_ref, nt_ref, x_ref, wg_ref, wu_ref, wd_ref, o_ref, xn_scr, h_scr, acc_scr, *, ff_chunk):
    n = pl.program_id(0)
    f = pl.program_id(1)
    tm, d = acc_scr.shape

    @pl.when(n < nt_ref[0])
    def _():
        @pl.when(f == 0)
        def _():
            xn_scr[...] = _load_row_tiles(x_ref, tm, d).astype(BF16)

        _swiglu_chunks(xn_scr[...], wg_ref.at[0], wu_ref.at[0], h_scr, ff_chunk)
        part = jnp.dot(h_scr[...], wd_ref[0], preferred_element_type=F32)

        @pl.when(f == 0)
        def _():
            acc_scr[...] = part

        @pl.when(f > 0)
        def _():
            acc_scr[...] += part

        @pl.when(f == pl.num_programs(1) - 1)
        def _():
            _store_row_tiles(o_ref, acc_scr[...])

    @pl.when((n >= nt_ref[0]) & (f == pl.num_programs(1) - 1))
    def _():
        o_ref[...] = jnp.zeros(o_ref.shape, o_ref.dtype)


def expert_ffn(xs, tile_expert, n_tiles_used, wg, wu, wd, tm, ff_split=2):
    d, d_ff = wg.shape[1], wg.shape[2]
    sub = d // LANES
    nr = xs.shape[0] // sub
    ff_blk = d_ff // ff_split
    ff_chunk = 256
    assert nr % tm == 0 and ff_blk % ff_chunk == 0
    grid_spec = pltpu.PrefetchScalarGridSpec(
        num_scalar_prefetch=2,
        grid=(nr // tm, ff_split),
        in_specs=[
            pl.BlockSpec((tm * sub, LANES), lambda n, f, te, nt: (n, 0)),
            pl.BlockSpec((1, d, ff_blk), lambda n, f, te, nt: (te[n], 0, f)),
            pl.BlockSpec((1, d, ff_blk), lambda n, f, te, nt: (te[n], 0, f)),
            pl.BlockSpec((1, ff_blk, d), lambda n, f, te, nt: (te[n], f, 0)),
        ],
        out_specs=pl.BlockSpec((tm * sub, LANES), lambda n, f, te, nt: (n, 0)),
        scratch_shapes=[pltpu.VMEM((tm, d), BF16), pltpu.VMEM((tm, ff_blk), BF16), pltpu.VMEM((tm, d), F32)],
    )
    return pl.pallas_call(
        functools.partial(_expert_ffn_kernel, ff_chunk=ff_chunk),
        out_shape=jax.ShapeDtypeStruct((nr * sub, LANES), F32),
        grid_spec=grid_spec,
        compiler_params=_cparams(("arbitrary", "arbitrary")),
        name="expert_ffn",
    )(tile_expert, n_tiles_used, xs, wg, wu, wd)


def _combine_kernel(x_ref, y1_ref, y2_ref, g1_ref, g2_ref, gn_ref, o_ref, *, final_norm):
    tm, d = x_ref.shape
    y = (x_ref[...] + g1_ref[...] * _load_row_tiles(y1_ref, tm, d)
         + g2_ref[...] * _load_row_tiles(y2_ref, tm, d))
    if final_norm:
        y = y * lax.rsqrt(jnp.mean(y * y, axis=-1, keepdims=True) + EPS) * gn_ref[...]
    o_ref[...] = y


def combine(x2, y1, y2, g1, g2, gn, final_norm, tm=512):
    t, d = x2.shape
    tm = min(tm, t)
    sub = d // LANES
    big = pl.BlockSpec((tm, d), lambda i: (i, 0))
    tiles = pl.BlockSpec((tm * sub, LANES), lambda i: (i, 0))
    col = pl.BlockSpec((tm, 1), lambda i: (i, 0))
    return pl.pallas_call(
        functools.partial(_combine_kernel, final_norm=final_norm),
        out_shape=jax.ShapeDtypeStruct((t, d), F32),
        grid=(t // tm,),
        in_specs=[big, tiles, tiles, col, col, pl.BlockSpec((1, d), lambda i: (0, 0))],
        out_specs=big,
        compiler_params=_cparams(("parallel",)),
        name="moe_combine",
    )(x2, y1, y2, g1, g2, gn.reshape(1, d))


def moe_residual(x2, g, w_router, wg, wu, wd, final_gain, final_norm, tm=512):
    t, d = x2.shape
    tm = min(tm, t)
    hn, meta, counts = router(x2, g, w_router)
    tr = meta.shape[2]
    field = lambda r: meta[:, r, :].reshape(t)
    idx1, idx2 = field(0).astype(I32), field(1).astype(I32)
    rank1, rank2 = field(2).astype(I32), field(3).astype(I32)
    gate1, gate2 = field(4), field(5)
    cnt = counts[:, 0].astype(I32)
    padded = ((cnt + tm - 1) // tm) * tm
    ends = jnp.cumsum(padded)
    offs = ends - padded
    pos1 = offs[idx1] + rank1
    pos2 = offs[idx2] + rank2
    n_rows = TOP_K * t + N_EXPERTS * tm
    tile_start = jnp.arange(n_rows // tm, dtype=I32) * tm
    tile_expert = jnp.minimum(jnp.sum(tile_start[:, None] >= ends[None, :], axis=1), N_EXPERTS - 1).astype(I32)
    n_tiles_used = (ends[-1:] // tm).astype(I32)

    sub = d // LANES
    xs = scatter_rows(hn, pos1, pos2, jnp.zeros((n_rows * sub, LANES), F32), sub)
    ys = expert_ffn(xs, tile_expert, n_tiles_used, wg, wu, wd, tm)
    y1, y2 = gather_rows(ys, pos1, pos2, sub)
    return combine(x2, y1, y2, gate1.reshape(t, 1), gate2.reshape(t, 1), final_gain, final_norm)


def kernel(x, ev_norm_mix, ev_w_in, ev_lambda_q1, ev_lambda_k1, ev_lambda_q2, ev_lambda_k2, ev_subln, ev_w_out,
           ev_norm_ffn, ev_ffn_gate, ev_ffn_up, ev_ffn_down, od_norm_mix, od_w_in, od_conv_w, od_conv_b,
           od_conv_ln_g, od_conv_ln_b, od_rel_bias, od_w_out, od_norm_ffn, od_router, od_exp_gate, od_exp_up,
           od_exp_down, final_norm):
    batch, seq, d = x.shape
    depth = 2 * ev_w_in.shape[0]
    assert od_w_in.shape[0] * 2 == depth
    x2 = x.reshape(batch * seq, d)
    bf = lambda a: a.astype(BF16)
    a_qk = A_HEADS * 2 * HEAD_DIM
    a_v = A_HEADS * 2 * HEAD_DIM
    b_w = B_HEADS * HEAD_DIM
    attn_q_block = 512
    attn_k_block = 256
    for layer in range(depth):
        i = layer // 2
        if layer % 2 == 0:
            lambda_init = 0.8 - 0.6 * math.exp(-0.3 * layer)
            w = ev_w_in[i]
            va0, qb0 = 2 * a_qk, 2 * a_qk + a_v
            vb0 = qb0 + 2 * b_w
            w_qk = bf(jnp.concatenate([w[:, 0:va0], w[:, qb0:vb0]], axis=1))
            w_vt = bf(jnp.concatenate([w[:, va0:qb0], w[:, vb0:]], axis=1).T)
            proj, vt = norm_proj(x2, ev_norm_mix[i], w_qk, seq, rope_cols=2 * a_qk, wvt_bf16=w_vt,
                                 tk=attn_k_block)
            oa = diff_attention(proj, vt, ev_lambda_q1[i], ev_lambda_k1[i], ev_lambda_q2[i], ev_lambda_k2[i],
                                ev_subln[i], batch, seq, lambda_init, tq=attn_q_block)
            ob = stick_breaking(proj, vt, batch, seq, tq=attn_q_block)
            x2 = out_proj_residual(x2, oa, ob, bf(ev_w_out[i]))
            x2 = ffn_residual(x2, ev_norm_ffn[i], bf(ev_ffn_gate[i]), bf(ev_ffn_up[i]), bf(ev_ffn_down[i]))
        else:
            proj = norm_proj(x2, od_norm_mix[i], bf(od_w_in[i]), seq)
            c = conv_module(proj, od_conv_w[i], od_conv_b[i], od_conv_ln_g[i], od_conv_ln_b[i], seq)
            od = chunk_rel_attention(proj, rel_bias(od_rel_bias[i]), batch, seq)
            x2 = out_proj_residual(x2, c, od, bf(od_w_out[i]))
            x2 = moe_residual(x2, od_norm_ffn[i], od_router[i], bf(od_exp_gate[i]), bf(od_exp_up[i]),
                              bf(od_exp_down[i]), final_norm, final_norm=(layer == depth - 1))
    return x2.reshape(batch, seq, d)
```

```python
import functools
import math

import numpy as np
import jax
import jax.numpy as jnp
from jax import lax
from jax.experimental import pallas as pl
from jax.experimental.pallas import tpu as pltpu

F32 = jnp.float32
BF16 = jnp.bfloat16
I32 = jnp.int32

HEAD_DIM = 64
CHUNK = 64
ROPE_THETA = 10000.0
EPS = 1e-6
A_HEADS = 4
B_HEADS = 8
C_CHANNELS = 512
CONV_WIDTH = 31
D_HEADS = 8
LEFT_CHUNKS = 8
BAND = (LEFT_CHUNKS + 1) * CHUNK
MAX_REL = 128
N_EXPERTS = 8
TOP_K = 2

LANES = 128
NEG = -1e30
VMEM_LIMIT = 56 * 1024 * 1024


def _cparams(sem):
    return pltpu.CompilerParams(dimension_semantics=sem, vmem_limit_bytes=VMEM_LIMIT)


def _nt_dot(a, b):
    return lax.dot_general(a, b, (((1,), (1,)), ((), ())), preferred_element_type=F32)


def _norm_proj_kernel(x_ref, g_ref, w_ref, cos_ref, sin_ref, *rest, col_chunk, rope_cols, tk):
    if tk:
        wvt_ref, o_ref, vt_ref = rest
    else:
        (o_ref,) = rest
    x = x_ref[...]
    xn = (x * lax.rsqrt(jnp.mean(x * x, axis=-1, keepdims=True) + EPS) * g_ref[...]).astype(BF16)
    n_out = o_ref.shape[1]
    if tk:
        for r0 in range(0, wvt_ref.shape[0], col_chunk):
            res = _nt_dot(wvt_ref[r0:r0 + col_chunk, :], xn)
            for b in range(vt_ref.shape[0]):
                vt_ref[b, r0:r0 + col_chunk, :] = res[:, b * tk:(b + 1) * tk].astype(BF16)
    if rope_cols:
        cos = cos_ref[...]
        sin = sin_ref[...]
        lane = lax.broadcasted_iota(I32, cos.shape, 1)
        first_half = (lane % HEAD_DIM) < (HEAD_DIM // 2)
    for c0 in range(0, n_out, col_chunk):
        r = jnp.dot(xn, w_ref[:, c0:c0 + col_chunk], preferred_element_type=F32)
        if c0 < rope_cols:
            parts = []
            for l0 in range(0, col_chunk, LANES):
                seg = r[:, l0:l0 + LANES]
                partner = jnp.where(first_half,
                                    pltpu.roll(seg, LANES - HEAD_DIM // 2, 1),
                                    pltpu.roll(seg, HEAD_DIM // 2, 1))
                parts.append(seg * cos + partner * sin)
            r = jnp.concatenate(parts, axis=1)
        o_ref[:, c0:c0 + col_chunk] = r.astype(o_ref.dtype)


def _rope_tables(seq):
    half = HEAD_DIM // 2
    inv_freq = ROPE_THETA ** (-jnp.arange(half, dtype=F32) * 2.0 / HEAD_DIM)
    ang = jnp.arange(seq, dtype=F32)[:, None] * inv_freq[None, :]
    cos, sin = jnp.cos(ang), jnp.sin(ang)
    cos_t = jnp.tile(jnp.concatenate([cos, cos], axis=1), (1, LANES // HEAD_DIM))
    sin_t = jnp.tile(jnp.concatenate([-sin, sin], axis=1), (1, LANES // HEAD_DIM))
    return cos_t, sin_t


def norm_proj(x2, g, w_bf16, seq, rope_cols=0, wvt_bf16=None, tk=0, tm=512):
    t, d = x2.shape
    n_out = w_bf16.shape[1]
    tm = min(tm, seq)
    col_chunk = 512
    assert t % tm == 0 and seq % tm == 0 and n_out % col_chunk == 0 and rope_cols % col_chunk == 0
    cos_t, sin_t = _rope_tables(seq)
    tiles_per_seq = seq // tm
    in_specs = [
        pl.BlockSpec((tm, d), lambda i: (i, 0)),
        pl.BlockSpec((1, d), lambda i: (0, 0)),
        pl.BlockSpec((d, n_out), lambda i: (0, 0)),
        pl.BlockSpec((tm, LANES), lambda i: (i % tiles_per_seq, 0)),
        pl.BlockSpec((tm, LANES), lambda i: (i % tiles_per_seq, 0)),
    ]
    args = [x2, g.reshape(1, d), w_bf16, cos_t, sin_t]
    out_shape = jax.ShapeDtypeStruct((t, n_out), BF16)
    out_specs = pl.BlockSpec((tm, n_out), lambda i: (i, 0))
    if wvt_bf16 is not None:
        n_v = wvt_bf16.shape[0]
        tk = min(tk, tm)
        assert tm % tk == 0 and n_v % col_chunk == 0
        in_specs.append(pl.BlockSpec((n_v, d), lambda i: (0, 0)))
        args.append(wvt_bf16)
        out_shape = (out_shape, jax.ShapeDtypeStruct((t // tk, n_v, tk), BF16))
        out_specs = (out_specs, pl.BlockSpec((tm // tk, n_v, tk), lambda i: (i, 0, 0)))
    return pl.pallas_call(
        functools.partial(_norm_proj_kernel, col_chunk=col_chunk, rope_cols=rope_cols,
                          tk=tk if wvt_bf16 is not None else 0),
        out_shape=out_shape,
        grid=(t // tm,),
        in_specs=in_specs,
        out_specs=out_specs,
        compiler_params=_cparams(("parallel",)),
        name="norm_proj",
    )(*args)


def _stack_query_pair(q_ref, q2_scr, tq):
    lane = lax.broadcasted_iota(I32, (tq, LANES), 1)
    qs = q_ref[...] * (HEAD_DIM ** -0.5)
    zero = jnp.zeros_like(qs)
    q2_scr[0:tq, :] = jnp.where(lane < HEAD_DIM, qs, zero)
    q2_scr[tq:, :] = jnp.where(lane >= HEAD_DIM, qs, zero)


def _diff_attn_kernel(q_ref, k_ref, vt_ref, lq1_ref, lk1_ref, lq2_ref, lk2_ref, g_ref, o_ref,
                      q2_scr, *state, tq, q_cols, lambda_init):
    i = pl.program_id(2)
    n_grp = len(state) // 2
    m_scrs, acc_scrs = state[:n_grp], state[n_grp:]
    _stack_query_pair(q_ref, q2_scr, tq)
    for m_scr, acc_scr in zip(m_scrs, acc_scrs):
        m_scr[...] = jnp.full(m_scr.shape, NEG, F32)
        acc_scr[...] = jnp.zeros(acc_scr.shape, F32)
    ones = jnp.ones((LANES, tq), BF16)

    def step(j, masked):
        start = pl.multiple_of(j * tq, tq)
        k = k_ref[pl.ds(start, tq), :]
        n_sub = tq // vt_ref.shape[2]
        vt = jnp.concatenate([vt_ref[j * n_sub + s] for s in range(n_sub)], axis=1)
        vt_ext = jnp.concatenate([vt, ones], axis=0)
        for grp, (m_scr, acc_scr) in enumerate(zip(m_scrs, acc_scrs)):
            c0 = grp * q_cols
            st = _nt_dot(k, q2_scr[c0:c0 + q_cols, :])
            if masked:
                key = lax.broadcasted_iota(I32, st.shape, 0)
                qry = lax.broadcasted_iota(I32, st.shape, 1) + c0
                st = jnp.where(key // CHUNK <= (qry % tq) // CHUNK, st, NEG)
            m_old = m_scr[...]
            m_new = jnp.maximum(m_old, jnp.max(st, axis=0, keepdims=True))
            alpha = jnp.exp(m_old - m_new)
            pt = jnp.exp(st - m_new).astype(BF16)
            acc_scr[...] = alpha * acc_scr[...] + jnp.dot(vt_ext, pt, preferred_element_type=F32)
            m_scr[...] = m_new

    def body(j, carry):
        step(j, False)
        return carry

    lax.fori_loop(0, i, body, 0)
    step(i, True)

    acc = jnp.concatenate([acc_scr[...] for acc_scr in acc_scrs], axis=1)
    ot = acc[0:LANES, :] / acc[LANES:, :]
    lam = (jnp.exp(jnp.sum(lq1_ref[...] * lk1_ref[...], axis=1, keepdims=True))
           - jnp.exp(jnp.sum(lq2_ref[...] * lk2_ref[...], axis=1, keepdims=True)) + lambda_init)
    od = ot[:, 0:tq].T - lam * ot[:, tq:].T
    y = od * lax.rsqrt(jnp.mean(od * od, axis=-1, keepdims=True) + EPS) * g_ref[...]
    o_ref[...] = (y * (1.0 - lambda_init)).astype(o_ref.dtype)


def diff_attention(proj, vt, lq1, lk1, lq2, lk2, subln_g, batch, seq, lambda_init, tq=512, q_cols=1024):
    t = proj.shape[0]
    tk = vt.shape[2]
    tq = max(min(tq, seq), tk)
    nq = seq // tq
    assert tq % tk == 0
    k_off = (A_HEADS * 2 * HEAD_DIM) // LANES
    vec = lambda a: a.reshape(1, -1).astype(F32)
    small = lambda n: pl.BlockSpec((1, n), lambda b, h, i: (0, 0))
    q_cols = min(q_cols, 2 * tq)
    n_grp = 2 * tq // q_cols
    return pl.pallas_call(
        functools.partial(_diff_attn_kernel, tq=tq, q_cols=q_cols, lambda_init=lambda_init),
        out_shape=jax.ShapeDtypeStruct((t, A_HEADS * 2 * HEAD_DIM), BF16),
        grid=(batch, A_HEADS, nq),
        in_specs=[
            pl.BlockSpec((tq, LANES), lambda b, h, i: (b * nq + i, h)),
            pl.BlockSpec((seq, LANES), lambda b, h, i: (b, k_off + h)),
            pl.BlockSpec((seq // tk, LANES, tk), lambda b, h, i: (b, h, 0)),
            small(HEAD_DIM), small(HEAD_DIM), small(HEAD_DIM), small(HEAD_DIM), small(2 * HEAD_DIM),
        ],
        out_specs=pl.BlockSpec((tq, LANES), lambda b, h, i: (b * nq + i, h)),
        scratch_shapes=([pltpu.VMEM((2 * tq, LANES), BF16)]
                        + [pltpu.VMEM((1, q_cols), F32)] * n_grp
                        + [pltpu.VMEM((2 * LANES, q_cols), F32)] * n_grp),
        compiler_params=_cparams(("parallel", "parallel", "arbitrary")),
        name="diff_attn",
    )(proj, proj, vt, vec(lq1), vec(lk1), vec(lq2), vec(lk2), vec(subln_g))


def _stick_kernel(q_ref, k_ref, vt_ref, o_ref, q2_scr, tri_scr, c_scr, acc_scr, *, tq, tk):
    i = pl.program_id(2)
    n_sub = tq // tk
    _stack_query_pair(q_ref, q2_scr, tq)
    r_i = lax.broadcasted_iota(I32, (tk, tk), 0)
    c_i = lax.broadcasted_iota(I32, (tk, tk), 1)
    tri_scr[...] = jnp.where(c_i > r_i, 1.0, 0.0).astype(BF16)
    c_scr[...] = jnp.zeros(c_scr.shape, F32)
    acc_scr[...] = jnp.zeros(acc_scr.shape, F32)

    def step(j, masked):
        k = k_ref[pl.ds(pl.multiple_of(j * tk, tk), tk), :]
        zt = _nt_dot(k, q2_scr[...])
        sp = jnp.maximum(zt, 0.0) + jnp.log(1.0 + jnp.exp(-jnp.abs(zt)))
        if masked:
            key = lax.broadcasted_iota(I32, zt.shape, 0) + j * tk
            qry = lax.broadcasted_iota(I32, zt.shape, 1) % tq + i * tq
            valid = key < qry
            l = jnp.where(valid, -sp, 0.0)
        else:
            l = -sp
        l_hi = l.astype(BF16)
        l_lo = (l - l_hi.astype(F32)).astype(BF16)
        res = jnp.dot(tri_scr[...], jnp.concatenate([l_hi, l_lo], axis=1), preferred_element_type=F32)
        after = res[:, 0:2 * tq] + res[:, 2 * tq:]
        w = jnp.exp(zt - sp + after + c_scr[...])
        if masked:
            w = jnp.where(valid, w, 0.0)
        acc_scr[...] += jnp.dot(vt_ref[j], w.astype(BF16), preferred_element_type=F32)
        c_scr[...] += after[0:1, :] + l[0:1, :]

    for s in reversed(range(n_sub)):
        step(i * n_sub + s, True)

    def body(n, carry):
        step(i * n_sub - 1 - n, False)
        return carry

    lax.fori_loop(0, i * n_sub, body, 0)
    acc = acc_scr[...]
    vrow = lax.broadcasted_iota(I32, (LANES, tq), 0)
    ot = jnp.where(vrow < HEAD_DIM, acc[:, 0:tq], acc[:, tq:])
    o_ref[...] = ot.T.astype(o_ref.dtype)


def stick_breaking(proj, vt, batch, seq, tq=512):
    t = proj.shape[0]
    tk = vt.shape[2]
    tq = max(min(tq, seq), tk)
    nq = seq // tq
    assert tq % tk == 0
    pairs = (B_HEADS * HEAD_DIM) // LANES
    a_blocks = (A_HEADS * 2 * HEAD_DIM) // LANES
    q_off = 2 * a_blocks
    k_off = q_off + pairs
    v_off = a_blocks
    return pl.pallas_call(
        functools.partial(_stick_kernel, tq=tq, tk=tk),
        out_shape=jax.ShapeDtypeStruct((t, B_HEADS * HEAD_DIM), BF16),
        grid=(batch, pairs, nq),
        in_specs=[
            pl.BlockSpec((tq, LANES), lambda b, p, i: (b * nq + i, q_off + p)),
            pl.BlockSpec((seq, LANES), lambda b, p, i: (b, k_off + p)),
            pl.BlockSpec((seq // tk, LANES, tk), lambda b, p, i: (b, v_off + p, 0)),
        ],
        out_specs=pl.BlockSpec((tq, LANES), lambda b, p, i: (b * nq + i, p)),
        scratch_shapes=[
            pltpu.VMEM((2 * tq, LANES), BF16),
            pltpu.VMEM((tk, tk), BF16),
            pltpu.VMEM((1, 2 * tq), F32),
            pltpu.VMEM((LANES, 2 * tq), F32),
        ],
        compiler_params=_cparams(("parallel", "parallel", "arbitrary")),
        name="stick_attn",
    )(proj, proj, vt)


def _out_proj_kernel(x_ref, a_ref, b_ref, w_ref, o_ref):
    half = a_ref.shape[1]
    o_ref[...] = (x_ref[...]
                  + jnp.dot(a_ref[...], w_ref[0:half, :], preferred_element_type=F32)
                  + jnp.dot(b_ref[...], w_ref[half:, :], preferred_element_type=F32))


def out_proj_residual(x2, a, b, w_bf16, tm=512):
    t, d = x2.shape
    half = a.shape[1]
    tm = min(tm, t)
    return pl.pallas_call(
        _out_proj_kernel,
        out_shape=jax.ShapeDtypeStruct((t, d), F32),
        grid=(t // tm,),
        in_specs=[
            pl.BlockSpec((tm, d), lambda i: (i, 0)),
            pl.BlockSpec((tm, half), lambda i: (i, 0)),
            pl.BlockSpec((tm, half), lambda i: (i, 0)),
            pl.BlockSpec((2 * half, d), lambda i: (0, 0)),
        ],
        out_specs=pl.BlockSpec((tm, d), lambda i: (i, 0)),
        compiler_params=_cparams(("parallel",)),
        name="out_proj",
    )(x2, a, b, w_bf16)


def _swiglu_chunks(xn, wg_ref, wu_ref, h_scr, ff_chunk):
    d_ff = h_scr.shape[1]
    for c0 in range(0, d_ff, ff_chunk):
        g = jnp.dot(xn, wg_ref[:, c0:c0 + ff_chunk], preferred_element_type=F32)
        u = jnp.dot(xn, wu_ref[:, c0:c0 + ff_chunk], preferred_element_type=F32)
        h_scr[:, c0:c0 + ff_chunk] = (g * jax.nn.sigmoid(g) * u).astype(BF16)


def _ffn_kernel(x_ref, g_ref, wg_ref, wu_ref, wd_ref, o_ref, h_scr, *, ff_chunk):
    x = x_ref[...]
    xn = (x * lax.rsqrt(jnp.mean(x * x, axis=-1, keepdims=True) + EPS) * g_ref[...]).astype(BF16)
    _swiglu_chunks(xn, wg_ref, wu_ref, h_scr, ff_chunk)
    o_ref[...] = x + jnp.dot(h_scr[...], wd_ref[...], preferred_element_type=F32)


def ffn_residual(x2, g, wg, wu, wd, tm=512):
    t, d = x2.shape
    d_ff = wg.shape[1]
    tm = min(tm, t)
    ff_chunk = 256
    assert d_ff % ff_chunk == 0
    resident = lambda shape: pl.BlockSpec(shape, lambda i: (0, 0), pipeline_mode=pl.Buffered(1))
    return pl.pallas_call(
        functools.partial(_ffn_kernel, ff_chunk=ff_chunk),
        out_shape=jax.ShapeDtypeStruct((t, d), F32),
        grid=(t // tm,),
        in_specs=[
            pl.BlockSpec((tm, d), lambda i: (i, 0)),
            pl.BlockSpec((1, d), lambda i: (0, 0)),
            resident((d, d_ff)), resident((d, d_ff)), resident((d_ff, d)),
        ],
        out_specs=pl.BlockSpec((tm, d), lambda i: (i, 0)),
        scratch_shapes=[pltpu.VMEM((tm, d_ff), BF16)],
        compiler_params=_cparams(("parallel",)),
        name="ffn",
    )(x2, g.reshape(1, d), wg, wu, wd)


CONV_HALO = 32


def _conv_kernel(val_ref, gate_ref, pval_ref, pgate_ref, w_ref, b_ref, lg_ref, lb_ref, o_ref,
                 u_scr, c_scr, *, ts, tiles_per_seq, row_blk):
    i = pl.program_id(0)
    glu = lambda v, g: v.astype(F32) * jax.nn.sigmoid(g.astype(F32))
    keep = jnp.where(i % tiles_per_seq == 0, 0.0, 1.0)
    u_scr[0:CONV_HALO, :] = glu(pval_ref[...], pgate_ref[...]) * keep
    u_scr[CONV_HALO:, :] = glu(val_ref[...], gate_ref[...])
    shift = CONV_HALO - (CONV_WIDTH - 1)
    n_ch = val_ref.shape[1]
    for c0 in range(0, n_ch, LANES):
        for r0 in range(0, ts, row_blk):
            acc = jnp.broadcast_to(b_ref[:, c0:c0 + LANES], (row_blk, LANES))
            for j in range(CONV_WIDTH):
                acc = acc + w_ref[j:j + 1, c0:c0 + LANES] * u_scr[r0 + shift + j:r0 + shift + j + row_blk,
                                                                   c0:c0 + LANES]
            c_scr[r0:r0 + row_blk, c0:c0 + LANES] = acc
    c = c_scr[...]
    mu = jnp.mean(c, axis=-1, keepdims=True)
    xc = c - mu
    var = jnp.mean(xc * xc, axis=-1, keepdims=True)
    y = xc * lax.rsqrt(var + EPS) * lg_ref[...] + lb_ref[...]
    o_ref[...] = (y * jax.nn.sigmoid(y)).astype(o_ref.dtype)


def conv_module(proj, conv_w, conv_b, ln_g, ln_b, seq, ts=256):
    t = proj.shape[0]
    ts = min(ts, seq)
    n_ch = C_CHANNELS
    halo_blocks = ts // CONV_HALO
    row = lambda a: a.reshape(1, n_ch).astype(F32)
    small = pl.BlockSpec((1, n_ch), lambda i: (0, 0))
    prev = lambda col: pl.BlockSpec((CONV_HALO, n_ch), lambda i: (jnp.maximum(i * halo_blocks - 1, 0), col))
    return pl.pallas_call(
        functools.partial(_conv_kernel, ts=ts, tiles_per_seq=seq // ts, row_blk=64),
        out_shape=jax.ShapeDtypeStruct((t, n_ch), BF16),
        grid=(t // ts,),
        in_specs=[
            pl.BlockSpec((ts, n_ch), lambda i: (i, 0)),
            pl.BlockSpec((ts, n_ch), lambda i: (i, 1)),
            prev(0), prev(1),
            pl.BlockSpec((CONV_WIDTH, n_ch), lambda i: (0, 0)),
            small, small, small,
        ],
        out_specs=pl.BlockSpec((ts, n_ch), lambda i: (i, 0)),
        scratch_shapes=[pltpu.VMEM((ts + CONV_HALO, n_ch), F32), pltpu.VMEM((ts, n_ch), F32)],
        compiler_params=_cparams(("parallel",)),
        name="conv_module",
    )(proj, proj, proj, proj, conv_w.astype(F32), row(conv_b), row(ln_g), row(ln_b))


def _rel_bias_kernel(tbl_ref, o_ref):
    p = pl.program_id(0)
    iq = lax.broadcasted_iota(I32, (CHUNK, BAND), 0)
    kk = lax.broadcasted_iota(I32, (CHUNK, BAND), 1)
    idx = jnp.clip(iq - (kk - LEFT_CHUNKS * CHUNK), -MAX_REL, MAX_REL) + MAX_REL
    for hh in range(2):
        h = 2 * p + hh

        def body(j, acc):
            return acc + jnp.where(idx == j, tbl_ref[h, j], 0.0)

        o_ref[0, hh * CHUNK:(hh + 1) * CHUNK, :] = lax.fori_loop(
            0, 2 * MAX_REL + 1, body, jnp.zeros((CHUNK, BAND), F32))


def rel_bias(rel_table):
    pairs = D_HEADS // 2
    return pl.pallas_call(
        _rel_bias_kernel,
        out_shape=jax.ShapeDtypeStruct((pairs, 2 * CHUNK, BAND), F32),
        grid=(pairs,),
        in_specs=[pl.BlockSpec(memory_space=pltpu.SMEM)],
        out_specs=pl.BlockSpec((1, 2 * CHUNK, BAND), lambda p: (p, 0, 0)),
        compiler_params=_cparams(("parallel",)),
        name="rel_bias",
    )(rel_table.astype(F32))


def _chunk_attn_kernel(q_ref, k_ref, v_ref, bias_ref, o_ref, kpad, vpad, *, group, seq):
    i = pl.program_id(2)
    pad = LEFT_CHUNKS * CHUNK

    @pl.when(i == 0)
    def _():
        kpad[0:pad, :] = jnp.zeros((pad, LANES), BF16)
        vpad[0:pad, :] = jnp.zeros((pad, LANES), BF16)
        kpad[pad:, :] = k_ref[...]
        vpad[pad:, :] = v_ref[...]

    lane = lax.broadcasted_iota(I32, (CHUNK, LANES), 1)
    col_chunk = lax.broadcasted_iota(I32, (2 * CHUNK, BAND), 1) // CHUNK
    bias = bias_ref[0]
    for g in range(group):
        c = i * group + g
        qs = q_ref[g * CHUNK:(g + 1) * CHUNK, :] * (HEAD_DIM ** -0.5)
        zero = jnp.zeros_like(qs)
        q2 = jnp.concatenate([jnp.where(lane < HEAD_DIM, qs, zero),
                              jnp.where(lane >= HEAD_DIM, qs, zero)], axis=0)
        start = pl.multiple_of(c * CHUNK, CHUNK)
        kw = kpad[pl.ds(start, BAND), :]
        vw = vpad[pl.ds(start, BAND), :]
        s = _nt_dot(q2, kw) + bias
        s = jnp.where(col_chunk >= LEFT_CHUNKS - c, s, NEG)
        m = jnp.max(s, axis=1, keepdims=True)
        p = jnp.exp(s - m)
        denom = jnp.sum(p, axis=1, keepdims=True)
        o = jnp.dot(p.astype(BF16), vw, preferred_element_type=F32) / denom
        o_ref[g * CHUNK:(g + 1) * CHUNK, :] = jnp.where(lane < HEAD_DIM, o[0:CHUNK, :],
                                                        o[CHUNK:, :]).astype(o_ref.dtype)


def chunk_rel_attention(proj, bias, batch, seq, group=4):
    t = proj.shape[0]
    pairs = D_HEADS // 2
    q_off = (2 * C_CHANNELS) // LANES
    k_off = q_off + pairs
    v_off = k_off + pairs
    tq = group * CHUNK
    nq = seq // tq
    return pl.pallas_call(
        functools.partial(_chunk_attn_kernel, group=group, seq=seq),
        out_shape=jax.ShapeDtypeStruct((t, D_HEADS * HEAD_DIM), BF16),
        grid=(batch, pairs, nq),
        in_specs=[
            pl.BlockSpec((tq, LANES), lambda b, p, i: (b * nq + i, q_off + p)),
            pl.BlockSpec((seq, LANES), lambda b, p, i: (b, k_off + p)),
            pl.BlockSpec((seq, LANES), lambda b, p, i: (b, v_off + p)),
            pl.BlockSpec((1, 2 * CHUNK, BAND), lambda b, p, i: (p, 0, 0)),
        ],
        out_specs=pl.BlockSpec((tq, LANES), lambda b, p, i: (b * nq + i, p)),
        scratch_shapes=[pltpu.VMEM((seq + LEFT_CHUNKS * CHUNK, LANES), BF16),
                        pltpu.VMEM((seq + LEFT_CHUNKS * CHUNK, LANES), BF16)],
        compiler_params=_cparams(("parallel", "parallel", "arbitrary")),
        name="chunk_attn",
    )(proj, proj, proj, bias)


def _store_row_tiles(ref, val):
    n, d = val.shape
    sub = d // LANES
    for k in range(sub):
        ref[pl.ds(k, n, stride=sub), :] = val[:, k * LANES:(k + 1) * LANES]


def _load_row_tiles(ref, n, d):
    sub = d // LANES
    return jnp.concatenate([ref[pl.ds(k, n, stride=sub), :] for k in range(sub)], axis=1)


def _router_kernel(x_ref, g_ref, wr_ref, h_ref, meta_ref, cnt_ref, carry_scr, tri_scr, *, tr):
    i = pl.program_id(0)

    @pl.when(i == 0)
    def _():
        carry_scr[...] = jnp.zeros(carry_scr.shape, F32)
        r_i = lax.broadcasted_iota(I32, (tr, tr), 0)
        c_i = lax.broadcasted_iota(I32, (tr, tr), 1)
        tri_scr[...] = jnp.where(r_i < c_i, 1.0, 0.0).astype(BF16)

    x = x_ref[...]
    hn = x * lax.rsqrt(jnp.mean(x * x, axis=-1, keepdims=True) + EPS) * g_ref[...]
    _store_row_tiles(h_ref, hn)
    h_hi = hn.astype(BF16)
    h_lo = (hn - h_hi.astype(F32)).astype(BF16)
    w = wr_ref[...]
    w_hi = w.astype(BF16)
    w_lo = (w - w_hi.astype(F32)).astype(BF16)
    logits = _nt_dot(w_hi, h_hi) + (_nt_dot(w_hi, h_lo) + _nt_dot(w_lo, h_hi))
    eidx = lax.broadcasted_iota(I32, logits.shape, 0).astype(F32)
    m1 = jnp.max(logits, axis=0, keepdims=True)
    i1 = jnp.min(jnp.where(logits == m1, eidx, float(N_EXPERTS)), axis=0, keepdims=True)
    rest = jnp.where(eidx == i1, -jnp.inf, logits)
    m2 = jnp.max(rest, axis=0, keepdims=True)
    i2 = jnp.min(jnp.where(rest == m2, eidx, float(N_EXPERTS)), axis=0, keepdims=True)
    e = jnp.exp(m2 - m1)
    g1 = 1.0 / (1.0 + e)
    g2 = e / (1.0 + e)
    sel1 = eidx == i1
    sel2 = eidx == i2
    onehot = jnp.where(sel1 | sel2, 1.0, 0.0)
    rank = jnp.dot(onehot.astype(BF16), tri_scr[...], preferred_element_type=F32) + carry_scr[...]
    r1 = jnp.sum(jnp.where(sel1, rank, 0.0), axis=0, keepdims=True)
    r2 = jnp.sum(jnp.where(sel2, rank, 0.0), axis=0, keepdims=True)
    carry = carry_scr[...] + jnp.sum(onehot, axis=1, keepdims=True)
    carry_scr[...] = carry
    zero = jnp.zeros_like(g1)
    meta_ref[0] = jnp.concatenate([i1, i2, r1, r2, g1, g2, zero, zero], axis=0)
    cnt_ref[...] = jnp.broadcast_to(carry, cnt_ref.shape)


def router(x2, g, w_router, tr=512):
    t, d = x2.shape
    tr = min(tr, t)
    sub = d // LANES
    return pl.pallas_call(
        functools.partial(_router_kernel, tr=tr),
        out_shape=(jax.ShapeDtypeStruct((t * sub, LANES), F32),
                   jax.ShapeDtypeStruct((t // tr, 8, tr), F32),
                   jax.ShapeDtypeStruct((N_EXPERTS, LANES), F32)),
        grid=(t // tr,),
        in_specs=[
            pl.BlockSpec((tr, d), lambda i: (i, 0)),
            pl.BlockSpec((1, d), lambda i: (0, 0)),
            pl.BlockSpec((N_EXPERTS, d), lambda i: (0, 0)),
        ],
        out_specs=(pl.BlockSpec((tr * sub, LANES), lambda i: (i, 0)),
                   pl.BlockSpec((1, 8, tr), lambda i: (i, 0, 0)),
                   pl.BlockSpec((N_EXPERTS, LANES), lambda i: (0, 0))),
        scratch_shapes=[pltpu.VMEM((N_EXPERTS, 1), F32), pltpu.VMEM((tr, tr), BF16)],
        compiler_params=_cparams(("arbitrary",)),
        name="router",
    )(x2, g.reshape(1, d), w_router.T.astype(F32))


ROW_DMA_UNROLL = 8


def _rows(ref, row, sub):
    return ref.at[pl.ds(pl.multiple_of(row * sub, sub), sub)]


def _scatter_rows_kernel(idx1_ref, idx2_ref, src_ref, init_ref, dst_ref, sem, *, rows, sub):
    del init_ref

    def issue(r, carry):
        pltpu.make_async_copy(_rows(src_ref, r, sub), _rows(dst_ref, idx1_ref[0, 0, r], sub), sem).start()
        pltpu.make_async_copy(_rows(src_ref, r, sub), _rows(dst_ref, idx2_ref[0, 0, r], sub), sem).start()
        return carry

    lax.fori_loop(0, rows, issue, 0, unroll=ROW_DMA_UNROLL)
    for _ in range(2):
        pltpu.make_async_copy(src_ref, dst_ref.at[pl.ds(0, rows * sub)], sem).wait()


def scatter_rows(src, idx1, idx2, dst_init, sub, rows=512):
    n = idx1.shape[0]
    rows = min(rows, n)
    assert n % rows == 0 and rows % ROW_DMA_UNROLL == 0
    idx_spec = pl.BlockSpec((1, 1, rows), lambda i: (i, 0, 0), memory_space=pltpu.SMEM)
    any_spec = pl.BlockSpec(memory_space=pl.ANY)
    return pl.pallas_call(
        functools.partial(_scatter_rows_kernel, rows=rows, sub=sub),
        out_shape=jax.ShapeDtypeStruct(dst_init.shape, dst_init.dtype),
        grid=(n // rows,),
        in_specs=[idx_spec, idx_spec, pl.BlockSpec((rows * sub, LANES), lambda i: (i, 0)), any_spec],
        out_specs=any_spec,
        scratch_shapes=[pltpu.SemaphoreType.DMA(())],
        input_output_aliases={3: 0},
        compiler_params=_cparams(("arbitrary",)),
        name="row_scatter",
    )(idx1.reshape(n // rows, 1, rows), idx2.reshape(n // rows, 1, rows), src, dst_init)


def _expert_ffn_kernel(te_ref, nt_ref, x_ref, wg_ref, wu_ref, wd_ref, o_ref, xn_scr, h_scr, acc_scr, *, ff_chunk):
    n = pl.program_id(0)
    f = pl.program_id(1)
    tm, d = acc_scr.shape

    @pl.when(n < nt_ref[0])
    def _():
        @pl.when(f == 0)
        def _():
            xn_scr[...] = _load_row_tiles(x_ref, tm, d).astype(BF16)

        _swiglu_chunks(xn_scr[...], wg_ref.at[0], wu_ref.at[0], h_scr, ff_chunk)
        part = jnp.dot(h_scr[...], wd_ref[0], preferred_element_type=F32)

        @pl.when(f == 0)
        def _():
            acc_scr[...] = part

        @pl.when(f > 0)
        def _():
            acc_scr[...] += part

        @pl.when(f == pl.num_programs(1) - 1)
        def _():
            _store_row_tiles(o_ref, acc_scr[...])

    @pl.when((n >= nt_ref[0]) & (f == pl.num_programs(1) - 1))
    def _():
        o_ref[...] = jnp.zeros(o_ref.shape, o_ref.dtype)


def expert_ffn(xs, tile_expert, n_tiles_used, wg, wu, wd, tm, ff_split=2):
    d, d_ff = wg.shape[1], wg.shape[2]
    sub = d // LANES
    nr = xs.shape[0] // sub
    ff_blk = d_ff // ff_split
    ff_chunk = 256
    assert nr % tm == 0 and ff_blk % ff_chunk == 0
    grid_spec = pltpu.PrefetchScalarGridSpec(
        num_scalar_prefetch=2,
        grid=(nr // tm, ff_split),
        in_specs=[
            pl.BlockSpec((tm * sub, LANES), lambda n, f, te, nt: (n, 0)),
            pl.BlockSpec((1, d, ff_blk), lambda n, f, te, nt: (te[n], 0, f)),
            pl.BlockSpec((1, d, ff_blk), lambda n, f, te, nt: (te[n], 0, f)),
            pl.BlockSpec((1, ff_blk, d), lambda n, f, te, nt: (te[n], f, 0)),
        ],
        out_specs=pl.BlockSpec((tm * sub, LANES), lambda n, f, te, nt: (n, 0)),
        scratch_shapes=[pltpu.VMEM((tm, d), BF16), pltpu.VMEM((tm, ff_blk), BF16), pltpu.VMEM((tm, d), F32)],
    )
    return pl.pallas_call(
        functools.partial(_expert_ffn_kernel, ff_chunk=ff_chunk),
        out_shape=jax.ShapeDtypeStruct((nr * sub, LANES), F32),
        grid_spec=grid_spec,
        compiler_params=_cparams(("arbitrary", "arbitrary")),
        name="expert_ffn",
    )(tile_expert, n_tiles_used, xs, wg, wu, wd)


def _combine_kernel(idx1_ref, idx2_ref, x_ref, ys_ref, g1_ref, g2_ref, gn_ref, o_ref, y1_buf, y2_buf, sem, *,
                    final_norm):
    tm, d = x_ref.shape
    sub = d // LANES

    def issue(r, carry):
        pltpu.make_async_copy(_rows(ys_ref, idx1_ref[0, 0, r], sub), _rows(y1_buf, r, sub), sem).start()
        pltpu.make_async_copy(_rows(ys_ref, idx2_ref[0, 0, r], sub), _rows(y2_buf, r, sub), sem).start()
        return carry

    lax.fori_loop(0, tm, issue, 0, unroll=ROW_DMA_UNROLL)
    for buf in (y1_buf, y2_buf):
        pltpu.make_async_copy(ys_ref.at[pl.ds(0, tm * sub)], buf, sem).wait()
    y = (x_ref[...] + g1_ref[...] * _load_row_tiles(y1_buf, tm, d)
         + g2_ref[...] * _load_row_tiles(y2_buf, tm, d))
    if final_norm:
        y = y * lax.rsqrt(jnp.mean(y * y, axis=-1, keepdims=True) + EPS) * gn_ref[...]
    o_ref[...] = y


def combine(x2, ys, idx1, idx2, g1, g2, gn, final_norm, tm=512):
    t, d = x2.shape
    tm = min(tm, t)
    sub = d // LANES
    assert t % tm == 0 and tm % ROW_DMA_UNROLL == 0 and ys.shape[0] >= tm * sub
    big = pl.BlockSpec((tm, d), lambda i: (i, 0))
    col = pl.BlockSpec((tm, 1), lambda i: (i, 0))
    idx_spec = pl.BlockSpec((1, 1, tm), lambda i: (i, 0, 0), memory_space=pltpu.SMEM)
    return pl.pallas_call(
        functools.partial(_combine_kernel, final_norm=final_norm),
        out_shape=jax.ShapeDtypeStruct((t, d), F32),
        grid=(t // tm,),
        in_specs=[idx_spec, idx_spec, big, pl.BlockSpec(memory_space=pl.ANY), col, col,
                  pl.BlockSpec((1, d), lambda i: (0, 0))],
        out_specs=big,
        scratch_shapes=[pltpu.VMEM((tm * sub, LANES), F32), pltpu.VMEM((tm * sub, LANES), F32),
                        pltpu.SemaphoreType.DMA(())],
        compiler_params=_cparams(("arbitrary",)),
        name="moe_combine",
    )(idx1.reshape(t // tm, 1, tm), idx2.reshape(t // tm, 1, tm), x2, ys, g1, g2, gn.reshape(1, d))


def moe_residual(x2, g, w_router, wg, wu, wd, final_gain, final_norm, tm=512):
    t, d = x2.shape
    tm = min(tm, t)
    hn, meta, counts = router(x2, g, w_router)
    tr = meta.shape[2]
    field = lambda r: meta[:, r, :].reshape(t)
    idx1, idx2 = field(0).astype(I32), field(1).astype(I32)
    rank1, rank2 = field(2).astype(I32), field(3).astype(I32)
    gate1, gate2 = field(4), field(5)
    cnt = counts[:, 0].astype(I32)
    padded = ((cnt + tm - 1) // tm) * tm
    ends = jnp.cumsum(padded)
    offs = ends - padded
    pos1 = offs[idx1] + rank1
    pos2 = offs[idx2] + rank2
    n_rows = TOP_K * t + N_EXPERTS * tm
    tile_start = jnp.arange(n_rows // tm, dtype=I32) * tm
    tile_expert = jnp.minimum(jnp.sum(tile_start[:, None] >= ends[None, :], axis=1), N_EXPERTS - 1).astype(I32)
    n_tiles_used = (ends[-1:] // tm).astype(I32)

    sub = d // LANES
    xs = scatter_rows(hn, pos1, pos2, jnp.zeros((n_rows * sub, LANES), F32), sub)
    ys = expert_ffn(xs, tile_expert, n_tiles_used, wg, wu, wd, tm)
    return combine(x2, ys, pos1, pos2, gate1.reshape(t, 1), gate2.reshape(t, 1), final_gain, final_norm)


def kernel(x, ev_norm_mix, ev_w_in, ev_lambda_q1, ev_lambda_k1, ev_lambda_q2, ev_lambda_k2, ev_subln, ev_w_out,
           ev_norm_ffn, ev_ffn_gate, ev_ffn_up, ev_ffn_down, od_norm_mix, od_w_in, od_conv_w, od_conv_b,
           od_conv_ln_g, od_conv_ln_b, od_rel_bias, od_w_out, od_norm_ffn, od_router, od_exp_gate, od_exp_up,
           od_exp_down, final_norm):
    batch, seq, d = x.shape
    depth = 2 * ev_w_in.shape[0]
    assert od_w_in.shape[0] * 2 == depth
    x2 = x.reshape(batch * seq, d)
    bf = lambda a: a.astype(BF16)
    a_qk = A_HEADS * 2 * HEAD_DIM
    a_v = A_HEADS * 2 * HEAD_DIM
    b_w = B_HEADS * HEAD_DIM
    attn_q_block = 512
    attn_k_block = 256
    for layer in range(depth):
        i = layer // 2
        if layer % 2 == 0:
            lambda_init = 0.8 - 0.6 * math.exp(-0.3 * layer)
            w = ev_w_in[i]
            va0, qb0 = 2 * a_qk, 2 * a_qk + a_v
            vb0 = qb0 + 2 * b_w
            w_qk = bf(jnp.concatenate([w[:, 0:va0], w[:, qb0:vb0]], axis=1))
            w_vt = bf(jnp.concatenate([w[:, va0:qb0], w[:, vb0:]], axis=1).T)
            proj, vt = norm_proj(x2, ev_norm_mix[i], w_qk, seq, rope_cols=2 * a_qk, wvt_bf16=w_vt,
                                 tk=attn_k_block)
            oa = diff_attention(proj, vt, ev_lambda_q1[i], ev_lambda_k1[i], ev_lambda_q2[i], ev_lambda_k2[i],
                                ev_subln[i], batch, seq, lambda_init, tq=attn_q_block)
            ob = stick_breaking(proj, vt, batch, seq, tq=attn_q_block)
            x2 = out_proj_residual(x2, oa, ob, bf(ev_w_out[i]))
            x2 = ffn_residual(x2, ev_norm_ffn[i], bf(ev_ffn_gate[i]), bf(ev_ffn_up[i]), bf(ev_ffn_down[i]))
        else:
            proj = norm_proj(x2, od_norm_mix[i], bf(od_w_in[i]), seq)
            c = conv_module(proj, od_conv_w[i], od_conv_b[i], od_conv_ln_g[i], od_conv_ln_b[i], seq)
            od = chunk_rel_attention(proj, rel_bias(od_rel_bias[i]), batch, seq)
            x2 = out_proj_residual(x2, c, od, bf(od_w_out[i]))
            x2 = moe_residual(x2, od_norm_ffn[i], od_router[i], bf(od_exp_gate[i]), bf(od_exp_up[i]),
                              bf(od_exp_down[i]), final_norm, final_norm=(layer == depth - 1))
    return x2.reshape(batch, seq, d)
```

```python
import functools
import math

import numpy as np
import jax
import jax.numpy as jnp
from jax import lax
from jax.experimental import pallas as pl
from jax.experimental.pallas import tpu as pltpu

F32 = jnp.float32
BF16 = jnp.bfloat16
I32 = jnp.int32

HEAD_DIM = 64
CHUNK = 64
ROPE_THETA = 10000.0
EPS = 1e-6
A_HEADS = 4
B_HEADS = 8
C_CHANNELS = 512
CONV_WIDTH = 31
D_HEADS = 8
LEFT_CHUNKS = 8
BAND = (LEFT_CHUNKS + 1) * CHUNK
MAX_REL = 128
N_EXPERTS = 8
TOP_K = 2

LANES = 128
NEG = -1e30
LOG2E = math.log2(math.e)
SIGN_BIT = np.int32(-2 ** 31)
VMEM_LIMIT = 56 * 1024 * 1024


def _cparams(sem):
    return pltpu.CompilerParams(dimension_semantics=sem, vmem_limit_bytes=VMEM_LIMIT)


def _nt_dot(a, b):
    return lax.dot_general(a, b, (((1,), (1,)), ((), ())), preferred_element_type=F32)


def _norm_proj_kernel(x_ref, g_ref, w_ref, cos_ref, sin_ref, *rest, col_chunk, rope_cols, tk):
    if tk:
        wvt_ref, o_ref, vt_ref = rest
    else:
        (o_ref,) = rest
    x = x_ref[...]
    xn = (x * lax.rsqrt(jnp.mean(x * x, axis=-1, keepdims=True) + EPS) * g_ref[...]).astype(BF16)
    n_out = o_ref.shape[1]
    if tk:
        for r0 in range(0, wvt_ref.shape[0], col_chunk):
            res = _nt_dot(wvt_ref[r0:r0 + col_chunk, :], xn)
            for b in range(vt_ref.shape[0]):
                vt_ref[b, r0:r0 + col_chunk, :] = res[:, b * tk:(b + 1) * tk].astype(BF16)
    if rope_cols:
        cos = cos_ref[...]
        sin = sin_ref[...]
        lane = lax.broadcasted_iota(I32, cos.shape, 1)
        first_half = (lane % HEAD_DIM) < (HEAD_DIM // 2)
    for c0 in range(0, n_out, col_chunk):
        r = jnp.dot(xn, w_ref[:, c0:c0 + col_chunk], preferred_element_type=F32)
        if c0 < rope_cols:
            parts = []
            for l0 in range(0, col_chunk, LANES):
                seg = r[:, l0:l0 + LANES]
                partner = jnp.where(first_half,
                                    pltpu.roll(seg, LANES - HEAD_DIM // 2, 1),
                                    pltpu.roll(seg, HEAD_DIM // 2, 1))
                parts.append(seg * cos + partner * sin)
            r = jnp.concatenate(parts, axis=1)
        o_ref[:, c0:c0 + col_chunk] = r.astype(o_ref.dtype)


def _rope_tables(seq):
    half = HEAD_DIM // 2
    inv_freq = ROPE_THETA ** (-jnp.arange(half, dtype=F32) * 2.0 / HEAD_DIM)
    ang = jnp.arange(seq, dtype=F32)[:, None] * inv_freq[None, :]
    cos, sin = jnp.cos(ang), jnp.sin(ang)
    cos_t = jnp.tile(jnp.concatenate([cos, cos], axis=1), (1, LANES // HEAD_DIM))
    sin_t = jnp.tile(jnp.concatenate([-sin, sin], axis=1), (1, LANES // HEAD_DIM))
    return cos_t, sin_t


def norm_proj(x2, g, w_bf16, seq, rope_cols=0, wvt_bf16=None, tk=0, tm=512):
    t, d = x2.shape
    n_out = w_bf16.shape[1]
    tm = min(tm, seq)
    col_chunk = 512
    assert t % tm == 0 and seq % tm == 0 and n_out % col_chunk == 0 and rope_cols % col_chunk == 0
    cos_t, sin_t = _rope_tables(seq)
    tiles_per_seq = seq // tm
    in_specs = [
        pl.BlockSpec((tm, d), lambda i: (i, 0)),
        pl.BlockSpec((1, d), lambda i: (0, 0)),
        pl.BlockSpec((d, n_out), lambda i: (0, 0)),
        pl.BlockSpec((tm, LANES), lambda i: (i % tiles_per_seq, 0)),
        pl.BlockSpec((tm, LANES), lambda i: (i % tiles_per_seq, 0)),
    ]
    args = [x2, g.reshape(1, d), w_bf16, cos_t, sin_t]
    out_shape = jax.ShapeDtypeStruct((t, n_out), BF16)
    out_specs = pl.BlockSpec((tm, n_out), lambda i: (i, 0))
    if wvt_bf16 is not None:
        n_v = wvt_bf16.shape[0]
        tk = min(tk, tm)
        assert tm % tk == 0 and n_v % col_chunk == 0
        in_specs.append(pl.BlockSpec((n_v, d), lambda i: (0, 0)))
        args.append(wvt_bf16)
        out_shape = (out_shape, jax.ShapeDtypeStruct((t // tk, n_v, tk), BF16))
        out_specs = (out_specs, pl.BlockSpec((tm // tk, n_v, tk), lambda i: (i, 0, 0)))
    return pl.pallas_call(
        functools.partial(_norm_proj_kernel, col_chunk=col_chunk, rope_cols=rope_cols,
                          tk=tk if wvt_bf16 is not None else 0),
        out_shape=out_shape,
        grid=(t // tm,),
        in_specs=in_specs,
        out_specs=out_specs,
        compiler_params=_cparams(("parallel",)),
        name="norm_proj",
    )(*args)


def _stack_query_pair(q_ref, q2_scr, tq):
    lane = lax.broadcasted_iota(I32, (tq, LANES), 1)
    qs = (q_ref[...].astype(F32) * (HEAD_DIM ** -0.5 * LOG2E)).astype(BF16)
    zero = jnp.zeros_like(qs)
    q2_scr[0:tq, :] = jnp.where(lane < HEAD_DIM, qs, zero)
    q2_scr[tq:, :] = jnp.where(lane >= HEAD_DIM, qs, zero)


def _diff_attn_kernel(q_ref, k_ref, vt_ref, lq1_ref, lk1_ref, lq2_ref, lk2_ref, g_ref, o_ref,
                      q2_scr, m_scr, acc_scr, st0_scr, st1_scr, *, tq, tk, lambda_init):
    i = pl.program_id(2)
    assert tq == 2 * tk
    _stack_query_pair(q_ref, q2_scr, tq)
    m_scr[...] = jnp.full(m_scr.shape, NEG, F32)
    acc_scr[...] = jnp.zeros(acc_scr.shape, F32)
    ones = jnp.ones((LANES, tk), BF16)

    def scores(j, st_scr):
        k = k_ref[pl.ds(pl.multiple_of(j * tk, tk), tk), :]
        st_scr[...] = _nt_dot(k, q2_scr[...])

    def consume(j, st_scr, masked):
        st = st_scr[...]
        if masked:
            key = lax.broadcasted_iota(I32, st.shape, 0) + j * tk
            qry = lax.broadcasted_iota(I32, st.shape, 1) % tq + i * tq
            st = jnp.where(key // CHUNK <= qry // CHUNK, st, NEG)
        m_old = m_scr[...]
        m_new = jnp.maximum(m_old, jnp.max(st, axis=0, keepdims=True))
        alpha = jnp.exp2(m_old - m_new)
        pt = jnp.exp2(st - m_new).astype(BF16)
        vt_ext = jnp.concatenate([vt_ref[j], ones], axis=0)
        acc_scr[...] = alpha * acc_scr[...] + jnp.dot(vt_ext, pt, preferred_element_type=F32)
        m_scr[...] = m_new

    scores(0, st0_scr)

    def body(n, carry):
        scores(2 * n + 1, st1_scr)
        consume(2 * n, st0_scr, False)
        scores(2 * n + 2, st0_scr)
        consume(2 * n + 1, st1_scr, False)
        return carry

    lax.fori_loop(0, i, body, 0)
    scores(2 * i + 1, st1_scr)
    consume(2 * i, st0_scr, True)
    consume(2 * i + 1, st1_scr, True)

    acc = acc_scr[...]
    ot = acc[0:LANES, :] / acc[LANES:, :]
    lam = (jnp.exp(jnp.sum(lq1_ref[...] * lk1_ref[...], axis=1, keepdims=True))
           - jnp.exp(jnp.sum(lq2_ref[...] * lk2_ref[...], axis=1, keepdims=True)) + lambda_init)
    od = ot[:, 0:tq].T - lam * ot[:, tq:].T
    y = od * lax.rsqrt(jnp.mean(od * od, axis=-1, keepdims=True) + EPS) * g_ref[...]
    o_ref[...] = (y * (1.0 - lambda_init)).astype(o_ref.dtype)


def diff_attention(proj, vt, lq1, lk1, lq2, lk2, subln_g, batch, seq, lambda_init):
    t = proj.shape[0]
    tk = vt.shape[2]
    tq = 2 * tk
    nq = seq // tq
    assert seq % tq == 0 and tk % CHUNK == 0
    k_off = (A_HEADS * 2 * HEAD_DIM) // LANES
    vec = lambda a: a.reshape(1, -1).astype(F32)
    small = lambda n: pl.BlockSpec((1, n), lambda b, h, i: (0, 0))
    return pl.pallas_call(
        functools.partial(_diff_attn_kernel, tq=tq, tk=tk, lambda_init=lambda_init),
        out_shape=jax.ShapeDtypeStruct((t, A_HEADS * 2 * HEAD_DIM), BF16),
        grid=(batch, A_HEADS, nq),
        in_specs=[
            pl.BlockSpec((tq, LANES), lambda b, h, i: (b * nq + i, h)),
            pl.BlockSpec((seq, LANES), lambda b, h, i: (b, k_off + h)),
            pl.BlockSpec((seq // tk, LANES, tk), lambda b, h, i: (b, h, 0)),
            small(HEAD_DIM), small(HEAD_DIM), small(HEAD_DIM), small(HEAD_DIM), small(2 * HEAD_DIM),
        ],
        out_specs=pl.BlockSpec((tq, LANES), lambda b, h, i: (b * nq + i, h)),
        scratch_shapes=[
            pltpu.VMEM((2 * tq, LANES), BF16),
            pltpu.VMEM((1, 2 * tq), F32),
            pltpu.VMEM((2 * LANES, 2 * tq), F32),
            pltpu.VMEM((tk, 2 * tq), F32),
            pltpu.VMEM((tk, 2 * tq), F32),
        ],
        compiler_params=_cparams(("parallel", "parallel", "arbitrary")),
        name="diff_attn",
    )(proj, proj, vt, vec(lq1), vec(lk1), vec(lq2), vec(lk2), vec(subln_g))


def _stick_kernel(q_ref, k_ref, vt_ref, o_ref, q2_scr, tri_scr, c_scr, acc_scr, st0_scr, st1_scr, *, tq, tk):
    i = pl.program_id(2)
    assert tq == 2 * tk
    _stack_query_pair(q_ref, q2_scr, tq)
    r_i = lax.broadcasted_iota(I32, (tk, tk), 0)
    c_i = lax.broadcasted_iota(I32, (tk, tk), 1)
    tri_scr[...] = jnp.where(c_i > r_i, 1.0, 0.0).astype(BF16)
    c_scr[...] = jnp.zeros(c_scr.shape, F32)
    acc_scr[...] = jnp.zeros(acc_scr.shape, F32)

    def scores(j, st_scr):
        k = k_ref[pl.ds(pl.multiple_of(j * tk, tk), tk), :]
        st_scr[...] = _nt_dot(k, q2_scr[...])

    def consume(j, st_scr, masked):
        zt = st_scr[...]
        neg_abs = lax.bitcast_convert_type(lax.bitcast_convert_type(zt, I32) | SIGN_BIT, F32)
        sp = jnp.maximum(zt, 0.0) + jnp.log2(1.0 + jnp.exp2(neg_abs))
        if masked:
            key = lax.broadcasted_iota(I32, zt.shape, 0) + j * tk
            qry = lax.broadcasted_iota(I32, zt.shape, 1) % tq + i * tq
            valid = key < qry
            spm = jnp.where(valid, sp, 0.0)
        else:
            spm = sp
        after = jnp.dot(tri_scr[...], spm.astype(BF16), preferred_element_type=F32)
        w = jnp.exp2(zt - sp - after - c_scr[...])
        if masked:
            w = jnp.where(valid, w, 0.0)
        acc_scr[...] += jnp.dot(vt_ref[j], w.astype(BF16), preferred_element_type=F32)
        c_scr[...] += jnp.sum(spm, axis=0, keepdims=True)

    scores(2 * i + 1, st0_scr)
    scores(2 * i, st1_scr)
    consume(2 * i + 1, st0_scr, True)
    scores(jnp.maximum(2 * i - 1, 0), st0_scr)
    consume(2 * i, st1_scr, True)

    def body(n, carry):
        j = 2 * (i - n) - 1
        scores(j - 1, st1_scr)
        consume(j, st0_scr, False)
        scores(jnp.maximum(j - 2, 0), st0_scr)
        consume(j - 1, st1_scr, False)
        return carry

    lax.fori_loop(0, i, body, 0)
    acc = acc_scr[...]
    vrow = lax.broadcasted_iota(I32, (LANES, tq), 0)
    ot = jnp.where(vrow < HEAD_DIM, acc[:, 0:tq], acc[:, tq:])
    o_ref[...] = ot.T.astype(o_ref.dtype)


def stick_breaking(proj, vt, batch, seq):
    t = proj.shape[0]
    tk = vt.shape[2]
    tq = 2 * tk
    nq = seq // tq
    assert seq % tq == 0
    pairs = (B_HEADS * HEAD_DIM) // LANES
    a_blocks = (A_HEADS * 2 * HEAD_DIM) // LANES
    q_off = 2 * a_blocks
    k_off = q_off + pairs
    v_off = a_blocks
    return pl.pallas_call(
        functools.partial(_stick_kernel, tq=tq, tk=tk),
        out_shape=jax.ShapeDtypeStruct((t, B_HEADS * HEAD_DIM), BF16),
        grid=(batch, pairs, nq),
        in_specs=[
            pl.BlockSpec((tq, LANES), lambda b, p, i: (b * nq + i, q_off + p)),
            pl.BlockSpec((seq, LANES), lambda b, p, i: (b, k_off + p)),
            pl.BlockSpec((seq // tk, LANES, tk), lambda b, p, i: (b, v_off + p, 0)),
        ],
        out_specs=pl.BlockSpec((tq, LANES), lambda b, p, i: (b * nq + i, p)),
        scratch_shapes=[
            pltpu.VMEM((2 * tq, LANES), BF16),
            pltpu.VMEM((tk, tk), BF16),
            pltpu.VMEM((1, 2 * tq), F32),
            pltpu.VMEM((LANES, 2 * tq), F32),
            pltpu.VMEM((tk, 2 * tq), F32),
            pltpu.VMEM((tk, 2 * tq), F32),
        ],
        compiler_params=_cparams(("parallel", "parallel", "arbitrary")),
        name="stick_attn",
    )(proj, proj, vt)


def _out_proj_kernel(x_ref, a_ref, b_ref, w_ref, o_ref):
    half = a_ref.shape[1]
    o_ref[...] = (x_ref[...]
                  + jnp.dot(a_ref[...], w_ref[0:half, :], preferred_element_type=F32)
                  + jnp.dot(b_ref[...], w_ref[half:, :], preferred_element_type=F32))


def out_proj_residual(x2, a, b, w_bf16, tm=512):
    t, d = x2.shape
    half = a.shape[1]
    tm = min(tm, t)
    return pl.pallas_call(
        _out_proj_kernel,
        out_shape=jax.ShapeDtypeStruct((t, d), F32),
        grid=(t // tm,),
        in_specs=[
            pl.BlockSpec((tm, d), lambda i: (i, 0)),
            pl.BlockSpec((tm, half), lambda i: (i, 0)),
            pl.BlockSpec((tm, half), lambda i: (i, 0)),
            pl.BlockSpec((2 * half, d), lambda i: (0, 0)),
        ],
        out_specs=pl.BlockSpec((tm, d), lambda i: (i, 0)),
        compiler_params=_cparams(("parallel",)),
        name="out_proj",
    )(x2, a, b, w_bf16)


def _swiglu_chunks(xn, wg_ref, wu_ref, h_scr, ff_chunk):
    d_ff = h_scr.shape[1]
    for c0 in range(0, d_ff, ff_chunk):
        g = jnp.dot(xn, wg_ref[:, c0:c0 + ff_chunk], preferred_element_type=F32)
        u = jnp.dot(xn, wu_ref[:, c0:c0 + ff_chunk], preferred_element_type=F32)
        h_scr[:, c0:c0 + ff_chunk] = (g * jax.nn.sigmoid(g) * u).astype(BF16)


def _ffn_kernel(x_ref, g_ref, wg_ref, wu_ref, wd_ref, o_ref, h_scr, *, ff_chunk):
    x = x_ref[...]
    xn = (x * lax.rsqrt(jnp.mean(x * x, axis=-1, keepdims=True) + EPS) * g_ref[...]).astype(BF16)
    _swiglu_chunks(xn, wg_ref, wu_ref, h_scr, ff_chunk)
    o_ref[...] = x + jnp.dot(h_scr[...], wd_ref[...], preferred_element_type=F32)


def ffn_residual(x2, g, wg, wu, wd, tm=512):
    t, d = x2.shape
    d_ff = wg.shape[1]
    tm = min(tm, t)
    ff_chunk = 256
    assert d_ff % ff_chunk == 0
    resident = lambda shape: pl.BlockSpec(shape, lambda i: (0, 0), pipeline_mode=pl.Buffered(1))
    return pl.pallas_call(
        functools.partial(_ffn_kernel, ff_chunk=ff_chunk),
        out_shape=jax.ShapeDtypeStruct((t, d), F32),
        grid=(t // tm,),
        in_specs=[
            pl.BlockSpec((tm, d), lambda i: (i, 0)),
            pl.BlockSpec((1, d), lambda i: (0, 0)),
            resident((d, d_ff)), resident((d, d_ff)), resident((d_ff, d)),
        ],
        out_specs=pl.BlockSpec((tm, d), lambda i: (i, 0)),
        scratch_shapes=[pltpu.VMEM((tm, d_ff), BF16)],
        compiler_params=_cparams(("parallel",)),
        name="ffn",
    )(x2, g.reshape(1, d), wg, wu, wd)


CONV_HALO = 32


def _conv_kernel(val_ref, gate_ref, pval_ref, pgate_ref, w_ref, b_ref, lg_ref, lb_ref, o_ref,
                 u_scr, c_scr, *, ts, tiles_per_seq, row_blk):
    i = pl.program_id(0)
    glu = lambda v, g: v.astype(F32) * jax.nn.sigmoid(g.astype(F32))
    keep = jnp.where(i % tiles_per_seq == 0, 0.0, 1.0)
    u_scr[0:CONV_HALO, :] = glu(pval_ref[...], pgate_ref[...]) * keep
    u_scr[CONV_HALO:, :] = glu(val_ref[...], gate_ref[...])
    shift = CONV_HALO - (CONV_WIDTH - 1)
    n_ch = val_ref.shape[1]
    for c0 in range(0, n_ch, LANES):
        for r0 in range(0, ts, row_blk):
            acc = jnp.broadcast_to(b_ref[:, c0:c0 + LANES], (row_blk, LANES))
            for j in range(CONV_WIDTH):
                acc = acc + w_ref[j:j + 1, c0:c0 + LANES] * u_scr[r0 + shift + j:r0 + shift + j + row_blk,
                                                                   c0:c0 + LANES]
            c_scr[r0:r0 + row_blk, c0:c0 + LANES] = acc
    c = c_scr[...]
    mu = jnp.mean(c, axis=-1, keepdims=True)
    xc = c - mu
    var = jnp.mean(xc * xc, axis=-1, keepdims=True)
    y = xc * lax.rsqrt(var + EPS) * lg_ref[...] + lb_ref[...]
    o_ref[...] = (y * jax.nn.sigmoid(y)).astype(o_ref.dtype)


def conv_module(proj, conv_w, conv_b, ln_g, ln_b, seq, ts=256):
    t = proj.shape[0]
    ts = min(ts, seq)
    n_ch = C_CHANNELS
    halo_blocks = ts // CONV_HALO
    row = lambda a: a.reshape(1, n_ch).astype(F32)
    small = pl.BlockSpec((1, n_ch), lambda i: (0, 0))
    prev = lambda col: pl.BlockSpec((CONV_HALO, n_ch), lambda i: (jnp.maximum(i * halo_blocks - 1, 0), col))
    return pl.pallas_call(
        functools.partial(_conv_kernel, ts=ts, tiles_per_seq=seq // ts, row_blk=64),
        out_shape=jax.ShapeDtypeStruct((t, n_ch), BF16),
        grid=(t // ts,),
        in_specs=[
            pl.BlockSpec((ts, n_ch), lambda i: (i, 0)),
            pl.BlockSpec((ts, n_ch), lambda i: (i, 1)),
            prev(0), prev(1),
            pl.BlockSpec((CONV_WIDTH, n_ch), lambda i: (0, 0)),
            small, small, small,
        ],
        out_specs=pl.BlockSpec((ts, n_ch), lambda i: (i, 0)),
        scratch_shapes=[pltpu.VMEM((ts + CONV_HALO, n_ch), F32), pltpu.VMEM((ts, n_ch), F32)],
        compiler_params=_cparams(("parallel",)),
        name="conv_module",
    )(proj, proj, proj, proj, conv_w.astype(F32), row(conv_b), row(ln_g), row(ln_b))


def _rel_bias_kernel(tbl_ref, o_ref):
    p = pl.program_id(0)
    iq = lax.broadcasted_iota(I32, (CHUNK, BAND), 0)
    kk = lax.broadcasted_iota(I32, (CHUNK, BAND), 1)
    idx = jnp.clip(iq - (kk - LEFT_CHUNKS * CHUNK), -MAX_REL, MAX_REL) + MAX_REL
    for hh in range(2):
        h = 2 * p + hh

        def body(j, acc):
            return acc + jnp.where(idx == j, tbl_ref[h, j], 0.0)

        o_ref[0, hh * CHUNK:(hh + 1) * CHUNK, :] = lax.fori_loop(
            0, 2 * MAX_REL + 1, body, jnp.zeros((CHUNK, BAND), F32))


def rel_bias(rel_table):
    pairs = D_HEADS // 2
    return pl.pallas_call(
        _rel_bias_kernel,
        out_shape=jax.ShapeDtypeStruct((pairs, 2 * CHUNK, BAND), F32),
        grid=(pairs,),
        in_specs=[pl.BlockSpec(memory_space=pltpu.SMEM)],
        out_specs=pl.BlockSpec((1, 2 * CHUNK, BAND), lambda p: (p, 0, 0)),
        compiler_params=_cparams(("parallel",)),
        name="rel_bias",
    )(rel_table.astype(F32))


def _chunk_attn_kernel(q_ref, k_ref, v_ref, bias_ref, o_ref, kpad, vpad, *, group, seq):
    i = pl.program_id(2)
    pad = LEFT_CHUNKS * CHUNK

    @pl.when(i == 0)
    def _():
        kpad[0:pad, :] = jnp.zeros((pad, LANES), BF16)
        vpad[0:pad, :] = jnp.zeros((pad, LANES), BF16)
        kpad[pad:, :] = k_ref[...]
        vpad[pad:, :] = v_ref[...]

    lane = lax.broadcasted_iota(I32, (CHUNK, LANES), 1)
    col_chunk = lax.broadcasted_iota(I32, (2 * CHUNK, BAND), 1) // CHUNK
    bias = bias_ref[0]
    for g in range(group):
        c = i * group + g
        qs = q_ref[g * CHUNK:(g + 1) * CHUNK, :] * (HEAD_DIM ** -0.5)
        zero = jnp.zeros_like(qs)
        q2 = jnp.concatenate([jnp.where(lane < HEAD_DIM, qs, zero),
                              jnp.where(lane >= HEAD_DIM, qs, zero)], axis=0)
        start = pl.multiple_of(c * CHUNK, CHUNK)
        kw = kpad[pl.ds(start, BAND), :]
        vw = vpad[pl.ds(start, BAND), :]
        s = _nt_dot(q2, kw) + bias
        s = jnp.where(col_chunk >= LEFT_CHUNKS - c, s, NEG)
        m = jnp.max(s, axis=1, keepdims=True)
        p = jnp.exp(s - m)
        denom = jnp.sum(p, axis=1, keepdims=True)
        o = jnp.dot(p.astype(BF16), vw, preferred_element_type=F32) / denom
        o_ref[g * CHUNK:(g + 1) * CHUNK, :] = jnp.where(lane < HEAD_DIM, o[0:CHUNK, :],
                                                        o[CHUNK:, :]).astype(o_ref.dtype)


def chunk_rel_attention(proj, bias, batch, seq, group=4):
    t = proj.shape[0]
    pairs = D_HEADS // 2
    q_off = (2 * C_CHANNELS) // LANES
    k_off = q_off + pairs
    v_off = k_off + pairs
    tq = group * CHUNK
    nq = seq // tq
    return pl.pallas_call(
        functools.partial(_chunk_attn_kernel, group=group, seq=seq),
        out_shape=jax.ShapeDtypeStruct((t, D_HEADS * HEAD_DIM), BF16),
        grid=(batch, pairs, nq),
        in_specs=[
            pl.BlockSpec((tq, LANES), lambda b, p, i: (b * nq + i, q_off + p)),
            pl.BlockSpec((seq, LANES), lambda b, p, i: (b, k_off + p)),
            pl.BlockSpec((seq, LANES), lambda b, p, i: (b, v_off + p)),
            pl.BlockSpec((1, 2 * CHUNK, BAND), lambda b, p, i: (p, 0, 0)),
        ],
        out_specs=pl.BlockSpec((tq, LANES), lambda b, p, i: (b * nq + i, p)),
        scratch_shapes=[pltpu.VMEM((seq + LEFT_CHUNKS * CHUNK, LANES), BF16),
                        pltpu.VMEM((seq + LEFT_CHUNKS * CHUNK, LANES), BF16)],
        compiler_params=_cparams(("parallel", "parallel", "arbitrary")),
        name="chunk_attn",
    )(proj, proj, proj, bias)


def _store_row_tiles(ref, val):
    n, d = val.shape
    sub = d // LANES
    for k in range(sub):
        ref[pl.ds(k, n, stride=sub), :] = val[:, k * LANES:(k + 1) * LANES]


def _load_row_tiles(ref, n, d):
    sub = d // LANES
    return jnp.concatenate([ref[pl.ds(k, n, stride=sub), :] for k in range(sub)], axis=1)


def _router_kernel(x_ref, g_ref, wr_ref, h_ref, meta_ref, cnt_ref, carry_scr, tri_scr, *, tr):
    i = pl.program_id(0)

    @pl.when(i == 0)
    def _():
        carry_scr[...] = jnp.zeros(carry_scr.shape, F32)
        r_i = lax.broadcasted_iota(I32, (tr, tr), 0)
        c_i = lax.broadcasted_iota(I32, (tr, tr), 1)
        tri_scr[...] = jnp.where(r_i < c_i, 1.0, 0.0).astype(BF16)

    x = x_ref[...]
    hn = x * lax.rsqrt(jnp.mean(x * x, axis=-1, keepdims=True) + EPS) * g_ref[...]
    _store_row_tiles(h_ref, hn)
    h_hi = hn.astype(BF16)
    h_lo = (hn - h_hi.astype(F32)).astype(BF16)
    w = wr_ref[...]
    w_hi = w.astype(BF16)
    w_lo = (w - w_hi.astype(F32)).astype(BF16)
    logits = _nt_dot(w_hi, h_hi) + (_nt_dot(w_hi, h_lo) + _nt_dot(w_lo, h_hi))
    eidx = lax.broadcasted_iota(I32, logits.shape, 0).astype(F32)
    m1 = jnp.max(logits, axis=0, keepdims=True)
    i1 = jnp.min(jnp.where(logits == m1, eidx, float(N_EXPERTS)), axis=0, keepdims=True)
    rest = jnp.where(eidx == i1, -jnp.inf, logits)
    m2 = jnp.max(rest, axis=0, keepdims=True)
    i2 = jnp.min(jnp.where(rest == m2, eidx, float(N_EXPERTS)), axis=0, keepdims=True)
    e = jnp.exp(m2 - m1)
    g1 = 1.0 / (1.0 + e)
    g2 = e / (1.0 + e)
    sel1 = eidx == i1
    sel2 = eidx == i2
    onehot = jnp.where(sel1 | sel2, 1.0, 0.0)
    rank = jnp.dot(onehot.astype(BF16), tri_scr[...], preferred_element_type=F32) + carry_scr[...]
    r1 = jnp.sum(jnp.where(sel1, rank, 0.0), axis=0, keepdims=True)
    r2 = jnp.sum(jnp.where(sel2, rank, 0.0), axis=0, keepdims=True)
    carry = carry_scr[...] + jnp.sum(onehot, axis=1, keepdims=True)
    carry_scr[...] = carry
    zero = jnp.zeros_like(g1)
    meta_ref[0] = jnp.concatenate([i1, i2, r1, r2, g1, g2, zero, zero], axis=0)
    cnt_ref[...] = jnp.broadcast_to(carry, cnt_ref.shape)


def router(x2, g, w_router, tr=512):
    t, d = x2.shape
    tr = min(tr, t)
    sub = d // LANES
    return pl.pallas_call(
        functools.partial(_router_kernel, tr=tr),
        out_shape=(jax.ShapeDtypeStruct((t * sub, LANES), F32),
                   jax.ShapeDtypeStruct((t // tr, 8, tr), F32),
                   jax.ShapeDtypeStruct((N_EXPERTS, LANES), F32)),
        grid=(t // tr,),
        in_specs=[
            pl.BlockSpec((tr, d), lambda i: (i, 0)),
            pl.BlockSpec((1, d), lambda i: (0, 0)),
            pl.BlockSpec((N_EXPERTS, d), lambda i: (0, 0)),
        ],
        out_specs=(pl.BlockSpec((tr * sub, LANES), lambda i: (i, 0)),
                   pl.BlockSpec((1, 8, tr), lambda i: (i, 0, 0)),
                   pl.BlockSpec((N_EXPERTS, LANES), lambda i: (0, 0))),
        scratch_shapes=[pltpu.VMEM((N_EXPERTS, 1), F32), pltpu.VMEM((tr, tr), BF16)],
        compiler_params=_cparams(("arbitrary",)),
        name="router",
    )(x2, g.reshape(1, d), w_router.T.astype(F32))


ROW_DMA_UNROLL = 8


def _rows(ref, row, sub):
    return ref.at[pl.ds(pl.multiple_of(row * sub, sub), sub)]


def _scatter_rows_kernel(idx1_ref, idx2_ref, src_ref, init_ref, dst_ref, sem, *, rows, sub):
    del init_ref

    def issue(r, carry):
        pltpu.make_async_copy(_rows(src_ref, r, sub), _rows(dst_ref, idx1_ref[0, 0, r], sub), sem).start()
        pltpu.make_async_copy(_rows(src_ref, r, sub), _rows(dst_ref, idx2_ref[0, 0, r], sub), sem).start()
        return carry

    lax.fori_loop(0, rows, issue, 0, unroll=ROW_DMA_UNROLL)
    for _ in range(2):
        pltpu.make_async_copy(src_ref, dst_ref.at[pl.ds(0, rows * sub)], sem).wait()


def scatter_rows(src, idx1, idx2, dst_init, sub, rows=512):
    n = idx1.shape[0]
    rows = min(rows, n)
    assert n % rows == 0 and rows % ROW_DMA_UNROLL == 0
    idx_spec = pl.BlockSpec((1, 1, rows), lambda i: (i, 0, 0), memory_space=pltpu.SMEM)
    any_spec = pl.BlockSpec(memory_space=pl.ANY)
    return pl.pallas_call(
        functools.partial(_scatter_rows_kernel, rows=rows, sub=sub),
        out_shape=jax.ShapeDtypeStruct(dst_init.shape, dst_init.dtype),
        grid=(n // rows,),
        in_specs=[idx_spec, idx_spec, pl.BlockSpec((rows * sub, LANES), lambda i: (i, 0)), any_spec],
        out_specs=any_spec,
        scratch_shapes=[pltpu.SemaphoreType.DMA(())],
        input_output_aliases={3: 0},
        compiler_params=_cparams(("arbitrary",)),
        name="row_scatter",
    )(idx1.reshape(n // rows, 1, rows), idx2.reshape(n // rows, 1, rows), src, dst_init)


def _expert_ffn_kernel(te_ref, nt_ref, x_ref, wg_ref, wu_ref, wd_ref, o_ref, xn_scr, h_scr, acc_scr, *, ff_chunk):
    n = pl.program_id(0)
    f = pl.program_id(1)
    tm, d = acc_scr.shape

    @pl.when(n < nt_ref[0])
    def _():
        @pl.when(f == 0)
        def _():
            xn_scr[...] = _load_row_tiles(x_ref, tm, d).astype(BF16)

        _swiglu_chunks(xn_scr[...], wg_ref.at[0], wu_ref.at[0], h_scr, ff_chunk)
        part = jnp.dot(h_scr[...], wd_ref[0], preferred_element_type=F32)

        @pl.when(f == 0)
        def _():
            acc_scr[...] = part

        @pl.when(f > 0)
        def _():
            acc_scr[...] += part

        @pl.when(f == pl.num_programs(1) - 1)
        def _():
            _store_row_tiles(o_ref, acc_scr[...])

    @pl.when((n >= nt_ref[0]) & (f == pl.num_programs(1) - 1))
    def _():
        o_ref[...] = jnp.zeros(o_ref.shape, o_ref.dtype)


def expert_ffn(xs, tile_expert, n_tiles_used, wg, wu, wd, tm, ff_split=2):
    d, d_ff = wg.shape[1], wg.shape[2]
    sub = d // LANES
    nr = xs.shape[0] // sub
    ff_blk = d_ff // ff_split
    ff_chunk = 256
    assert nr % tm == 0 and ff_blk % ff_chunk == 0
    grid_spec = pltpu.PrefetchScalarGridSpec(
        num_scalar_prefetch=2,
        grid=(nr // tm, ff_split),
        in_specs=[
            pl.BlockSpec((tm * sub, LANES), lambda n, f, te, nt: (n, 0)),
            pl.BlockSpec((1, d, ff_blk), lambda n, f, te, nt: (te[n], 0, f)),
            pl.BlockSpec((1, d, ff_blk), lambda n, f, te, nt: (te[n], 0, f)),
            pl.BlockSpec((1, ff_blk, d), lambda n, f, te, nt: (te[n], f, 0)),
        ],
        out_specs=pl.BlockSpec((tm * sub, LANES), lambda n, f, te, nt: (n, 0)),
        scratch_shapes=[pltpu.VMEM((tm, d), BF16), pltpu.VMEM((tm, ff_blk), BF16), pltpu.VMEM((tm, d), F32)],
    )
    return pl.pallas_call(
        functools.partial(_expert_ffn_kernel, ff_chunk=ff_chunk),
        out_shape=jax.ShapeDtypeStruct((nr * sub, LANES), F32),
        grid_spec=grid_spec,
        compiler_params=_cparams(("arbitrary", "arbitrary")),
        name="expert_ffn",
    )(tile_expert, n_tiles_used, xs, wg, wu, wd)


def _combine_kernel(idx1_ref, idx2_ref, x_ref, ys_ref, g1_ref, g2_ref, gn_ref, o_ref, y1_buf, y2_buf, sem, *,
                    final_norm):
    tm, d = x_ref.shape
    sub = d // LANES

    def issue(r, carry):
        pltpu.make_async_copy(_rows(ys_ref, idx1_ref[0, 0, r], sub), _rows(y1_buf, r, sub), sem).start()
        pltpu.make_async_copy(_rows(ys_ref, idx2_ref[0, 0, r], sub), _rows(y2_buf, r, sub), sem).start()
        return carry

    lax.fori_loop(0, tm, issue, 0, unroll=ROW_DMA_UNROLL)
    for buf in (y1_buf, y2_buf):
        pltpu.make_async_copy(ys_ref.at[pl.ds(0, tm * sub)], buf, sem).wait()
    y = (x_ref[...] + g1_ref[...] * _load_row_tiles(y1_buf, tm, d)
         + g2_ref[...] * _load_row_tiles(y2_buf, tm, d))
    if final_norm:
        y = y * lax.rsqrt(jnp.mean(y * y, axis=-1, keepdims=True) + EPS) * gn_ref[...]
    o_ref[...] = y


def combine(x2, ys, idx1, idx2, g1, g2, gn, final_norm, tm=512):
    t, d = x2.shape
    tm = min(tm, t)
    sub = d // LANES
    assert t % tm == 0 and tm % ROW_DMA_UNROLL == 0 and ys.shape[0] >= tm * sub
    big = pl.BlockSpec((tm, d), lambda i: (i, 0))
    col = pl.BlockSpec((tm, 1), lambda i: (i, 0))
    idx_spec = pl.BlockSpec((1, 1, tm), lambda i: (i, 0, 0), memory_space=pltpu.SMEM)
    return pl.pallas_call(
        functools.partial(_combine_kernel, final_norm=final_norm),
        out_shape=jax.ShapeDtypeStruct((t, d), F32),
        grid=(t // tm,),
        in_specs=[idx_spec, idx_spec, big, pl.BlockSpec(memory_space=pl.ANY), col, col,
                  pl.BlockSpec((1, d), lambda i: (0, 0))],
        out_specs=big,
        scratch_shapes=[pltpu.VMEM((tm * sub, LANES), F32), pltpu.VMEM((tm * sub, LANES), F32),
                        pltpu.SemaphoreType.DMA(())],
        compiler_params=_cparams(("arbitrary",)),
        name="moe_combine",
    )(idx1.reshape(t // tm, 1, tm), idx2.reshape(t // tm, 1, tm), x2, ys, g1, g2, gn.reshape(1, d))


def moe_residual(x2, g, w_router, wg, wu, wd, final_gain, final_norm, tm=512):
    t, d = x2.shape
    tm = min(tm, t)
    hn, meta, counts = router(x2, g, w_router)
    tr = meta.shape[2]
    field = lambda r: meta[:, r, :].reshape(t)
    idx1, idx2 = field(0).astype(I32), field(1).astype(I32)
    rank1, rank2 = field(2).astype(I32), field(3).astype(I32)
    gate1, gate2 = field(4), field(5)
    cnt = counts[:, 0].astype(I32)
    padded = ((cnt + tm - 1) // tm) * tm
    ends = jnp.cumsum(padded)
    offs = ends - padded
    pos1 = offs[idx1] + rank1
    pos2 = offs[idx2] + rank2
    n_rows = TOP_K * t + N_EXPERTS * tm
    tile_start = jnp.arange(n_rows // tm, dtype=I32) * tm
    tile_expert = jnp.minimum(jnp.sum(tile_start[:, None] >= ends[None, :], axis=1), N_EXPERTS - 1).astype(I32)
    n_tiles_used = (ends[-1:] // tm).astype(I32)

    sub = d // LANES
    xs = scatter_rows(hn, pos1, pos2, jnp.zeros((n_rows * sub, LANES), F32), sub)
    ys = expert_ffn(xs, tile_expert, n_tiles_used, wg, wu, wd, tm)
    return combine(x2, ys, pos1, pos2, gate1.reshape(t, 1), gate2.reshape(t, 1), final_gain, final_norm)


def kernel(x, ev_norm_mix, ev_w_in, ev_lambda_q1, ev_lambda_k1, ev_lambda_q2, ev_lambda_k2, ev_subln, ev_w_out,
           ev_norm_ffn, ev_ffn_gate, ev_ffn_up, ev_ffn_down, od_norm_mix, od_w_in, od_conv_w, od_conv_b,
           od_conv_ln_g, od_conv_ln_b, od_rel_bias, od_w_out, od_norm_ffn, od_router, od_exp_gate, od_exp_up,
           od_exp_down, final_norm):
    batch, seq, d = x.shape
    depth = 2 * ev_w_in.shape[0]
    assert od_w_in.shape[0] * 2 == depth
    x2 = x.reshape(batch * seq, d)
    bf = lambda a: a.astype(BF16)
    a_qk = A_HEADS * 2 * HEAD_DIM
    a_v = A_HEADS * 2 * HEAD_DIM
    b_w = B_HEADS * HEAD_DIM
    attn_k_block = 256
    for layer in range(depth):
        i = layer // 2
        if layer % 2 == 0:
            lambda_init = 0.8 - 0.6 * math.exp(-0.3 * layer)
            w = ev_w_in[i]
            va0, qb0 = 2 * a_qk, 2 * a_qk + a_v
            vb0 = qb0 + 2 * b_w
            w_qk = bf(jnp.concatenate([w[:, 0:va0], w[:, qb0:vb0]], axis=1))
            w_vt = bf(jnp.concatenate([w[:, va0:qb0], w[:, vb0:]], axis=1).T)
            proj, vt = norm_proj(x2, ev_norm_mix[i], w_qk, seq, rope_cols=2 * a_qk, wvt_bf16=w_vt,
                                 tk=attn_k_block)
            oa = diff_attention(proj, vt, ev_lambda_q1[i], ev_lambda_k1[i], ev_lambda_q2[i], ev_lambda_k2[i],
                                ev_subln[i], batch, seq, lambda_init)
            ob = stick_breaking(proj, vt, batch, seq)
            x2 = out_proj_residual(x2, oa, ob, bf(ev_w_out[i]))
            x2 = ffn_residual(x2, ev_norm_ffn[i], bf(ev_ffn_gate[i]), bf(ev_ffn_up[i]), bf(ev_ffn_down[i]))
        else:
            proj = norm_proj(x2, od_norm_mix[i], bf(od_w_in[i]), seq)
            c = conv_module(proj, od_conv_w[i], od_conv_b[i], od_conv_ln_g[i], od_conv_ln_b[i], seq)
            od = chunk_rel_attention(proj, rel_bias(od_rel_bias[i]), batch, seq)
            x2 = out_proj_residual(x2, c, od, bf(od_w_out[i]))
            x2 = moe_residual(x2, od_norm_ffn[i], od_router[i], bf(od_exp_gate[i]), bf(od_exp_up[i]),
                              bf(od_exp_down[i]), final_norm, final_norm=(layer == depth - 1))
    return x2.reshape(batch, seq, d)
```

```python
import functools
import math

import numpy as np
import jax
import jax.numpy as jnp
from jax import lax
from jax.experimental import pallas as pl
from jax.experimental.pallas import tpu as pltpu

F32 = jnp.float32
BF16 = jnp.bfloat16
I32 = jnp.int32

HEAD_DIM = 64
CHUNK = 64
ROPE_THETA = 10000.0
EPS = 1e-6
A_HEADS = 4
B_HEADS = 8
C_CHANNELS = 512
CONV_WIDTH = 31
D_HEADS = 8
LEFT_CHUNKS = 8
BAND = (LEFT_CHUNKS + 1) * CHUNK
MAX_REL = 128
N_EXPERTS = 8
TOP_K = 2

LANES = 128
NEG = -1e30
LOG2E = math.log2(math.e)
SIGN_BIT = np.int32(-2 ** 31)
VMEM_LIMIT = 56 * 1024 * 1024


def _cparams(sem):
    return pltpu.CompilerParams(dimension_semantics=sem, vmem_limit_bytes=VMEM_LIMIT)


def _nt_dot(a, b):
    return lax.dot_general(a, b, (((1,), (1,)), ((), ())), preferred_element_type=F32)


def _norm_proj_kernel(x_ref, g_ref, w_ref, cos_ref, sin_ref, *rest, col_chunk, rope_cols, tk):
    if tk:
        wvt_ref, o_ref, vt_ref = rest
    else:
        (o_ref,) = rest
    x = x_ref[...]
    xn = (x * lax.rsqrt(jnp.mean(x * x, axis=-1, keepdims=True) + EPS) * g_ref[...]).astype(BF16)
    n_out = o_ref.shape[1]
    if tk:
        for r0 in range(0, wvt_ref.shape[0], col_chunk):
            res = _nt_dot(wvt_ref[r0:r0 + col_chunk, :], xn)
            for b in range(vt_ref.shape[0]):
                vt_ref[b, r0:r0 + col_chunk, :] = res[:, b * tk:(b + 1) * tk].astype(BF16)
    if rope_cols:
        cos = cos_ref[...]
        sin = sin_ref[...]
        lane = lax.broadcasted_iota(I32, cos.shape, 1)
        first_half = (lane % HEAD_DIM) < (HEAD_DIM // 2)
    for c0 in range(0, n_out, col_chunk):
        r = jnp.dot(xn, w_ref[:, c0:c0 + col_chunk], preferred_element_type=F32)
        if c0 < rope_cols:
            parts = []
            for l0 in range(0, col_chunk, LANES):
                seg = r[:, l0:l0 + LANES]
                partner = jnp.where(first_half,
                                    pltpu.roll(seg, LANES - HEAD_DIM // 2, 1),
                                    pltpu.roll(seg, HEAD_DIM // 2, 1))
                parts.append(seg * cos + partner * sin)
            r = jnp.concatenate(parts, axis=1)
        o_ref[:, c0:c0 + col_chunk] = r.astype(o_ref.dtype)


def _rope_tables(seq):
    half = HEAD_DIM // 2
    inv_freq = ROPE_THETA ** (-jnp.arange(half, dtype=F32) * 2.0 / HEAD_DIM)
    ang = jnp.arange(seq, dtype=F32)[:, None] * inv_freq[None, :]
    cos, sin = jnp.cos(ang), jnp.sin(ang)
    cos_t = jnp.tile(jnp.concatenate([cos, cos], axis=1), (1, LANES // HEAD_DIM))
    sin_t = jnp.tile(jnp.concatenate([-sin, sin], axis=1), (1, LANES // HEAD_DIM))
    return cos_t, sin_t


def norm_proj(x2, g, w_bf16, seq, rope_cols=0, wvt_bf16=None, tk=0, tm=512):
    t, d = x2.shape
    n_out = w_bf16.shape[1]
    tm = min(tm, seq)
    col_chunk = 512
    assert t % tm == 0 and seq % tm == 0 and n_out % col_chunk == 0 and rope_cols % col_chunk == 0
    cos_t, sin_t = _rope_tables(seq)
    tiles_per_seq = seq // tm
    in_specs = [
        pl.BlockSpec((tm, d), lambda i: (i, 0)),
        pl.BlockSpec((1, d), lambda i: (0, 0)),
        pl.BlockSpec((d, n_out), lambda i: (0, 0)),
        pl.BlockSpec((tm, LANES), lambda i: (i % tiles_per_seq, 0)),
        pl.BlockSpec((tm, LANES), lambda i: (i % tiles_per_seq, 0)),
    ]
    args = [x2, g.reshape(1, d), w_bf16, cos_t, sin_t]
    out_shape = jax.ShapeDtypeStruct((t, n_out), BF16)
    out_specs = pl.BlockSpec((tm, n_out), lambda i: (i, 0))
    if wvt_bf16 is not None:
        n_v = wvt_bf16.shape[0]
        tk = min(tk, tm)
        assert tm % tk == 0 and n_v % col_chunk == 0
        in_specs.append(pl.BlockSpec((n_v, d), lambda i: (0, 0)))
        args.append(wvt_bf16)
        out_shape = (out_shape, jax.ShapeDtypeStruct((t // tk, n_v, tk), BF16))
        out_specs = (out_specs, pl.BlockSpec((tm // tk, n_v, tk), lambda i: (i, 0, 0)))
    return pl.pallas_call(
        functools.partial(_norm_proj_kernel, col_chunk=col_chunk, rope_cols=rope_cols,
                          tk=tk if wvt_bf16 is not None else 0),
        out_shape=out_shape,
        grid=(t // tm,),
        in_specs=in_specs,
        out_specs=out_specs,
        compiler_params=_cparams(("parallel",)),
        name="norm_proj",
    )(*args)


def _stack_query_pair(q_ref, q2_scr, tq):
    lane = lax.broadcasted_iota(I32, (tq, LANES), 1)
    qs = (q_ref[...].astype(F32) * (HEAD_DIM ** -0.5 * LOG2E)).astype(BF16)
    zero = jnp.zeros_like(qs)
    q2_scr[0:tq, :] = jnp.where(lane < HEAD_DIM, qs, zero)
    q2_scr[tq:, :] = jnp.where(lane >= HEAD_DIM, qs, zero)


def _diff_attn_kernel(q_ref, k_ref, vt_ref, lq1_ref, lk1_ref, lq2_ref, lk2_ref, g_ref, o_ref,
                      q2_scr, m_scr, l_scr, acc_scr, st0_scr, st1_scr, *, tq, tk, lambda_init):
    i = pl.program_id(2)
    assert tq == 2 * tk
    _stack_query_pair(q_ref, q2_scr, tq)
    m_scr[...] = jnp.full(m_scr.shape, NEG, F32)
    l_scr[...] = jnp.zeros(l_scr.shape, F32)
    acc_scr[...] = jnp.zeros(acc_scr.shape, F32)

    def scores(j, st_scr):
        k = k_ref[pl.ds(pl.multiple_of(j * tk, tk), tk), :]
        st_scr[...] = _nt_dot(k, q2_scr[...])

    def consume(j, st_scr, masked):
        st = st_scr[...]
        if masked:
            key = lax.broadcasted_iota(I32, st.shape, 0) + j * tk
            qry = lax.broadcasted_iota(I32, st.shape, 1) % tq + i * tq
            st = jnp.where(key // CHUNK <= qry // CHUNK, st, NEG)
        m_old = m_scr[...]
        m_new = jnp.maximum(m_old, jnp.max(st, axis=0, keepdims=True))
        alpha = jnp.exp2(m_old - m_new)
        pt = jnp.exp2(st - m_new)
        l_scr[...] = alpha * l_scr[...] + jnp.sum(pt, axis=0, keepdims=True)
        acc_scr[...] = alpha * acc_scr[...] + jnp.dot(vt_ref[j], pt.astype(BF16), preferred_element_type=F32)
        m_scr[...] = m_new

    scores(0, st0_scr)

    def body(n, carry):
        scores(2 * n + 1, st1_scr)
        consume(2 * n, st0_scr, False)
        scores(2 * n + 2, st0_scr)
        consume(2 * n + 1, st1_scr, False)
        return carry

    lax.fori_loop(0, i, body, 0)
    scores(2 * i + 1, st1_scr)
    consume(2 * i, st0_scr, True)
    consume(2 * i + 1, st1_scr, True)

    ot = acc_scr[...] / l_scr[...]
    lam = (jnp.exp(jnp.sum(lq1_ref[...] * lk1_ref[...], axis=1, keepdims=True))
           - jnp.exp(jnp.sum(lq2_ref[...] * lk2_ref[...], axis=1, keepdims=True)) + lambda_init)
    od = ot[:, 0:tq].T - lam * ot[:, tq:].T
    y = od * lax.rsqrt(jnp.mean(od * od, axis=-1, keepdims=True) + EPS) * g_ref[...]
    o_ref[...] = (y * (1.0 - lambda_init)).astype(o_ref.dtype)


def diff_attention(proj, vt, lq1, lk1, lq2, lk2, subln_g, batch, seq, lambda_init):
    t = proj.shape[0]
    tk = vt.shape[2]
    tq = 2 * tk
    nq = seq // tq
    assert seq % tq == 0 and tk % CHUNK == 0
    k_off = (A_HEADS * 2 * HEAD_DIM) // LANES
    vec = lambda a: a.reshape(1, -1).astype(F32)
    small = lambda n: pl.BlockSpec((1, n), lambda b, h, i: (0, 0))
    return pl.pallas_call(
        functools.partial(_diff_attn_kernel, tq=tq, tk=tk, lambda_init=lambda_init),
        out_shape=jax.ShapeDtypeStruct((t, A_HEADS * 2 * HEAD_DIM), BF16),
        grid=(batch, A_HEADS, nq),
        in_specs=[
            pl.BlockSpec((tq, LANES), lambda b, h, i: (b * nq + i, h)),
            pl.BlockSpec((seq, LANES), lambda b, h, i: (b, k_off + h)),
            pl.BlockSpec((seq // tk, LANES, tk), lambda b, h, i: (b, h, 0)),
            small(HEAD_DIM), small(HEAD_DIM), small(HEAD_DIM), small(HEAD_DIM), small(2 * HEAD_DIM),
        ],
        out_specs=pl.BlockSpec((tq, LANES), lambda b, h, i: (b * nq + i, h)),
        scratch_shapes=[
            pltpu.VMEM((2 * tq, LANES), BF16),
            pltpu.VMEM((1, 2 * tq), F32),
            pltpu.VMEM((1, 2 * tq), F32),
            pltpu.VMEM((LANES, 2 * tq), F32),
            pltpu.VMEM((tk, 2 * tq), F32),
            pltpu.VMEM((tk, 2 * tq), F32),
        ],
        compiler_params=_cparams(("parallel", "parallel", "arbitrary")),
        name="diff_attn",
    )(proj, proj, vt, vec(lq1), vec(lk1), vec(lq2), vec(lk2), vec(subln_g))


def _stick_kernel(q_ref, k_ref, vt_ref, o_ref, q2_scr, tri_scr, c_scr, acc_scr, st0_scr, st1_scr, *, tq, tk):
    i = pl.program_id(2)
    assert tq == 2 * tk
    _stack_query_pair(q_ref, q2_scr, tq)
    r_i = lax.broadcasted_iota(I32, (tk, tk), 0)
    c_i = lax.broadcasted_iota(I32, (tk, tk), 1)
    tri_scr[...] = jnp.where(c_i > r_i, 1.0, 0.0).astype(BF16)
    c_scr[...] = jnp.zeros(c_scr.shape, F32)
    acc_scr[...] = jnp.zeros(acc_scr.shape, F32)

    def scores(j, st_scr):
        k = k_ref[pl.ds(pl.multiple_of(j * tk, tk), tk), :]
        st_scr[...] = _nt_dot(k, q2_scr[...])

    def consume(j, st_scr, masked):
        zt = st_scr[...]
        neg_abs = lax.bitcast_convert_type(lax.bitcast_convert_type(zt, I32) | SIGN_BIT, F32)
        sp = jnp.maximum(zt, 0.0) + jnp.log2(1.0 + jnp.exp2(neg_abs))
        if masked:
            key = lax.broadcasted_iota(I32, zt.shape, 0) + j * tk
            qry = lax.broadcasted_iota(I32, zt.shape, 1) % tq + i * tq
            valid = key < qry
            spm = jnp.where(valid, sp, 0.0)
        else:
            spm = sp
        after = jnp.dot(tri_scr[...], spm.astype(BF16), preferred_element_type=F32)
        w = jnp.exp2(zt - sp - after - c_scr[...])
        if masked:
            w = jnp.where(valid, w, 0.0)
        acc_scr[...] += jnp.dot(vt_ref[j], w.astype(BF16), preferred_element_type=F32)
        c_scr[...] += jnp.sum(spm, axis=0, keepdims=True)

    scores(2 * i + 1, st0_scr)
    scores(2 * i, st1_scr)
    consume(2 * i + 1, st0_scr, True)
    scores(jnp.maximum(2 * i - 1, 0), st0_scr)
    consume(2 * i, st1_scr, True)

    def body(n, carry):
        j = 2 * (i - n) - 1
        scores(j - 1, st1_scr)
        consume(j, st0_scr, False)
        scores(jnp.maximum(j - 2, 0), st0_scr)
        consume(j - 1, st1_scr, False)
        return carry

    lax.fori_loop(0, i, body, 0)
    acc = acc_scr[...]
    vrow = lax.broadcasted_iota(I32, (LANES, tq), 0)
    ot = jnp.where(vrow < HEAD_DIM, acc[:, 0:tq], acc[:, tq:])
    o_ref[...] = ot.T.astype(o_ref.dtype)


def stick_breaking(proj, vt, batch, seq):
    t = proj.shape[0]
    tk = vt.shape[2]
    tq = 2 * tk
    nq = seq // tq
    assert seq % tq == 0
    pairs = (B_HEADS * HEAD_DIM) // LANES
    a_blocks = (A_HEADS * 2 * HEAD_DIM) // LANES
    q_off = 2 * a_blocks
    k_off = q_off + pairs
    v_off = a_blocks
    return pl.pallas_call(
        functools.partial(_stick_kernel, tq=tq, tk=tk),
        out_shape=jax.ShapeDtypeStruct((t, B_HEADS * HEAD_DIM), BF16),
        grid=(batch, pairs, nq),
        in_specs=[
            pl.BlockSpec((tq, LANES), lambda b, p, i: (b * nq + i, q_off + p)),
            pl.BlockSpec((seq, LANES), lambda b, p, i: (b, k_off + p)),
            pl.BlockSpec((seq // tk, LANES, tk), lambda b, p, i: (b, v_off + p, 0)),
        ],
        out_specs=pl.BlockSpec((tq, LANES), lambda b, p, i: (b * nq + i, p)),
        scratch_shapes=[
            pltpu.VMEM((2 * tq, LANES), BF16),
            pltpu.VMEM((tk, tk), BF16),
            pltpu.VMEM((1, 2 * tq), F32),
            pltpu.VMEM((LANES, 2 * tq), F32),
            pltpu.VMEM((tk, 2 * tq), F32),
            pltpu.VMEM((tk, 2 * tq), F32),
        ],
        compiler_params=_cparams(("parallel", "parallel", "arbitrary")),
        name="stick_attn",
    )(proj, proj, vt)


def _out_proj_kernel(x_ref, a_ref, b_ref, w_ref, o_ref):
    half = a_ref.shape[1]
    o_ref[...] = (x_ref[...]
                  + jnp.dot(a_ref[...], w_ref[0:half, :], preferred_element_type=F32)
                  + jnp.dot(b_ref[...], w_ref[half:, :], preferred_element_type=F32))


def out_proj_residual(x2, a, b, w_bf16, tm=512):
    t, d = x2.shape
    half = a.shape[1]
    tm = min(tm, t)
    return pl.pallas_call(
        _out_proj_kernel,
        out_shape=jax.ShapeDtypeStruct((t, d), F32),
        grid=(t // tm,),
        in_specs=[
            pl.BlockSpec((tm, d), lambda i: (i, 0)),
            pl.BlockSpec((tm, half), lambda i: (i, 0)),
            pl.BlockSpec((tm, half), lambda i: (i, 0)),
            pl.BlockSpec((2 * half, d), lambda i: (0, 0)),
        ],
        out_specs=pl.BlockSpec((tm, d), lambda i: (i, 0)),
        compiler_params=_cparams(("parallel",)),
        name="out_proj",
    )(x2, a, b, w_bf16)


def _swiglu_chunks(xn, wg_ref, wu_ref, h_scr, ff_chunk):
    d_ff = h_scr.shape[1]
    for c0 in range(0, d_ff, ff_chunk):
        g = jnp.dot(xn, wg_ref[:, c0:c0 + ff_chunk], preferred_element_type=F32)
        u = jnp.dot(xn, wu_ref[:, c0:c0 + ff_chunk], preferred_element_type=F32)
        h_scr[:, c0:c0 + ff_chunk] = (g * jax.nn.sigmoid(g) * u).astype(BF16)


def _ffn_kernel(x_ref, g_ref, wg_ref, wu_ref, wd_ref, o_ref, h_scr, *, ff_chunk):
    x = x_ref[...]
    xn = (x * lax.rsqrt(jnp.mean(x * x, axis=-1, keepdims=True) + EPS) * g_ref[...]).astype(BF16)
    _swiglu_chunks(xn, wg_ref, wu_ref, h_scr, ff_chunk)
    o_ref[...] = x + jnp.dot(h_scr[...], wd_ref[...], preferred_element_type=F32)


def ffn_residual(x2, g, wg, wu, wd, tm=512):
    t, d = x2.shape
    d_ff = wg.shape[1]
    tm = min(tm, t)
    ff_chunk = 256
    assert d_ff % ff_chunk == 0
    resident = lambda shape: pl.BlockSpec(shape, lambda i: (0, 0), pipeline_mode=pl.Buffered(1))
    return pl.pallas_call(
        functools.partial(_ffn_kernel, ff_chunk=ff_chunk),
        out_shape=jax.ShapeDtypeStruct((t, d), F32),
        grid=(t // tm,),
        in_specs=[
            pl.BlockSpec((tm, d), lambda i: (i, 0)),
            pl.BlockSpec((1, d), lambda i: (0, 0)),
            resident((d, d_ff)), resident((d, d_ff)), resident((d_ff, d)),
        ],
        out_specs=pl.BlockSpec((tm, d), lambda i: (i, 0)),
        scratch_shapes=[pltpu.VMEM((tm, d_ff), BF16)],
        compiler_params=_cparams(("parallel",)),
        name="ffn",
    )(x2, g.reshape(1, d), wg, wu, wd)


CONV_HALO = 32


def _conv_kernel(val_ref, gate_ref, pval_ref, pgate_ref, w_ref, b_ref, lg_ref, lb_ref, o_ref,
                 u_scr, c_scr, *, ts, tiles_per_seq, row_blk):
    i = pl.program_id(0)
    glu = lambda v, g: v.astype(F32) * jax.nn.sigmoid(g.astype(F32))
    keep = jnp.where(i % tiles_per_seq == 0, 0.0, 1.0)
    u_scr[0:CONV_HALO, :] = glu(pval_ref[...], pgate_ref[...]) * keep
    u_scr[CONV_HALO:, :] = glu(val_ref[...], gate_ref[...])
    shift = CONV_HALO - (CONV_WIDTH - 1)
    n_ch = val_ref.shape[1]
    for c0 in range(0, n_ch, LANES):
        for r0 in range(0, ts, row_blk):
            acc = jnp.broadcast_to(b_ref[:, c0:c0 + LANES], (row_blk, LANES))
            for j in range(CONV_WIDTH):
                acc = acc + w_ref[j:j + 1, c0:c0 + LANES] * u_scr[r0 + shift + j:r0 + shift + j + row_blk,
                                                                   c0:c0 + LANES]
            c_scr[r0:r0 + row_blk, c0:c0 + LANES] = acc
    c = c_scr[...]
    mu = jnp.mean(c, axis=-1, keepdims=True)
    xc = c - mu
    var = jnp.mean(xc * xc, axis=-1, keepdims=True)
    y = xc * lax.rsqrt(var + EPS) * lg_ref[...] + lb_ref[...]
    o_ref[...] = (y * jax.nn.sigmoid(y)).astype(o_ref.dtype)


def conv_module(proj, conv_w, conv_b, ln_g, ln_b, seq, ts=256):
    t = proj.shape[0]
    ts = min(ts, seq)
    n_ch = C_CHANNELS
    halo_blocks = ts // CONV_HALO
    row = lambda a: a.reshape(1, n_ch).astype(F32)
    small = pl.BlockSpec((1, n_ch), lambda i: (0, 0))
    prev = lambda col: pl.BlockSpec((CONV_HALO, n_ch), lambda i: (jnp.maximum(i * halo_blocks - 1, 0), col))
    return pl.pallas_call(
        functools.partial(_conv_kernel, ts=ts, tiles_per_seq=seq // ts, row_blk=64),
        out_shape=jax.ShapeDtypeStruct((t, n_ch), BF16),
        grid=(t // ts,),
        in_specs=[
            pl.BlockSpec((ts, n_ch), lambda i: (i, 0)),
            pl.BlockSpec((ts, n_ch), lambda i: (i, 1)),
            prev(0), prev(1),
            pl.BlockSpec((CONV_WIDTH, n_ch), lambda i: (0, 0)),
            small, small, small,
        ],
        out_specs=pl.BlockSpec((ts, n_ch), lambda i: (i, 0)),
        scratch_shapes=[pltpu.VMEM((ts + CONV_HALO, n_ch), F32), pltpu.VMEM((ts, n_ch), F32)],
        compiler_params=_cparams(("parallel",)),
        name="conv_module",
    )(proj, proj, proj, proj, conv_w.astype(F32), row(conv_b), row(ln_g), row(ln_b))


def _rel_bias_kernel(tbl_ref, o_ref):
    p = pl.program_id(0)
    kk = lax.broadcasted_iota(I32, (BAND, LANES), 0)
    lane = lax.broadcasted_iota(I32, (BAND, LANES), 1)
    idx = jnp.clip(lane % CHUNK - (kk - LEFT_CHUNKS * CHUNK), -MAX_REL, MAX_REL) + MAX_REL
    first = lane < CHUNK

    def body(j, acc):
        return acc + jnp.where(idx == j, jnp.where(first, tbl_ref[2 * p, j], tbl_ref[2 * p + 1, j]), 0.0)

    o_ref[0] = lax.fori_loop(0, 2 * MAX_REL + 1, body, jnp.zeros((BAND, LANES), F32)) * LOG2E


def rel_bias(rel_table):
    pairs = D_HEADS // 2
    assert 2 * CHUNK == LANES
    return pl.pallas_call(
        _rel_bias_kernel,
        out_shape=jax.ShapeDtypeStruct((pairs, BAND, LANES), F32),
        grid=(pairs,),
        in_specs=[pl.BlockSpec(memory_space=pltpu.SMEM)],
        out_specs=pl.BlockSpec((1, BAND, LANES), lambda p: (p, 0, 0)),
        compiler_params=_cparams(("parallel",)),
        name="rel_bias",
    )(rel_table.astype(F32))


def _chunk_attn_kernel(q_ref, k_ref, v_ref, bias_ref, o_ref, kpad, vpad, *, group, seq):
    i = pl.program_id(2)
    pad = LEFT_CHUNKS * CHUNK

    @pl.when(i == 0)
    def _():
        kpad[0:pad, :] = jnp.zeros((pad, LANES), BF16)
        vpad[0:pad, :] = jnp.zeros((pad, LANES), BF16)
        kpad[pad:, :] = k_ref[...]
        vpad[pad:, :] = v_ref[...]

    lane = lax.broadcasted_iota(I32, (CHUNK, LANES), 1)
    band_chunk = lax.broadcasted_iota(I32, (BAND, LANES), 0) // CHUNK
    bias = bias_ref[0]
    ones = jnp.ones((BAND, LANES), BF16)
    for g in range(group):
        c = i * group + g
        qs = (q_ref[g * CHUNK:(g + 1) * CHUNK, :].astype(F32) * (HEAD_DIM ** -0.5 * LOG2E)).astype(BF16)
        zero = jnp.zeros_like(qs)
        q2 = jnp.concatenate([jnp.where(lane < HEAD_DIM, qs, zero),
                              jnp.where(lane >= HEAD_DIM, qs, zero)], axis=0)
        start = pl.multiple_of(c * CHUNK, CHUNK)
        kw = kpad[pl.ds(start, BAND), :]
        vw = vpad[pl.ds(start, BAND), :]
        st = _nt_dot(kw, q2) + bias
        st = jnp.where(band_chunk >= LEFT_CHUNKS - c, st, NEG)
        pt = jnp.exp2(st - jnp.max(st, axis=0, keepdims=True))
        o2 = lax.dot_general(pt.astype(BF16), jnp.concatenate([vw, ones], axis=1),
                             (((0,), (0,)), ((), ())), preferred_element_type=F32)
        o = o2[:, 0:LANES] / o2[:, LANES:]
        o_ref[g * CHUNK:(g + 1) * CHUNK, :] = jnp.where(lane < HEAD_DIM, o[0:CHUNK, :],
                                                        o[CHUNK:, :]).astype(o_ref.dtype)


def chunk_rel_attention(proj, bias, batch, seq, group=16):
    t = proj.shape[0]
    pairs = D_HEADS // 2
    q_off = (2 * C_CHANNELS) // LANES
    k_off = q_off + pairs
    v_off = k_off + pairs
    tq = group * CHUNK
    nq = seq // tq
    return pl.pallas_call(
        functools.partial(_chunk_attn_kernel, group=group, seq=seq),
        out_shape=jax.ShapeDtypeStruct((t, D_HEADS * HEAD_DIM), BF16),
        grid=(batch, pairs, nq),
        in_specs=[
            pl.BlockSpec((tq, LANES), lambda b, p, i: (b * nq + i, q_off + p)),
            pl.BlockSpec((seq, LANES), lambda b, p, i: (b, k_off + p)),
            pl.BlockSpec((seq, LANES), lambda b, p, i: (b, v_off + p)),
            pl.BlockSpec((1, BAND, LANES), lambda b, p, i: (p, 0, 0)),
        ],
        out_specs=pl.BlockSpec((tq, LANES), lambda b, p, i: (b * nq + i, p)),
        scratch_shapes=[pltpu.VMEM((seq + LEFT_CHUNKS * CHUNK, LANES), BF16),
                        pltpu.VMEM((seq + LEFT_CHUNKS * CHUNK, LANES), BF16)],
        compiler_params=_cparams(("parallel", "parallel", "arbitrary")),
        name="chunk_attn",
    )(proj, proj, proj, bias)


def _store_row_tiles(ref, val):
    n, d = val.shape
    sub = d // LANES
    for k in range(sub):
        ref[pl.ds(k, n, stride=sub), :] = val[:, k * LANES:(k + 1) * LANES]


def _load_row_tiles(ref, n, d):
    sub = d // LANES
    return jnp.concatenate([ref[pl.ds(k, n, stride=sub), :] for k in range(sub)], axis=1)


def _router_kernel(x_ref, g_ref, wr_ref, h_ref, meta_ref, cnt_ref, carry_scr, tri_scr, *, tr):
    i = pl.program_id(0)

    @pl.when(i == 0)
    def _():
        carry_scr[...] = jnp.zeros(carry_scr.shape, F32)
        r_i = lax.broadcasted_iota(I32, (tr, tr), 0)
        c_i = lax.broadcasted_iota(I32, (tr, tr), 1)
        tri_scr[...] = jnp.where(r_i < c_i, 1.0, 0.0).astype(BF16)

    x = x_ref[...]
    hn = x * lax.rsqrt(jnp.mean(x * x, axis=-1, keepdims=True) + EPS) * g_ref[...]
    _store_row_tiles(h_ref, hn)
    h_hi = hn.astype(BF16)
    h_lo = (hn - h_hi.astype(F32)).astype(BF16)
    w = wr_ref[...]
    w_hi = w.astype(BF16)
    w_lo = (w - w_hi.astype(F32)).astype(BF16)
    logits = _nt_dot(w_hi, h_hi) + (_nt_dot(w_hi, h_lo) + _nt_dot(w_lo, h_hi))
    eidx = lax.broadcasted_iota(I32, logits.shape, 0).astype(F32)
    m1 = jnp.max(logits, axis=0, keepdims=True)
    i1 = jnp.min(jnp.where(logits == m1, eidx, float(N_EXPERTS)), axis=0, keepdims=True)
    rest = jnp.where(eidx == i1, -jnp.inf, logits)
    m2 = jnp.max(rest, axis=0, keepdims=True)
    i2 = jnp.min(jnp.where(rest == m2, eidx, float(N_EXPERTS)), axis=0, keepdims=True)
    e = jnp.exp(m2 - m1)
    g1 = 1.0 / (1.0 + e)
    g2 = e / (1.0 + e)
    sel1 = eidx == i1
    sel2 = eidx == i2
    onehot = jnp.where(sel1 | sel2, 1.0, 0.0)
    rank = jnp.dot(onehot.astype(BF16), tri_scr[...], preferred_element_type=F32) + carry_scr[...]
    r1 = jnp.sum(jnp.where(sel1, rank, 0.0), axis=0, keepdims=True)
    r2 = jnp.sum(jnp.where(sel2, rank, 0.0), axis=0, keepdims=True)
    carry = carry_scr[...] + jnp.sum(onehot, axis=1, keepdims=True)
    carry_scr[...] = carry
    zero = jnp.zeros_like(g1)
    meta_ref[0] = jnp.concatenate([i1, i2, r1, r2, g1, g2, zero, zero], axis=0)
    cnt_ref[...] = jnp.broadcast_to(carry, cnt_ref.shape)


def router(x2, g, w_router, tr=512):
    t, d = x2.shape
    tr = min(tr, t)
    sub = d // LANES
    return pl.pallas_call(
        functools.partial(_router_kernel, tr=tr),
        out_shape=(jax.ShapeDtypeStruct((t * sub, LANES), F32),
                   jax.ShapeDtypeStruct((t // tr, 8, tr), F32),
                   jax.ShapeDtypeStruct((N_EXPERTS, LANES), F32)),
        grid=(t // tr,),
        in_specs=[
            pl.BlockSpec((tr, d), lambda i: (i, 0)),
            pl.BlockSpec((1, d), lambda i: (0, 0)),
            pl.BlockSpec((N_EXPERTS, d), lambda i: (0, 0)),
        ],
        out_specs=(pl.BlockSpec((tr * sub, LANES), lambda i: (i, 0)),
                   pl.BlockSpec((1, 8, tr), lambda i: (i, 0, 0)),
                   pl.BlockSpec((N_EXPERTS, LANES), lambda i: (0, 0))),
        scratch_shapes=[pltpu.VMEM((N_EXPERTS, 1), F32), pltpu.VMEM((tr, tr), BF16)],
        compiler_params=_cparams(("arbitrary",)),
        name="router",
    )(x2, g.reshape(1, d), w_router.T.astype(F32))


ROW_DMA_UNROLL = 8


def _rows(ref, row, sub):
    return ref.at[pl.ds(pl.multiple_of(row * sub, sub), sub)]


def _scatter_rows_kernel(ends_ref, idx1_ref, idx2_ref, src_ref, dst_ref, zero_scr, sem, *, rows, sub, pad_rows):
    @pl.when(pl.program_id(0) == 0)
    def _():
        zero_scr[...] = jnp.zeros(zero_scr.shape, zero_scr.dtype)

        def tail(e):
            start = jnp.maximum(ends_ref[e] - pad_rows, 0)
            return pltpu.make_async_copy(zero_scr, dst_ref.at[pl.ds(pl.multiple_of(start * sub, sub),
                                                                  pad_rows * sub)], sem)

        n_exp = ends_ref.shape[0]
        for e in range(n_exp):
            tail(e).start()
        for e in range(n_exp):
            tail(e).wait()
        n_rows = dst_ref.shape[0] // sub
        for t in range(n_exp):
            start = ends_ref[n_exp - 1] + t * pad_rows

            @pl.when(start < n_rows)
            def _():
                spare = pltpu.make_async_copy(
                    zero_scr, dst_ref.at[pl.ds(pl.multiple_of(start * sub, sub), pad_rows * sub)], sem)
                spare.start()
                spare.wait()

    def issue(r, carry):
        pltpu.make_async_copy(_rows(src_ref, r, sub), _rows(dst_ref, idx1_ref[0, 0, r], sub), sem).start()
        pltpu.make_async_copy(_rows(src_ref, r, sub), _rows(dst_ref, idx2_ref[0, 0, r], sub), sem).start()
        return carry

    lax.fori_loop(0, rows, issue, 0, unroll=ROW_DMA_UNROLL)
    for _ in range(2):
        pltpu.make_async_copy(src_ref, dst_ref.at[pl.ds(0, rows * sub)], sem).wait()


def scatter_rows(src, idx1, idx2, ends, n_rows, pad_rows, sub, rows=512):
    n = idx1.shape[0]
    rows = min(rows, n)
    assert n % rows == 0 and rows % ROW_DMA_UNROLL == 0 and n_rows >= max(rows, pad_rows)
    idx_spec = pl.BlockSpec((1, 1, rows), lambda i, ends: (i, 0, 0), memory_space=pltpu.SMEM)
    grid_spec = pltpu.PrefetchScalarGridSpec(
        num_scalar_prefetch=1,
        grid=(n // rows,),
        in_specs=[idx_spec, idx_spec, pl.BlockSpec((rows * sub, LANES), lambda i, ends: (i, 0))],
        out_specs=pl.BlockSpec(memory_space=pl.ANY),
        scratch_shapes=[pltpu.VMEM((pad_rows * sub, LANES), src.dtype), pltpu.SemaphoreType.DMA(())],
    )
    return pl.pallas_call(
        functools.partial(_scatter_rows_kernel, rows=rows, sub=sub, pad_rows=pad_rows),
        out_shape=jax.ShapeDtypeStruct((n_rows * sub, LANES), src.dtype),
        grid_spec=grid_spec,
        compiler_params=_cparams(("arbitrary",)),
        name="row_scatter",
    )(ends, idx1.reshape(n // rows, 1, rows), idx2.reshape(n // rows, 1, rows), src)


def _expert_ffn_kernel(te_ref, nt_ref, x_ref, wg_ref, wu_ref, wd_ref, o_ref, xn_scr, h_scr, acc_scr, *, ff_chunk):
    n = pl.program_id(0)
    f = pl.program_id(1)
    tm, d = acc_scr.shape

    @pl.when(n < nt_ref[0])
    def _():
        @pl.when(f == 0)
        def _():
            xn_scr[...] = _load_row_tiles(x_ref, tm, d).astype(BF16)

        _swiglu_chunks(xn_scr[...], wg_ref.at[0], wu_ref.at[0], h_scr, ff_chunk)
        part = jnp.dot(h_scr[...], wd_ref[0], preferred_element_type=F32)

        @pl.when(f == 0)
        def _():
            acc_scr[...] = part

        @pl.when(f > 0)
        def _():
            acc_scr[...] += part

        @pl.when(f == pl.num_programs(1) - 1)
        def _():
            _store_row_tiles(o_ref, acc_scr[...])

    @pl.when((n >= nt_ref[0]) & (f == pl.num_programs(1) - 1))
    def _():
        o_ref[...] = jnp.zeros(o_ref.shape, o_ref.dtype)


def expert_ffn(xs, tile_expert, n_tiles_used, wg, wu, wd, tm, ff_split=2):
    d, d_ff = wg.shape[1], wg.shape[2]
    sub = d // LANES
    nr = xs.shape[0] // sub
    ff_blk = d_ff // ff_split
    ff_chunk = 256
    assert nr % tm == 0 and ff_blk % ff_chunk == 0
    grid_spec = pltpu.PrefetchScalarGridSpec(
        num_scalar_prefetch=2,
        grid=(nr // tm, ff_split),
        in_specs=[
            pl.BlockSpec((tm * sub, LANES), lambda n, f, te, nt: (jnp.minimum(n, nt[0] - 1), 0)),
            pl.BlockSpec((1, d, ff_blk), lambda n, f, te, nt: (te[n], 0, f)),
            pl.BlockSpec((1, d, ff_blk), lambda n, f, te, nt: (te[n], 0, f)),
            pl.BlockSpec((1, ff_blk, d), lambda n, f, te, nt: (te[n], f, 0)),
        ],
        out_specs=pl.BlockSpec((tm * sub, LANES), lambda n, f, te, nt: (n, 0)),
        scratch_shapes=[pltpu.VMEM((tm, d), BF16), pltpu.VMEM((tm, ff_blk), BF16), pltpu.VMEM((tm, d), F32)],
    )
    return pl.pallas_call(
        functools.partial(_expert_ffn_kernel, ff_chunk=ff_chunk),
        out_shape=jax.ShapeDtypeStruct((nr * sub, LANES), F32),
        grid_spec=grid_spec,
        compiler_params=_cparams(("arbitrary", "arbitrary")),
        name="expert_ffn",
    )(tile_expert, n_tiles_used, xs, wg, wu, wd)


def _combine_kernel(idx1_ref, idx2_ref, x_ref, ys_ref, g1_ref, g2_ref, gn_ref, o_ref, y1_buf, y2_buf, sem, *,
                    final_norm):
    tm, d = x_ref.shape
    sub = d // LANES

    def issue(r, carry):
        pltpu.make_async_copy(_rows(ys_ref, idx1_ref[0, 0, r], sub), _rows(y1_buf, r, sub), sem).start()
        pltpu.make_async_copy(_rows(ys_ref, idx2_ref[0, 0, r], sub), _rows(y2_buf, r, sub), sem).start()
        return carry

    lax.fori_loop(0, tm, issue, 0, unroll=ROW_DMA_UNROLL)
    for buf in (y1_buf, y2_buf):
        pltpu.make_async_copy(ys_ref.at[pl.ds(0, tm * sub)], buf, sem).wait()
    y = (x_ref[...] + g1_ref[...] * _load_row_tiles(y1_buf, tm, d)
         + g2_ref[...] * _load_row_tiles(y2_buf, tm, d))
    if final_norm:
        y = y * lax.rsqrt(jnp.mean(y * y, axis=-1, keepdims=True) + EPS) * gn_ref[...]
    o_ref[...] = y


def combine(x2, ys, idx1, idx2, g1, g2, gn, final_norm, tm=512):
    t, d = x2.shape
    tm = min(tm, t)
    sub = d // LANES
    assert t % tm == 0 and tm % ROW_DMA_UNROLL == 0 and ys.shape[0] >= tm * sub
    big = pl.BlockSpec((tm, d), lambda i: (i, 0))
    col = pl.BlockSpec((tm, 1), lambda i: (i, 0))
    idx_spec = pl.BlockSpec((1, 1, tm), lambda i: (i, 0, 0), memory_space=pltpu.SMEM)
    return pl.pallas_call(
        functools.partial(_combine_kernel, final_norm=final_norm),
        out_shape=jax.ShapeDtypeStruct((t, d), F32),
        grid=(t // tm,),
        in_specs=[idx_spec, idx_spec, big, pl.BlockSpec(memory_space=pl.ANY), col, col,
                  pl.BlockSpec((1, d), lambda i: (0, 0))],
        out_specs=big,
        scratch_shapes=[pltpu.VMEM((tm * sub, LANES), F32), pltpu.VMEM((tm * sub, LANES), F32),
                        pltpu.SemaphoreType.DMA(())],
        compiler_params=_cparams(("arbitrary",)),
        name="moe_combine",
    )(idx1.reshape(t // tm, 1, tm), idx2.reshape(t // tm, 1, tm), x2, ys, g1, g2, gn.reshape(1, d))


def moe_residual(x2, g, w_router, wg, wu, wd, final_gain, final_norm, tm=512):
    t, d = x2.shape
    tm = min(tm, t)
    hn, meta, counts = router(x2, g, w_router)
    tr = meta.shape[2]
    field = lambda r: meta[:, r, :].reshape(t)
    idx1, idx2 = field(0).astype(I32), field(1).astype(I32)
    rank1, rank2 = field(2).astype(I32), field(3).astype(I32)
    gate1, gate2 = field(4), field(5)
    cnt = counts[:, 0].astype(I32)
    padded = ((cnt + tm - 1) // tm) * tm
    ends = jnp.cumsum(padded)
    offs = ends - padded
    pos1 = offs[idx1] + rank1
    pos2 = offs[idx2] + rank2
    n_rows = TOP_K * t + N_EXPERTS * tm
    tile_start = jnp.arange(n_rows // tm, dtype=I32) * tm
    tile_expert = jnp.minimum(jnp.sum(tile_start[:, None] >= ends[None, :], axis=1), N_EXPERTS - 1).astype(I32)
    n_tiles_used = (ends[-1:] // tm).astype(I32)

    sub = d // LANES
    xs = scatter_rows(hn, pos1, pos2, ends.astype(I32), n_rows, tm, sub)
    ys = expert_ffn(xs, tile_expert, n_tiles_used, wg, wu, wd, tm)
    return combine(x2, ys, pos1, pos2, gate1.reshape(t, 1), gate2.reshape(t, 1), final_gain, final_norm)


def kernel(x, ev_norm_mix, ev_w_in, ev_lambda_q1, ev_lambda_k1, ev_lambda_q2, ev_lambda_k2, ev_subln, ev_w_out,
           ev_norm_ffn, ev_ffn_gate, ev_ffn_up, ev_ffn_down, od_norm_mix, od_w_in, od_conv_w, od_conv_b,
           od_conv_ln_g, od_conv_ln_b, od_rel_bias, od_w_out, od_norm_ffn, od_router, od_exp_gate, od_exp_up,
           od_exp_down, final_norm):
    batch, seq, d = x.shape
    depth = 2 * ev_w_in.shape[0]
    assert od_w_in.shape[0] * 2 == depth
    x2 = x.reshape(batch * seq, d)
    bf = lambda a: a.astype(BF16)
    a_qk = A_HEADS * 2 * HEAD_DIM
    a_v = A_HEADS * 2 * HEAD_DIM
    b_w = B_HEADS * HEAD_DIM
    attn_k_block = 256
    for layer in range(depth):
        i = layer // 2
        if layer % 2 == 0:
            lambda_init = 0.8 - 0.6 * math.exp(-0.3 * layer)
            w = ev_w_in[i]
            va0, qb0 = 2 * a_qk, 2 * a_qk + a_v
            vb0 = qb0 + 2 * b_w
            w_qk = bf(jnp.concatenate([w[:, 0:va0], w[:, qb0:vb0]], axis=1))
            w_vt = bf(jnp.concatenate([w[:, va0:qb0], w[:, vb0:]], axis=1).T)
            proj, vt = norm_proj(x2, ev_norm_mix[i], w_qk, seq, rope_cols=2 * a_qk, wvt_bf16=w_vt,
                                 tk=attn_k_block)
            oa = diff_attention(proj, vt, ev_lambda_q1[i], ev_lambda_k1[i], ev_lambda_q2[i], ev_lambda_k2[i],
                                ev_subln[i], batch, seq, lambda_init)
            ob = stick_breaking(proj, vt, batch, seq)
            x2 = out_proj_residual(x2, oa, ob, bf(ev_w_out[i]))
            x2 = ffn_residual(x2, ev_norm_ffn[i], bf(ev_ffn_gate[i]), bf(ev_ffn_up[i]), bf(ev_ffn_down[i]))
        else:
            proj = norm_proj(x2, od_norm_mix[i], bf(od_w_in[i]), seq)
            c = conv_module(proj, od_conv_w[i], od_conv_b[i], od_conv_ln_g[i], od_conv_ln_b[i], seq)
            od = chunk_rel_attention(proj, rel_bias(od_rel_bias[i]), batch, seq)
            x2 = out_proj_residual(x2, c, od, bf(od_w_out[i]))
            x2 = moe_residual(x2, od_norm_ffn[i], od_router[i], bf(od_exp_gate[i]), bf(od_exp_up[i]),
                              bf(od_exp_down[i]), final_norm, final_norm=(layer == depth - 1))
    return x2.reshape(batch, seq, d)
```

```python
import functools
import math

import numpy as np
import jax
import jax.numpy as jnp
from jax import lax
from jax.experimental import pallas as pl
from jax.experimental.pallas import tpu as pltpu

F32 = jnp.float32
BF16 = jnp.bfloat16
I32 = jnp.int32

HEAD_DIM = 64
CHUNK = 64
ROPE_THETA = 10000.0
EPS = 1e-6
A_HEADS = 4
B_HEADS = 8
C_CHANNELS = 512
CONV_WIDTH = 31
D_HEADS = 8
LEFT_CHUNKS = 8
BAND = (LEFT_CHUNKS + 1) * CHUNK
MAX_REL = 128
N_EXPERTS = 8
TOP_K = 2

LANES = 128
SUBLANES = 8
NEG = -1e30
LOG2E = math.log2(math.e)
SIGN_BIT = np.int32(-2 ** 31)
VMEM_LIMIT = 56 * 1024 * 1024


def _cparams(sem):
    return pltpu.CompilerParams(dimension_semantics=sem, vmem_limit_bytes=VMEM_LIMIT)


def _nt_dot(a, b):
    return lax.dot_general(a, b, (((1,), (1,)), ((), ())), preferred_element_type=F32)


def _norm_proj_kernel(x_ref, g_ref, w_ref, cos_ref, sin_ref, *rest, col_chunk, rope_cols, tk):
    if tk:
        wvt_ref, o_ref, vt_ref = rest
    else:
        (o_ref,) = rest
    x = x_ref[...]
    xn = (x * lax.rsqrt(jnp.mean(x * x, axis=-1, keepdims=True) + EPS) * g_ref[...]).astype(BF16)
    n_out = o_ref.shape[1]
    if tk:
        for r0 in range(0, wvt_ref.shape[0], col_chunk):
            res = _nt_dot(wvt_ref[r0:r0 + col_chunk, :], xn)
            for b in range(vt_ref.shape[0]):
                vt_ref[b, r0:r0 + col_chunk, :] = res[:, b * tk:(b + 1) * tk].astype(BF16)
    if rope_cols:
        cos = cos_ref[...]
        sin = sin_ref[...]
        lane = lax.broadcasted_iota(I32, cos.shape, 1)
        first_half = (lane % HEAD_DIM) < (HEAD_DIM // 2)
    for c0 in range(0, n_out, col_chunk):
        r = jnp.dot(xn, w_ref[:, c0:c0 + col_chunk], preferred_element_type=F32)
        if c0 < rope_cols:
            parts = []
            for l0 in range(0, col_chunk, LANES):
                seg = r[:, l0:l0 + LANES]
                partner = jnp.where(first_half,
                                    pltpu.roll(seg, LANES - HEAD_DIM // 2, 1),
                                    pltpu.roll(seg, HEAD_DIM // 2, 1))
                parts.append(seg * cos + partner * sin)
            r = jnp.concatenate(parts, axis=1)
        o_ref[:, c0:c0 + col_chunk] = r.astype(o_ref.dtype)


def _rope_tables(seq):
    half = HEAD_DIM // 2
    inv_freq = ROPE_THETA ** (-jnp.arange(half, dtype=F32) * 2.0 / HEAD_DIM)
    ang = jnp.arange(seq, dtype=F32)[:, None] * inv_freq[None, :]
    cos, sin = jnp.cos(ang), jnp.sin(ang)
    cos_t = jnp.tile(jnp.concatenate([cos, cos], axis=1), (1, LANES // HEAD_DIM))
    sin_t = jnp.tile(jnp.concatenate([-sin, sin], axis=1), (1, LANES // HEAD_DIM))
    return cos_t, sin_t


def norm_proj(x2, g, w_bf16, seq, rope_cols=0, wvt_bf16=None, tk=0, tm=512):
    t, d = x2.shape
    n_out = w_bf16.shape[1]
    tm = min(tm, seq)
    col_chunk = 512
    assert t % tm == 0 and seq % tm == 0 and n_out % col_chunk == 0 and rope_cols % col_chunk == 0
    cos_t, sin_t = _rope_tables(seq)
    tiles_per_seq = seq // tm
    in_specs = [
        pl.BlockSpec((tm, d), lambda i: (i, 0)),
        pl.BlockSpec((1, d), lambda i: (0, 0)),
        pl.BlockSpec((d, n_out), lambda i: (0, 0)),
        pl.BlockSpec((tm, LANES), lambda i: (i % tiles_per_seq, 0)),
        pl.BlockSpec((tm, LANES), lambda i: (i % tiles_per_seq, 0)),
    ]
    args = [x2, g.reshape(1, d), w_bf16, cos_t, sin_t]
    out_shape = jax.ShapeDtypeStruct((t, n_out), BF16)
    out_specs = pl.BlockSpec((tm, n_out), lambda i: (i, 0))
    if wvt_bf16 is not None:
        n_v = wvt_bf16.shape[0]
        tk = min(tk, tm)
        assert tm % tk == 0 and n_v % col_chunk == 0
        in_specs.append(pl.BlockSpec((n_v, d), lambda i: (0, 0)))
        args.append(wvt_bf16)
        out_shape = (out_shape, jax.ShapeDtypeStruct((t // tk, n_v, tk), BF16))
        out_specs = (out_specs, pl.BlockSpec((tm // tk, n_v, tk), lambda i: (i, 0, 0)))
    return pl.pallas_call(
        functools.partial(_norm_proj_kernel, col_chunk=col_chunk, rope_cols=rope_cols,
                          tk=tk if wvt_bf16 is not None else 0),
        out_shape=out_shape,
        grid=(t // tm,),
        in_specs=in_specs,
        out_specs=out_specs,
        compiler_params=_cparams(("parallel",)),
        name="norm_proj",
    )(*args)


def _stack_query_pair(q_ref, q2_scr, tq):
    lane = lax.broadcasted_iota(I32, (tq, LANES), 1)
    qs = (q_ref[...].astype(F32) * (HEAD_DIM ** -0.5 * LOG2E)).astype(BF16)
    zero = jnp.zeros_like(qs)
    q2_scr[0:tq, :] = jnp.where(lane < HEAD_DIM, qs, zero)
    q2_scr[tq:, :] = jnp.where(lane >= HEAD_DIM, qs, zero)


def _diag_visible(tq, tk, rule):
    key = np.arange(tq)[:, None]
    qry = (np.arange(2 * tq) % tq)[None, :]
    return rule(key, qry).reshape(tq // tk, tk, 2 * tq)


def _diff_attn_kernel(q_ref, k_ref, vt_ref, mask_ref, lq1_ref, lk1_ref, lq2_ref, lk2_ref, g_ref, o_ref,
                      q2_scr, m_scr, l_scr, acc_scr, st0_scr, st1_scr, *, tq, tk, lambda_init):
    i = pl.program_id(2)
    assert tq == 2 * tk
    _stack_query_pair(q_ref, q2_scr, tq)
    m_scr[...] = jnp.full(m_scr.shape, NEG, F32)
    l_scr[...] = jnp.zeros(l_scr.shape, F32)
    acc_scr[...] = jnp.zeros(acc_scr.shape, F32)

    def scores(j, st_scr):
        k = k_ref[pl.ds(pl.multiple_of(j * tk, tk), tk), :]
        st_scr[...] = _nt_dot(k, q2_scr[...])

    def consume(j, st_scr, diag=None):
        st = st_scr[...]
        if diag is not None:
            st = st + mask_ref[diag]
        m_old = m_scr[...]
        m_new = jnp.maximum(m_old, jnp.max(st, axis=0, keepdims=True))
        alpha = jnp.exp2(m_old - m_new)
        pt = jnp.exp2(st - m_new)
        l_scr[...] = alpha * l_scr[...] + jnp.sum(pt, axis=0, keepdims=True)
        acc_scr[...] = alpha * acc_scr[...] + jnp.dot(vt_ref[j], pt.astype(BF16), preferred_element_type=F32)
        m_scr[...] = m_new

    scores(0, st0_scr)

    def body(n, carry):
        scores(2 * n + 1, st1_scr)
        consume(2 * n, st0_scr)
        scores(2 * n + 2, st0_scr)
        consume(2 * n + 1, st1_scr)
        return carry

    lax.fori_loop(0, i, body, 0)
    scores(2 * i + 1, st1_scr)
    consume(2 * i, st0_scr, diag=0)
    consume(2 * i + 1, st1_scr, diag=1)

    ot = acc_scr[...] / l_scr[...]
    lam = (jnp.exp(jnp.sum(lq1_ref[...] * lk1_ref[...], axis=1, keepdims=True))
           - jnp.exp(jnp.sum(lq2_ref[...] * lk2_ref[...], axis=1, keepdims=True)) + lambda_init)
    od = ot[:, 0:tq].T - lam * ot[:, tq:].T
    y = od * lax.rsqrt(jnp.mean(od * od, axis=-1, keepdims=True) + EPS) * g_ref[...]
    o_ref[...] = (y * (1.0 - lambda_init)).astype(o_ref.dtype)


def diff_attention(proj, vt, lq1, lk1, lq2, lk2, subln_g, batch, seq, lambda_init):
    t = proj.shape[0]
    tk = vt.shape[2]
    tq = 2 * tk
    nq = seq // tq
    assert seq % tq == 0 and tk % CHUNK == 0
    k_off = (A_HEADS * 2 * HEAD_DIM) // LANES
    vec = lambda a: a.reshape(1, -1).astype(F32)
    small = lambda n: pl.BlockSpec((1, n), lambda b, h, i: (0, 0))
    mask = jnp.asarray(np.where(_diag_visible(tq, tk, lambda key, qry: key // CHUNK <= qry // CHUNK), 0.0, NEG), F32)
    return pl.pallas_call(
        functools.partial(_diff_attn_kernel, tq=tq, tk=tk, lambda_init=lambda_init),
        out_shape=jax.ShapeDtypeStruct((t, A_HEADS * 2 * HEAD_DIM), BF16),
        grid=(batch, A_HEADS, nq),
        in_specs=[
            pl.BlockSpec((tq, LANES), lambda b, h, i: (b * nq + i, h)),
            pl.BlockSpec((seq, LANES), lambda b, h, i: (b, k_off + h)),
            pl.BlockSpec((seq // tk, LANES, tk), lambda b, h, i: (b, h, 0)),
            pl.BlockSpec((tq // tk, tk, 2 * tq), lambda b, h, i: (0, 0, 0)),
            small(HEAD_DIM), small(HEAD_DIM), small(HEAD_DIM), small(HEAD_DIM), small(2 * HEAD_DIM),
        ],
        out_specs=pl.BlockSpec((tq, LANES), lambda b, h, i: (b * nq + i, h)),
        scratch_shapes=[
            pltpu.VMEM((2 * tq, LANES), BF16),
            pltpu.VMEM((1, 2 * tq), F32),
            pltpu.VMEM((1, 2 * tq), F32),
            pltpu.VMEM((LANES, 2 * tq), F32),
            pltpu.VMEM((tk, 2 * tq), F32),
            pltpu.VMEM((tk, 2 * tq), F32),
        ],
        compiler_params=_cparams(("parallel", "parallel", "arbitrary")),
        name="diff_attn",
    )(proj, proj, vt, mask, vec(lq1), vec(lk1), vec(lq2), vec(lk2), vec(subln_g))


def _stick_kernel(q_ref, k_ref, vt_ref, mask_ref, o_ref, q2_scr, tri_scr, c_scr, acc_scr, st0_scr, st1_scr, *, tq, tk):
    i = pl.program_id(2)
    assert tq == 2 * tk
    _stack_query_pair(q_ref, q2_scr, tq)
    r_i = lax.broadcasted_iota(I32, (tk, tk), 0)
    c_i = lax.broadcasted_iota(I32, (tk, tk), 1)
    tri_scr[...] = jnp.where(c_i > r_i, 1.0, 0.0).astype(BF16)
    c_scr[...] = jnp.zeros(c_scr.shape, F32)
    acc_scr[...] = jnp.zeros(acc_scr.shape, F32)

    def scores(j, st_scr):
        k = k_ref[pl.ds(pl.multiple_of(j * tk, tk), tk), :]
        st_scr[...] = _nt_dot(k, q2_scr[...])

    def consume(j, st_scr, diag=None):
        zt = st_scr[...]
        neg_abs = lax.bitcast_convert_type(lax.bitcast_convert_type(zt, I32) | SIGN_BIT, F32)
        sp = jnp.maximum(zt, 0.0) + jnp.log2(1.0 + jnp.exp2(neg_abs))
        if diag is not None:
            visible = mask_ref[diag]
            spm = sp * visible
        else:
            spm = sp
        after = jnp.dot(tri_scr[...], spm.astype(BF16), preferred_element_type=F32)
        w = jnp.exp2(zt - sp - after - c_scr[...])
        if diag is not None:
            w = w * visible
        acc_scr[...] += jnp.dot(vt_ref[j], w.astype(BF16), preferred_element_type=F32)
        c_scr[...] += jnp.sum(spm, axis=0, keepdims=True)

    scores(2 * i + 1, st0_scr)
    scores(2 * i, st1_scr)
    consume(2 * i + 1, st0_scr, diag=1)
    scores(jnp.maximum(2 * i - 1, 0), st0_scr)
    consume(2 * i, st1_scr, diag=0)

    def body(n, carry):
        j = 2 * (i - n) - 1
        scores(j - 1, st1_scr)
        consume(j, st0_scr)
        scores(jnp.maximum(j - 2, 0), st0_scr)
        consume(j - 1, st1_scr)
        return carry

    lax.fori_loop(0, i, body, 0)
    acc = acc_scr[...]
    vrow = lax.broadcasted_iota(I32, (LANES, tq), 0)
    ot = jnp.where(vrow < HEAD_DIM, acc[:, 0:tq], acc[:, tq:])
    o_ref[...] = ot.T.astype(o_ref.dtype)


def stick_breaking(proj, vt, batch, seq):
    t = proj.shape[0]
    tk = vt.shape[2]
    tq = 2 * tk
    nq = seq // tq
    assert seq % tq == 0
    pairs = (B_HEADS * HEAD_DIM) // LANES
    a_blocks = (A_HEADS * 2 * HEAD_DIM) // LANES
    q_off = 2 * a_blocks
    k_off = q_off + pairs
    v_off = a_blocks
    mask = jnp.asarray(_diag_visible(tq, tk, lambda key, qry: key < qry), F32)
    return pl.pallas_call(
        functools.partial(_stick_kernel, tq=tq, tk=tk),
        out_shape=jax.ShapeDtypeStruct((t, B_HEADS * HEAD_DIM), BF16),
        grid=(batch, pairs, nq),
        in_specs=[
            pl.BlockSpec((tq, LANES), lambda b, p, i: (b * nq + i, q_off + p)),
            pl.BlockSpec((seq, LANES), lambda b, p, i: (b, k_off + p)),
            pl.BlockSpec((seq // tk, LANES, tk), lambda b, p, i: (b, v_off + p, 0)),
            pl.BlockSpec((tq // tk, tk, 2 * tq), lambda b, p, i: (0, 0, 0)),
        ],
        out_specs=pl.BlockSpec((tq, LANES), lambda b, p, i: (b * nq + i, p)),
        scratch_shapes=[
            pltpu.VMEM((2 * tq, LANES), BF16),
            pltpu.VMEM((tk, tk), BF16),
            pltpu.VMEM((1, 2 * tq), F32),
            pltpu.VMEM((LANES, 2 * tq), F32),
            pltpu.VMEM((tk, 2 * tq), F32),
            pltpu.VMEM((tk, 2 * tq), F32),
        ],
        compiler_params=_cparams(("parallel", "parallel", "arbitrary")),
        name="stick_attn",
    )(proj, proj, vt, mask)


def _out_proj_kernel(x_ref, a_ref, b_ref, w_ref, o_ref):
    half = a_ref.shape[1]
    o_ref[...] = (x_ref[...]
                  + jnp.dot(a_ref[...], w_ref[0:half, :], preferred_element_type=F32)
                  + jnp.dot(b_ref[...], w_ref[half:, :], preferred_element_type=F32))


def out_proj_residual(x2, a, b, w_bf16, tm=512):
    t, d = x2.shape
    half = a.shape[1]
    tm = min(tm, t)
    return pl.pallas_call(
        _out_proj_kernel,
        out_shape=jax.ShapeDtypeStruct((t, d), F32),
        grid=(t // tm,),
        in_specs=[
            pl.BlockSpec((tm, d), lambda i: (i, 0)),
            pl.BlockSpec((tm, half), lambda i: (i, 0)),
            pl.BlockSpec((tm, half), lambda i: (i, 0)),
            pl.BlockSpec((2 * half, d), lambda i: (0, 0)),
        ],
        out_specs=pl.BlockSpec((tm, d), lambda i: (i, 0)),
        compiler_params=_cparams(("parallel",)),
        name="out_proj",
    )(x2, a, b, w_bf16)


def _swiglu_chunks(xn, wg_ref, wu_ref, h_scr, ff_chunk):
    d_ff = h_scr.shape[1]
    for c0 in range(0, d_ff, ff_chunk):
        g = jnp.dot(xn, wg_ref[:, c0:c0 + ff_chunk], preferred_element_type=F32)
        u = jnp.dot(xn, wu_ref[:, c0:c0 + ff_chunk], preferred_element_type=F32)
        h_scr[:, c0:c0 + ff_chunk] = (g * jax.nn.sigmoid(g) * u).astype(BF16)


def _ffn_kernel(x_ref, g_ref, wg_ref, wu_ref, wd_ref, o_ref, h_scr, *, ff_chunk):
    x = x_ref[...]
    xn = (x * lax.rsqrt(jnp.mean(x * x, axis=-1, keepdims=True) + EPS) * g_ref[...]).astype(BF16)
    _swiglu_chunks(xn, wg_ref, wu_ref, h_scr, ff_chunk)
    o_ref[...] = x + jnp.dot(h_scr[...], wd_ref[...], preferred_element_type=F32)


def ffn_residual(x2, g, wg, wu, wd, tm=512):
    t, d = x2.shape
    d_ff = wg.shape[1]
    tm = min(tm, t)
    ff_chunk = 256
    assert d_ff % ff_chunk == 0
    resident = lambda shape: pl.BlockSpec(shape, lambda i: (0, 0), pipeline_mode=pl.Buffered(1))
    return pl.pallas_call(
        functools.partial(_ffn_kernel, ff_chunk=ff_chunk),
        out_shape=jax.ShapeDtypeStruct((t, d), F32),
        grid=(t // tm,),
        in_specs=[
            pl.BlockSpec((tm, d), lambda i: (i, 0)),
            pl.BlockSpec((1, d), lambda i: (0, 0)),
            resident((d, d_ff)), resident((d, d_ff)), resident((d_ff, d)),
        ],
        out_specs=pl.BlockSpec((tm, d), lambda i: (i, 0)),
        scratch_shapes=[pltpu.VMEM((tm, d_ff), BF16)],
        compiler_params=_cparams(("parallel",)),
        name="ffn",
    )(x2, g.reshape(1, d), wg, wu, wd)


CONV_HALO = 32


def _conv_kernel(val_ref, gate_ref, pval_ref, pgate_ref, w_ref, b_ref, lg_ref, lb_ref, o_ref,
                 u_scr, ush_scr, c_scr, *, ts, tiles_per_seq, row_blk):
    i = pl.program_id(0)
    glu = lambda v, g: v.astype(F32) * jax.nn.sigmoid(g.astype(F32))
    keep = jnp.where(i % tiles_per_seq == 0, 0.0, 1.0)
    u_scr[0:CONV_HALO, :] = glu(pval_ref[...], pgate_ref[...]) * keep
    u_scr[CONV_HALO:, :] = glu(val_ref[...], gate_ref[...])
    shift = CONV_HALO - (CONV_WIDTH - 1)
    n_ch = val_ref.shape[1]
    n_sh = ush_scr.shape[1]
    for b in range(1, SUBLANES):
        ush_scr[b - 1] = u_scr[b:b + n_sh, :]
    for c0 in range(0, n_ch, LANES):
        for r0 in range(0, ts, row_blk):
            acc = jnp.broadcast_to(b_ref[:, c0:c0 + LANES], (row_blk, LANES))
            for j in range(CONV_WIDTH):
                a, b = divmod(shift + j, SUBLANES)
                src = u_scr if b == 0 else ush_scr.at[b - 1]
                r = r0 + a * SUBLANES
                acc = acc + w_ref[j:j + 1, c0:c0 + LANES] * src[r:r + row_blk, c0:c0 + LANES]
            c_scr[r0:r0 + row_blk, c0:c0 + LANES] = acc
    c = c_scr[...]
    mu = jnp.mean(c, axis=-1, keepdims=True)
    xc = c - mu
    var = jnp.mean(xc * xc, axis=-1, keepdims=True)
    y = xc * lax.rsqrt(var + EPS) * lg_ref[...] + lb_ref[...]
    o_ref[...] = (y * jax.nn.sigmoid(y)).astype(o_ref.dtype)


def conv_module(proj, conv_w, conv_b, ln_g, ln_b, seq, ts=256):
    t = proj.shape[0]
    ts = min(ts, seq)
    n_ch = C_CHANNELS
    halo_blocks = ts // CONV_HALO
    row = lambda a: a.reshape(1, n_ch).astype(F32)
    small = pl.BlockSpec((1, n_ch), lambda i: (0, 0))
    prev = lambda col: pl.BlockSpec((CONV_HALO, n_ch), lambda i: (jnp.maximum(i * halo_blocks - 1, 0), col))
    return pl.pallas_call(
        functools.partial(_conv_kernel, ts=ts, tiles_per_seq=seq // ts, row_blk=64),
        out_shape=jax.ShapeDtypeStruct((t, n_ch), BF16),
        grid=(t // ts,),
        in_specs=[
            pl.BlockSpec((ts, n_ch), lambda i: (i, 0)),
            pl.BlockSpec((ts, n_ch), lambda i: (i, 1)),
            prev(0), prev(1),
            pl.BlockSpec((CONV_WIDTH, n_ch), lambda i: (0, 0)),
            small, small, small,
        ],
        out_specs=pl.BlockSpec((ts, n_ch), lambda i: (i, 0)),
        scratch_shapes=[pltpu.VMEM((ts + CONV_HALO, n_ch), F32),
                        pltpu.VMEM((SUBLANES - 1, ts + CONV_HALO - SUBLANES, n_ch), F32),
                        pltpu.VMEM((ts, n_ch), F32)],
        compiler_params=_cparams(("parallel",)),
        name="conv_module",
    )(proj, proj, proj, proj, conv_w.astype(F32), row(conv_b), row(ln_g), row(ln_b))


def _rel_bias_kernel(tbl_ref, o_ref):
    p = pl.program_id(0)
    kk = lax.broadcasted_iota(I32, (BAND, LANES), 0)
    lane = lax.broadcasted_iota(I32, (BAND, LANES), 1)
    idx = jnp.clip(lane % CHUNK - (kk - LEFT_CHUNKS * CHUNK), -MAX_REL, MAX_REL) + MAX_REL
    first = lane < CHUNK

    def body(j, acc):
        return acc + jnp.where(idx == j, jnp.where(first, tbl_ref[2 * p, j], tbl_ref[2 * p + 1, j]), 0.0)

    o_ref[0] = lax.fori_loop(0, 2 * MAX_REL + 1, body, jnp.zeros((BAND, LANES), F32)) * LOG2E


def rel_bias(rel_table):
    pairs = D_HEADS // 2
    assert 2 * CHUNK == LANES
    return pl.pallas_call(
        _rel_bias_kernel,
        out_shape=jax.ShapeDtypeStruct((pairs, BAND, LANES), F32),
        grid=(pairs,),
        in_specs=[pl.BlockSpec(memory_space=pltpu.SMEM)],
        out_specs=pl.BlockSpec((1, BAND, LANES), lambda p: (p, 0, 0)),
        compiler_params=_cparams(("parallel",)),
        name="rel_bias",
    )(rel_table.astype(F32))


def _chunk_attn_kernel(q_ref, k_ref, v_ref, bias_ref, o_ref, kpad, vpad, *, group, seq):
    i = pl.program_id(2)
    pad = LEFT_CHUNKS * CHUNK

    @pl.when(i == 0)
    def _():
        kpad[0:pad, :] = jnp.zeros((pad, LANES), BF16)
        vpad[0:pad, :] = jnp.zeros((pad, LANES), BF16)
        kpad[pad:, :] = k_ref[...]
        vpad[pad:, :] = v_ref[...]

    lane = lax.broadcasted_iota(I32, (CHUNK, LANES), 1)
    band_chunk = lax.broadcasted_iota(I32, (BAND, LANES), 0) // CHUNK
    bias = bias_ref[0]
    ones = jnp.ones((BAND, LANES), BF16)
    for g in range(group):
        c = i * group + g
        qs = (q_ref[g * CHUNK:(g + 1) * CHUNK, :].astype(F32) * (HEAD_DIM ** -0.5 * LOG2E)).astype(BF16)
        zero = jnp.zeros_like(qs)
        q2 = jnp.concatenate([jnp.where(lane < HEAD_DIM, qs, zero),
                              jnp.where(lane >= HEAD_DIM, qs, zero)], axis=0)
        start = pl.multiple_of(c * CHUNK, CHUNK)
        kw = kpad[pl.ds(start, BAND), :]
        vw = vpad[pl.ds(start, BAND), :]
        st = _nt_dot(kw, q2) + bias
        st = jnp.where(band_chunk >= LEFT_CHUNKS - c, st, NEG)
        pt = jnp.exp2(st - jnp.max(st, axis=0, keepdims=True))
        o2 = lax.dot_general(pt.astype(BF16), jnp.concatenate([vw, ones], axis=1),
                             (((0,), (0,)), ((), ())), preferred_element_type=F32)
        o = o2[:, 0:LANES] / o2[:, LANES:]
        o_ref[g * CHUNK:(g + 1) * CHUNK, :] = jnp.where(lane < HEAD_DIM, o[0:CHUNK, :],
                                                        o[CHUNK:, :]).astype(o_ref.dtype)


def chunk_rel_attention(proj, bias, batch, seq, group=16):
    t = proj.shape[0]
    pairs = D_HEADS // 2
    q_off = (2 * C_CHANNELS) // LANES
    k_off = q_off + pairs
    v_off = k_off + pairs
    tq = group * CHUNK
    nq = seq // tq
    return pl.pallas_call(
        functools.partial(_chunk_attn_kernel, group=group, seq=seq),
        out_shape=jax.ShapeDtypeStruct((t, D_HEADS * HEAD_DIM), BF16),
        grid=(batch, pairs, nq),
        in_specs=[
            pl.BlockSpec((tq, LANES), lambda b, p, i: (b * nq + i, q_off + p)),
            pl.BlockSpec((seq, LANES), lambda b, p, i: (b, k_off + p)),
            pl.BlockSpec((seq, LANES), lambda b, p, i: (b, v_off + p)),
            pl.BlockSpec((1, BAND, LANES), lambda b, p, i: (p, 0, 0)),
        ],
        out_specs=pl.BlockSpec((tq, LANES), lambda b, p, i: (b * nq + i, p)),
        scratch_shapes=[pltpu.VMEM((seq + LEFT_CHUNKS * CHUNK, LANES), BF16),
                        pltpu.VMEM((seq + LEFT_CHUNKS * CHUNK, LANES), BF16)],
        compiler_params=_cparams(("parallel", "parallel", "arbitrary")),
        name="chunk_attn",
    )(proj, proj, proj, bias)


def _store_row_tiles(ref, val):
    n, d = val.shape
    sub = d // LANES
    for k in range(sub):
        ref[pl.ds(k, n, stride=sub), :] = val[:, k * LANES:(k + 1) * LANES]


def _load_row_tiles(ref, n, d):
    sub = d // LANES
    return jnp.concatenate([ref[pl.ds(k, n, stride=sub), :] for k in range(sub)], axis=1)


def _router_kernel(x_ref, g_ref, wr_ref, h_ref, meta_ref, cnt_ref, carry_scr, tri_scr, *, tr):
    i = pl.program_id(0)

    @pl.when(i == 0)
    def _():
        carry_scr[...] = jnp.zeros(carry_scr.shape, F32)
        r_i = lax.broadcasted_iota(I32, (tr, tr), 0)
        c_i = lax.broadcasted_iota(I32, (tr, tr), 1)
        tri_scr[...] = jnp.where(r_i < c_i, 1.0, 0.0).astype(BF16)

    x = x_ref[...]
    hn = x * lax.rsqrt(jnp.mean(x * x, axis=-1, keepdims=True) + EPS) * g_ref[...]
    _store_row_tiles(h_ref, hn)
    h_hi = hn.astype(BF16)
    h_lo = (hn - h_hi.astype(F32)).astype(BF16)
    w = wr_ref[...]
    w_hi = w.astype(BF16)
    w_lo = (w - w_hi.astype(F32)).astype(BF16)
    logits = _nt_dot(w_hi, h_hi) + (_nt_dot(w_hi, h_lo) + _nt_dot(w_lo, h_hi))
    eidx = lax.broadcasted_iota(I32, logits.shape, 0).astype(F32)
    m1 = jnp.max(logits, axis=0, keepdims=True)
    i1 = jnp.min(jnp.where(logits == m1, eidx, float(N_EXPERTS)), axis=0, keepdims=True)
    rest = jnp.where(eidx == i1, -jnp.inf, logits)
    m2 = jnp.max(rest, axis=0, keepdims=True)
    i2 = jnp.min(jnp.where(rest == m2, eidx, float(N_EXPERTS)), axis=0, keepdims=True)
    e = jnp.exp(m2 - m1)
    g1 = 1.0 / (1.0 + e)
    g2 = e / (1.0 + e)
    sel1 = eidx == i1
    sel2 = eidx == i2
    onehot = jnp.where(sel1 | sel2, 1.0, 0.0)
    rank = jnp.dot(onehot.astype(BF16), tri_scr[...], preferred_element_type=F32) + carry_scr[...]
    r1 = jnp.sum(jnp.where(sel1, rank, 0.0), axis=0, keepdims=True)
    r2 = jnp.sum(jnp.where(sel2, rank, 0.0), axis=0, keepdims=True)
    carry = carry_scr[...] + jnp.sum(onehot, axis=1, keepdims=True)
    carry_scr[...] = carry
    zero = jnp.zeros_like(g1)
    meta_ref[0] = jnp.concatenate([i1, i2, r1, r2, g1, g2, zero, zero], axis=0)
    cnt_ref[...] = jnp.broadcast_to(carry, cnt_ref.shape)


def router(x2, g, w_router, tr=512):
    t, d = x2.shape
    tr = min(tr, t)
    sub = d // LANES
    return pl.pallas_call(
        functools.partial(_router_kernel, tr=tr),
        out_shape=(jax.ShapeDtypeStruct((t * sub, LANES), F32),
                   jax.ShapeDtypeStruct((t // tr, 8, tr), F32),
                   jax.ShapeDtypeStruct((N_EXPERTS, LANES), F32)),
        grid=(t // tr,),
        in_specs=[
            pl.BlockSpec((tr, d), lambda i: (i, 0)),
            pl.BlockSpec((1, d), lambda i: (0, 0)),
            pl.BlockSpec((N_EXPERTS, d), lambda i: (0, 0)),
        ],
        out_specs=(pl.BlockSpec((tr * sub, LANES), lambda i: (i, 0)),
                   pl.BlockSpec((1, 8, tr), lambda i: (i, 0, 0)),
                   pl.BlockSpec((N_EXPERTS, LANES), lambda i: (0, 0))),
        scratch_shapes=[pltpu.VMEM((N_EXPERTS, 1), F32), pltpu.VMEM((tr, tr), BF16)],
        compiler_params=_cparams(("arbitrary",)),
        name="router",
    )(x2, g.reshape(1, d), w_router.T.astype(F32))


ROW_DMA_UNROLL = 8


def _rows(ref, row, sub):
    return ref.at[pl.ds(pl.multiple_of(row * sub, sub), sub)]


def _scatter_rows_kernel(ends_ref, idx1_ref, idx2_ref, src_ref, dst_ref, zero_scr, sem, *, rows, sub, pad_rows):
    @pl.when(pl.program_id(0) == 0)
    def _():
        zero_scr[...] = jnp.zeros(zero_scr.shape, zero_scr.dtype)

        def tail(e):
            start = jnp.maximum(ends_ref[e] - pad_rows, 0)
            return pltpu.make_async_copy(zero_scr, dst_ref.at[pl.ds(pl.multiple_of(start * sub, sub),
                                                                  pad_rows * sub)], sem)

        n_exp = ends_ref.shape[0]
        for e in range(n_exp):
            tail(e).start()
        for e in range(n_exp):
            tail(e).wait()
        n_rows = dst_ref.shape[0] // sub
        for t in range(n_exp):
            start = ends_ref[n_exp - 1] + t * pad_rows

            @pl.when(start < n_rows)
            def _():
                spare = pltpu.make_async_copy(
                    zero_scr, dst_ref.at[pl.ds(pl.multiple_of(start * sub, sub), pad_rows * sub)], sem)
                spare.start()
                spare.wait()

    def issue(r, carry):
        pltpu.make_async_copy(_rows(src_ref, r, sub), _rows(dst_ref, idx1_ref[0, 0, r], sub), sem).start()
        pltpu.make_async_copy(_rows(src_ref, r, sub), _rows(dst_ref, idx2_ref[0, 0, r], sub), sem).start(priority=1)
        return carry

    lax.fori_loop(0, rows, issue, 0, unroll=ROW_DMA_UNROLL)
    for _ in range(2):
        pltpu.make_async_copy(src_ref, dst_ref.at[pl.ds(0, rows * sub)], sem).wait()


def scatter_rows(src, idx1, idx2, ends, n_rows, pad_rows, sub, rows=512):
    n = idx1.shape[0]
    rows = min(rows, n)
    assert n % rows == 0 and rows % ROW_DMA_UNROLL == 0 and n_rows >= max(rows, pad_rows)
    idx_spec = pl.BlockSpec((1, 1, rows), lambda i, ends: (i, 0, 0), memory_space=pltpu.SMEM)
    grid_spec = pltpu.PrefetchScalarGridSpec(
        num_scalar_prefetch=1,
        grid=(n // rows,),
        in_specs=[idx_spec, idx_spec, pl.BlockSpec((rows * sub, LANES), lambda i, ends: (i, 0))],
        out_specs=pl.BlockSpec(memory_space=pl.ANY),
        scratch_shapes=[pltpu.VMEM((pad_rows * sub, LANES), src.dtype), pltpu.SemaphoreType.DMA(())],
    )
    return pl.pallas_call(
        functools.partial(_scatter_rows_kernel, rows=rows, sub=sub, pad_rows=pad_rows),
        out_shape=jax.ShapeDtypeStruct((n_rows * sub, LANES), src.dtype),
        grid_spec=grid_spec,
        compiler_params=_cparams(("arbitrary",)),
        name="row_scatter",
    )(ends, idx1.reshape(n // rows, 1, rows), idx2.reshape(n // rows, 1, rows), src)


def _expert_ffn_kernel(te_ref, nt_ref, x_ref, wg_ref, wu_ref, wd_ref, o_ref, xn_scr, h_scr, acc_scr, *, ff_chunk):
    n = pl.program_id(0)
    f = pl.program_id(1)
    tm, d = acc_scr.shape

    @pl.when(n < nt_ref[0])
    def _():
        @pl.when(f == 0)
        def _():
            xn_scr[...] = _load_row_tiles(x_ref, tm, d).astype(BF16)

        _swiglu_chunks(xn_scr[...], wg_ref.at[0], wu_ref.at[0], h_scr, ff_chunk)
        part = jnp.dot(h_scr[...], wd_ref[0], preferred_element_type=F32)

        @pl.when(f == 0)
        def _():
            acc_scr[...] = part

        @pl.when(f > 0)
        def _():
            acc_scr[...] += part

        @pl.when(f == pl.num_programs(1) - 1)
        def _():
            _store_row_tiles(o_ref, acc_scr[...])

    @pl.when((n >= nt_ref[0]) & (f == pl.num_programs(1) - 1))
    def _():
        o_ref[...] = jnp.zeros(o_ref.shape, o_ref.dtype)


def expert_ffn(xs, tile_expert, n_tiles_used, wg, wu, wd, tm, ff_split=2):
    d, d_ff = wg.shape[1], wg.shape[2]
    sub = d // LANES
    nr = xs.shape[0] // sub
    ff_blk = d_ff // ff_split
    ff_chunk = 256
    assert nr % tm == 0 and ff_blk % ff_chunk == 0
    grid_spec = pltpu.PrefetchScalarGridSpec(
        num_scalar_prefetch=2,
        grid=(nr // tm, ff_split),
        in_specs=[
            pl.BlockSpec((tm * sub, LANES), lambda n, f, te, nt: (jnp.minimum(n, nt[0] - 1), 0)),
            pl.BlockSpec((1, d, ff_blk), lambda n, f, te, nt: (te[n], 0, f)),
            pl.BlockSpec((1, d, ff_blk), lambda n, f, te, nt: (te[n], 0, f)),
            pl.BlockSpec((1, ff_blk, d), lambda n, f, te, nt: (te[n], f, 0)),
        ],
        out_specs=pl.BlockSpec((tm * sub, LANES), lambda n, f, te, nt: (n, 0)),
        scratch_shapes=[pltpu.VMEM((tm, d), BF16), pltpu.VMEM((tm, ff_blk), BF16), pltpu.VMEM((tm, d), F32)],
    )
    return pl.pallas_call(
        functools.partial(_expert_ffn_kernel, ff_chunk=ff_chunk),
        out_shape=jax.ShapeDtypeStruct((nr * sub, LANES), F32),
        grid_spec=grid_spec,
        compiler_params=_cparams(("arbitrary", "arbitrary")),
        name="expert_ffn",
    )(tile_expert, n_tiles_used, xs, wg, wu, wd)


def _combine_kernel(idx1_ref, idx2_ref, x_ref, ys_ref, g1_ref, g2_ref, gn_ref, o_ref, y1_buf, y2_buf, sem, *,
                    final_norm):
    tm, d = x_ref.shape
    sub = d // LANES

    def issue(r, carry):
        pltpu.make_async_copy(_rows(ys_ref, idx1_ref[0, 0, r], sub), _rows(y1_buf, r, sub), sem).start()
        pltpu.make_async_copy(_rows(ys_ref, idx2_ref[0, 0, r], sub), _rows(y2_buf, r, sub), sem).start(priority=1)
        return carry

    lax.fori_loop(0, tm, issue, 0, unroll=ROW_DMA_UNROLL)
    for buf in (y1_buf, y2_buf):
        pltpu.make_async_copy(ys_ref.at[pl.ds(0, tm * sub)], buf, sem).wait()
    y = (x_ref[...] + g1_ref[...] * _load_row_tiles(y1_buf, tm, d)
         + g2_ref[...] * _load_row_tiles(y2_buf, tm, d))
    if final_norm:
        y = y * lax.rsqrt(jnp.mean(y * y, axis=-1, keepdims=True) + EPS) * gn_ref[...]
    o_ref[...] = y


def combine(x2, ys, idx1, idx2, g1, g2, gn, final_norm, tm=512):
    t, d = x2.shape
    tm = min(tm, t)
    sub = d // LANES
    assert t % tm == 0 and tm % ROW_DMA_UNROLL == 0 and ys.shape[0] >= tm * sub
    big = pl.BlockSpec((tm, d), lambda i: (i, 0))
    col = pl.BlockSpec((tm, 1), lambda i: (i, 0))
    idx_spec = pl.BlockSpec((1, 1, tm), lambda i: (i, 0, 0), memory_space=pltpu.SMEM)
    return pl.pallas_call(
        functools.partial(_combine_kernel, final_norm=final_norm),
        out_shape=jax.ShapeDtypeStruct((t, d), F32),
        grid=(t // tm,),
        in_specs=[idx_spec, idx_spec, big, pl.BlockSpec(memory_space=pl.ANY), col, col,
                  pl.BlockSpec((1, d), lambda i: (0, 0))],
        out_specs=big,
        scratch_shapes=[pltpu.VMEM((tm * sub, LANES), F32), pltpu.VMEM((tm * sub, LANES), F32),
                        pltpu.SemaphoreType.DMA(())],
        compiler_params=_cparams(("arbitrary",)),
        name="moe_combine",
    )(idx1.reshape(t // tm, 1, tm), idx2.reshape(t // tm, 1, tm), x2, ys, g1, g2, gn.reshape(1, d))


def moe_residual(x2, g, w_router, wg, wu, wd, final_gain, final_norm, tm=512):
    t, d = x2.shape
    tm = min(tm, t)
    hn, meta, counts = router(x2, g, w_router)
    tr = meta.shape[2]
    field = lambda r: meta[:, r, :].reshape(t)
    idx1, idx2 = field(0).astype(I32), field(1).astype(I32)
    rank1, rank2 = field(2).astype(I32), field(3).astype(I32)
    gate1, gate2 = field(4), field(5)
    cnt = counts[:, 0].astype(I32)
    padded = ((cnt + tm - 1) // tm) * tm
    ends = jnp.cumsum(padded)
    offs = ends - padded
    pos1 = offs[idx1] + rank1
    pos2 = offs[idx2] + rank2
    n_rows = TOP_K * t + N_EXPERTS * tm
    tile_start = jnp.arange(n_rows // tm, dtype=I32) * tm
    tile_expert = jnp.minimum(jnp.sum(tile_start[:, None] >= ends[None, :], axis=1), N_EXPERTS - 1).astype(I32)
    n_tiles_used = (ends[-1:] // tm).astype(I32)

    sub = d // LANES
    xs = scatter_rows(hn, pos1, pos2, ends.astype(I32), n_rows, tm, sub)
    ys = expert_ffn(xs, tile_expert, n_tiles_used, wg, wu, wd, tm)
    return combine(x2, ys, pos1, pos2, gate1.reshape(t, 1), gate2.reshape(t, 1), final_gain, final_norm)


def kernel(x, ev_norm_mix, ev_w_in, ev_lambda_q1, ev_lambda_k1, ev_lambda_q2, ev_lambda_k2, ev_subln, ev_w_out,
           ev_norm_ffn, ev_ffn_gate, ev_ffn_up, ev_ffn_down, od_norm_mix, od_w_in, od_conv_w, od_conv_b,
           od_conv_ln_g, od_conv_ln_b, od_rel_bias, od_w_out, od_norm_ffn, od_router, od_exp_gate, od_exp_up,
           od_exp_down, final_norm):
    batch, seq, d = x.shape
    depth = 2 * ev_w_in.shape[0]
    assert od_w_in.shape[0] * 2 == depth
    x2 = x.reshape(batch * seq, d)
    bf = lambda a: a.astype(BF16)
    a_qk = A_HEADS * 2 * HEAD_DIM
    a_v = A_HEADS * 2 * HEAD_DIM
    b_w = B_HEADS * HEAD_DIM
    attn_k_block = 256
    for layer in range(depth):
        i = layer // 2
        if layer % 2 == 0:
            lambda_init = 0.8 - 0.6 * math.exp(-0.3 * layer)
            w = ev_w_in[i]
            va0, qb0 = 2 * a_qk, 2 * a_qk + a_v
            vb0 = qb0 + 2 * b_w
            w_qk = bf(jnp.concatenate([w[:, 0:va0], w[:, qb0:vb0]], axis=1))
            w_vt = bf(jnp.concatenate([w[:, va0:qb0], w[:, vb0:]], axis=1).T)
            proj, vt = norm_proj(x2, ev_norm_mix[i], w_qk, seq, rope_cols=2 * a_qk, wvt_bf16=w_vt,
                                 tk=attn_k_block)
            oa = diff_attention(proj, vt, ev_lambda_q1[i], ev_lambda_k1[i], ev_lambda_q2[i], ev_lambda_k2[i],
                                ev_subln[i], batch, seq, lambda_init)
            ob = stick_breaking(proj, vt, batch, seq)
            x2 = out_proj_residual(x2, oa, ob, bf(ev_w_out[i]))
            x2 = ffn_residual(x2, ev_norm_ffn[i], bf(ev_ffn_gate[i]), bf(ev_ffn_up[i]), bf(ev_ffn_down[i]))
        else:
            proj = norm_proj(x2, od_norm_mix[i], bf(od_w_in[i]), seq)
            c = conv_module(proj, od_conv_w[i], od_conv_b[i], od_conv_ln_g[i], od_conv_ln_b[i], seq)
            od = chunk_rel_attention(proj, rel_bias(od_rel_bias[i]), batch, seq)
            x2 = out_proj_residual(x2, c, od, bf(od_w_out[i]))
            x2 = moe_residual(x2, od_norm_ffn[i], od_router[i], bf(od_exp_gate[i]), bf(od_exp_up[i]),
                              bf(od_exp_down[i]), final_norm, final_norm=(layer == depth - 1))
    return x2.reshape(batch, seq, d)
```

```python
import functools
import math

import numpy as np
import jax
import jax.numpy as jnp
from jax import lax
from jax.experimental import pallas as pl
from jax.experimental.pallas import tpu as pltpu

F32 = jnp.float32
BF16 = jnp.bfloat16
I32 = jnp.int32

HEAD_DIM = 64
CHUNK = 64
ROPE_THETA = 10000.0
EPS = 1e-6
A_HEADS = 4
B_HEADS = 8
C_CHANNELS = 512
CONV_WIDTH = 31
D_HEADS = 8
LEFT_CHUNKS = 8
BAND = (LEFT_CHUNKS + 1) * CHUNK
MAX_REL = 128
N_EXPERTS = 8
TOP_K = 2

LANES = 128
SUBLANES = 8
NEG = -1e30
LOG2E = math.log2(math.e)
SIGN_BIT = np.int32(-2 ** 31)
VMEM_LIMIT = 56 * 1024 * 1024


def _cparams(sem):
    return pltpu.CompilerParams(dimension_semantics=sem, vmem_limit_bytes=VMEM_LIMIT)


def _nt_dot(a, b):
    return lax.dot_general(a, b, (((1,), (1,)), ((), ())), preferred_element_type=F32)


def _norm_proj_kernel(x_ref, g_ref, w_ref, cos_ref, sin_ref, *rest, col_chunk, rope_cols, tk):
    if tk:
        wvt_ref, o_ref, vt_ref = rest
    else:
        (o_ref,) = rest
    x = x_ref[...]
    xn = (x * lax.rsqrt(jnp.mean(x * x, axis=-1, keepdims=True) + EPS) * g_ref[...]).astype(BF16)
    n_out = o_ref.shape[1]
    if tk:
        for r0 in range(0, wvt_ref.shape[0], col_chunk):
            res = _nt_dot(wvt_ref[r0:r0 + col_chunk, :], xn)
            for b in range(vt_ref.shape[0]):
                vt_ref[b, r0:r0 + col_chunk, :] = res[:, b * tk:(b + 1) * tk].astype(BF16)
    if rope_cols:
        cos = cos_ref[...]
        sin = sin_ref[...]
        lane = lax.broadcasted_iota(I32, cos.shape, 1)
        first_half = (lane % HEAD_DIM) < (HEAD_DIM // 2)
    for c0 in range(0, n_out, col_chunk):
        r = jnp.dot(xn, w_ref[:, c0:c0 + col_chunk], preferred_element_type=F32)
        if c0 < rope_cols:
            parts = []
            for l0 in range(0, col_chunk, LANES):
                seg = r[:, l0:l0 + LANES]
                partner = jnp.where(first_half,
                                    pltpu.roll(seg, LANES - HEAD_DIM // 2, 1),
                                    pltpu.roll(seg, HEAD_DIM // 2, 1))
                parts.append(seg * cos + partner * sin)
            r = jnp.concatenate(parts, axis=1)
        o_ref[:, c0:c0 + col_chunk] = r.astype(o_ref.dtype)


def _rope_tables(seq):
    half = HEAD_DIM // 2
    inv_freq = ROPE_THETA ** (-jnp.arange(half, dtype=F32) * 2.0 / HEAD_DIM)
    ang = jnp.arange(seq, dtype=F32)[:, None] * inv_freq[None, :]
    cos, sin = jnp.cos(ang), jnp.sin(ang)
    cos_t = jnp.tile(jnp.concatenate([cos, cos], axis=1), (1, LANES // HEAD_DIM))
    sin_t = jnp.tile(jnp.concatenate([-sin, sin], axis=1), (1, LANES // HEAD_DIM))
    return cos_t, sin_t


def norm_proj(x2, g, w_bf16, seq, rope_cols=0, wvt_bf16=None, tk=0, tm=512):
    t, d = x2.shape
    n_out = w_bf16.shape[1]
    tm = min(tm, seq)
    col_chunk = 512
    assert t % tm == 0 and seq % tm == 0 and n_out % col_chunk == 0 and rope_cols % col_chunk == 0
    cos_t, sin_t = _rope_tables(seq)
    tiles_per_seq = seq // tm
    in_specs = [
        pl.BlockSpec((tm, d), lambda i: (i, 0)),
        pl.BlockSpec((1, d), lambda i: (0, 0)),
        pl.BlockSpec((d, n_out), lambda i: (0, 0)),
        pl.BlockSpec((tm, LANES), lambda i: (i % tiles_per_seq, 0)),
        pl.BlockSpec((tm, LANES), lambda i: (i % tiles_per_seq, 0)),
    ]
    args = [x2, g.reshape(1, d), w_bf16, cos_t, sin_t]
    out_shape = jax.ShapeDtypeStruct((t, n_out), BF16)
    out_specs = pl.BlockSpec((tm, n_out), lambda i: (i, 0))
    if wvt_bf16 is not None:
        n_v = wvt_bf16.shape[0]
        tk = min(tk, tm)
        assert tm % tk == 0 and n_v % col_chunk == 0
        in_specs.append(pl.BlockSpec((n_v, d), lambda i: (0, 0)))
        args.append(wvt_bf16)
        out_shape = (out_shape, jax.ShapeDtypeStruct((t // tk, n_v, tk), BF16))
        out_specs = (out_specs, pl.BlockSpec((tm // tk, n_v, tk), lambda i: (i, 0, 0)))
    return pl.pallas_call(
        functools.partial(_norm_proj_kernel, col_chunk=col_chunk, rope_cols=rope_cols,
                          tk=tk if wvt_bf16 is not None else 0),
        out_shape=out_shape,
        grid=(t // tm,),
        in_specs=in_specs,
        out_specs=out_specs,
        compiler_params=_cparams(("parallel",)),
        name="norm_proj",
    )(*args)


def _stack_query_pair(q_ref, q2_scr, tq):
    lane = lax.broadcasted_iota(I32, (tq, LANES), 1)
    qs = (q_ref[...].astype(F32) * (HEAD_DIM ** -0.5 * LOG2E)).astype(BF16)
    zero = jnp.zeros_like(qs)
    q2_scr[0:tq, :] = jnp.where(lane < HEAD_DIM, qs, zero)
    q2_scr[tq:, :] = jnp.where(lane >= HEAD_DIM, qs, zero)


def _diag_visible(tq, tk, rule):
    key = np.arange(tq)[:, None]
    qry = (np.arange(2 * tq) % tq)[None, :]
    return rule(key, qry).reshape(tq // tk, tk, 2 * tq)


def _diff_attn_kernel(q_ref, k_ref, vt_ref, mask_ref, lq1_ref, lk1_ref, lq2_ref, lk2_ref, g_ref, o_ref,
                      q2_scr, m_scr, l_scr, acc_scr, st0_scr, st1_scr, *, tq, tk, lambda_init):
    i = pl.program_id(2)
    assert tq == 2 * tk
    _stack_query_pair(q_ref, q2_scr, tq)
    m_scr[...] = jnp.full(m_scr.shape, NEG, F32)
    l_scr[...] = jnp.zeros(l_scr.shape, F32)
    acc_scr[...] = jnp.zeros(acc_scr.shape, F32)

    def scores(j, st_scr):
        k = k_ref[pl.ds(pl.multiple_of(j * tk, tk), tk), :]
        st_scr[...] = _nt_dot(k, q2_scr[...])

    def consume(j, st_scr, diag=None):
        st = st_scr[...]
        if diag is not None:
            st = st + mask_ref[diag]
        m_old = m_scr[...]
        m_new = jnp.maximum(m_old, jnp.max(st, axis=0, keepdims=True))
        alpha = jnp.exp2(m_old - m_new)
        pt = jnp.exp2(st - m_new)
        l_scr[...] = alpha * l_scr[...] + jnp.sum(pt, axis=0, keepdims=True)
        acc_scr[...] = alpha * acc_scr[...] + jnp.dot(vt_ref[j], pt.astype(BF16), preferred_element_type=F32)
        m_scr[...] = m_new

    scores(0, st0_scr)

    def body(n, carry):
        scores(2 * n + 1, st1_scr)
        consume(2 * n, st0_scr)
        scores(2 * n + 2, st0_scr)
        consume(2 * n + 1, st1_scr)
        return carry

    lax.fori_loop(0, i, body, 0)
    scores(2 * i + 1, st1_scr)
    consume(2 * i, st0_scr, diag=0)
    consume(2 * i + 1, st1_scr, diag=1)

    ot = acc_scr[...] / l_scr[...]
    lam = (jnp.exp(jnp.sum(lq1_ref[...] * lk1_ref[...], axis=1, keepdims=True))
           - jnp.exp(jnp.sum(lq2_ref[...] * lk2_ref[...], axis=1, keepdims=True)) + lambda_init)
    od = ot[:, 0:tq].T - lam * ot[:, tq:].T
    y = od * lax.rsqrt(jnp.mean(od * od, axis=-1, keepdims=True) + EPS) * g_ref[...]
    o_ref[...] = (y * (1.0 - lambda_init)).astype(o_ref.dtype)


def diff_attention(proj, vt, lq1, lk1, lq2, lk2, subln_g, batch, seq, lambda_init):
    t = proj.shape[0]
    tk = vt.shape[2]
    tq = 2 * tk
    nq = seq // tq
    assert seq % tq == 0 and tk % CHUNK == 0
    k_off = (A_HEADS * 2 * HEAD_DIM) // LANES
    vec = lambda a: a.reshape(1, -1).astype(F32)
    small = lambda n: pl.BlockSpec((1, n), lambda b, h, i: (0, 0))
    mask = jnp.asarray(np.where(_diag_visible(tq, tk, lambda key, qry: key // CHUNK <= qry // CHUNK), 0.0, NEG), F32)
    return pl.pallas_call(
        functools.partial(_diff_attn_kernel, tq=tq, tk=tk, lambda_init=lambda_init),
        out_shape=jax.ShapeDtypeStruct((t, A_HEADS * 2 * HEAD_DIM), BF16),
        grid=(batch, A_HEADS, nq),
        in_specs=[
            pl.BlockSpec((tq, LANES), lambda b, h, i: (b * nq + i, h)),
            pl.BlockSpec((seq, LANES), lambda b, h, i: (b, k_off + h)),
            pl.BlockSpec((seq // tk, LANES, tk), lambda b, h, i: (b, h, 0)),
            pl.BlockSpec((tq // tk, tk, 2 * tq), lambda b, h, i: (0, 0, 0)),
            small(HEAD_DIM), small(HEAD_DIM), small(HEAD_DIM), small(HEAD_DIM), small(2 * HEAD_DIM),
        ],
        out_specs=pl.BlockSpec((tq, LANES), lambda b, h, i: (b * nq + i, h)),
        scratch_shapes=[
            pltpu.VMEM((2 * tq, LANES), BF16),
            pltpu.VMEM((1, 2 * tq), F32),
            pltpu.VMEM((1, 2 * tq), F32),
            pltpu.VMEM((LANES, 2 * tq), F32),
            pltpu.VMEM((tk, 2 * tq), F32),
            pltpu.VMEM((tk, 2 * tq), F32),
        ],
        compiler_params=_cparams(("parallel", "parallel", "arbitrary")),
        name="diff_attn",
    )(proj, proj, vt, mask, vec(lq1), vec(lk1), vec(lq2), vec(lk2), vec(subln_g))


def _stick_kernel(q_ref, k_ref, vt_ref, mask_ref, o_ref, q2_scr, tri_scr, c_scr, acc_scr, st0_scr, st1_scr, *, tq, tk):
    i = pl.program_id(2)
    assert tq == 2 * tk
    _stack_query_pair(q_ref, q2_scr, tq)
    r_i = lax.broadcasted_iota(I32, (tk, tk), 0)
    c_i = lax.broadcasted_iota(I32, (tk, tk), 1)
    tri_scr[...] = jnp.where(c_i > r_i, 1.0, 0.0).astype(BF16)
    c_scr[...] = jnp.zeros(c_scr.shape, F32)
    acc_scr[...] = jnp.zeros(acc_scr.shape, F32)

    def scores(j, st_scr):
        k = k_ref[pl.ds(pl.multiple_of(j * tk, tk), tk), :]
        st_scr[...] = _nt_dot(k, q2_scr[...])

    def consume(j, st_scr, diag=None):
        zt = st_scr[...]
        neg_abs = lax.bitcast_convert_type(lax.bitcast_convert_type(zt, I32) | SIGN_BIT, F32)
        sp = jnp.maximum(zt, 0.0) + jnp.log2(1.0 + jnp.exp2(neg_abs))
        if diag is not None:
            visible = mask_ref[diag]
            spm = sp * visible
        else:
            spm = sp
        after = jnp.dot(tri_scr[...], spm.astype(BF16), preferred_element_type=F32)
        w = jnp.exp2(zt - sp - after - c_scr[...])
        if diag is not None:
            w = w * visible
        acc_scr[...] += jnp.dot(vt_ref[j], w.astype(BF16), preferred_element_type=F32)
        c_scr[...] += jnp.sum(spm, axis=0, keepdims=True)

    scores(2 * i + 1, st0_scr)
    scores(2 * i, st1_scr)
    consume(2 * i + 1, st0_scr, diag=1)
    scores(jnp.maximum(2 * i - 1, 0), st0_scr)
    consume(2 * i, st1_scr, diag=0)

    def body(n, carry):
        j = 2 * (i - n) - 1
        scores(j - 1, st1_scr)
        consume(j, st0_scr)
        scores(jnp.maximum(j - 2, 0), st0_scr)
        consume(j - 1, st1_scr)
        return carry

    lax.fori_loop(0, i, body, 0)
    acc = acc_scr[...]
    vrow = lax.broadcasted_iota(I32, (LANES, tq), 0)
    ot = jnp.where(vrow < HEAD_DIM, acc[:, 0:tq], acc[:, tq:])
    o_ref[...] = ot.T.astype(o_ref.dtype)


def stick_breaking(proj, vt, batch, seq):
    t = proj.shape[0]
    tk = vt.shape[2]
    tq = 2 * tk
    nq = seq // tq
    assert seq % tq == 0
    pairs = (B_HEADS * HEAD_DIM) // LANES
    a_blocks = (A_HEADS * 2 * HEAD_DIM) // LANES
    q_off = 2 * a_blocks
    k_off = q_off + pairs
    v_off = a_blocks
    mask = jnp.asarray(_diag_visible(tq, tk, lambda key, qry: key < qry), F32)
    return pl.pallas_call(
        functools.partial(_stick_kernel, tq=tq, tk=tk),
        out_shape=jax.ShapeDtypeStruct((t, B_HEADS * HEAD_DIM), BF16),
        grid=(batch, pairs, nq),
        in_specs=[
            pl.BlockSpec((tq, LANES), lambda b, p, i: (b * nq + i, q_off + p)),
            pl.BlockSpec((seq, LANES), lambda b, p, i: (b, k_off + p)),
            pl.BlockSpec((seq // tk, LANES, tk), lambda b, p, i: (b, v_off + p, 0)),
            pl.BlockSpec((tq // tk, tk, 2 * tq), lambda b, p, i: (0, 0, 0)),
        ],
        out_specs=pl.BlockSpec((tq, LANES), lambda b, p, i: (b * nq + i, p)),
        scratch_shapes=[
            pltpu.VMEM((2 * tq, LANES), BF16),
            pltpu.VMEM((tk, tk), BF16),
            pltpu.VMEM((1, 2 * tq), F32),
            pltpu.VMEM((LANES, 2 * tq), F32),
            pltpu.VMEM((tk, 2 * tq), F32),
            pltpu.VMEM((tk, 2 * tq), F32),
        ],
        compiler_params=_cparams(("parallel", "parallel", "arbitrary")),
        name="stick_attn",
    )(proj, proj, vt, mask)


def _out_proj_kernel(x_ref, a_ref, b_ref, w_ref, o_ref):
    half = a_ref.shape[1]
    o_ref[...] = (x_ref[...]
                  + jnp.dot(a_ref[...], w_ref[0:half, :], preferred_element_type=F32)
                  + jnp.dot(b_ref[...], w_ref[half:, :], preferred_element_type=F32))


def out_proj_residual(x2, a, b, w_bf16, tm=512):
    t, d = x2.shape
    half = a.shape[1]
    tm = min(tm, t)
    return pl.pallas_call(
        _out_proj_kernel,
        out_shape=jax.ShapeDtypeStruct((t, d), F32),
        grid=(t // tm,),
        in_specs=[
            pl.BlockSpec((tm, d), lambda i: (i, 0)),
            pl.BlockSpec((tm, half), lambda i: (i, 0)),
            pl.BlockSpec((tm, half), lambda i: (i, 0)),
            pl.BlockSpec((2 * half, d), lambda i: (0, 0)),
        ],
        out_specs=pl.BlockSpec((tm, d), lambda i: (i, 0)),
        compiler_params=_cparams(("parallel",)),
        name="out_proj",
    )(x2, a, b, w_bf16)


def _swiglu_chunks(xn, wg_ref, wu_ref, h_scr, ff_chunk):
    d_ff = h_scr.shape[1]
    for c0 in range(0, d_ff, ff_chunk):
        g = jnp.dot(xn, wg_ref[:, c0:c0 + ff_chunk], preferred_element_type=F32)
        u = jnp.dot(xn, wu_ref[:, c0:c0 + ff_chunk], preferred_element_type=F32)
        h_scr[:, c0:c0 + ff_chunk] = (g * jax.nn.sigmoid(g) * u).astype(BF16)


def _ffn_kernel(x_ref, g_ref, wg_ref, wu_ref, wd_ref, o_ref, h_scr, *, ff_chunk):
    x = x_ref[...]
    xn = (x * lax.rsqrt(jnp.mean(x * x, axis=-1, keepdims=True) + EPS) * g_ref[...]).astype(BF16)
    _swiglu_chunks(xn, wg_ref, wu_ref, h_scr, ff_chunk)
    o_ref[...] = x + jnp.dot(h_scr[...], wd_ref[...], preferred_element_type=F32)


def ffn_residual(x2, g, wg, wu, wd, tm=512):
    t, d = x2.shape
    d_ff = wg.shape[1]
    tm = min(tm, t)
    ff_chunk = 256
    assert d_ff % ff_chunk == 0
    resident = lambda shape: pl.BlockSpec(shape, lambda i: (0, 0), pipeline_mode=pl.Buffered(1))
    return pl.pallas_call(
        functools.partial(_ffn_kernel, ff_chunk=ff_chunk),
        out_shape=jax.ShapeDtypeStruct((t, d), F32),
        grid=(t // tm,),
        in_specs=[
            pl.BlockSpec((tm, d), lambda i: (i, 0)),
            pl.BlockSpec((1, d), lambda i: (0, 0)),
            resident((d, d_ff)), resident((d, d_ff)), resident((d_ff, d)),
        ],
        out_specs=pl.BlockSpec((tm, d), lambda i: (i, 0)),
        scratch_shapes=[pltpu.VMEM((tm, d_ff), BF16)],
        compiler_params=_cparams(("parallel",)),
        name="ffn",
    )(x2, g.reshape(1, d), wg, wu, wd)


CONV_HALO = 32


def _conv_kernel(val_ref, gate_ref, pval_ref, pgate_ref, w_ref, b_ref, lg_ref, lb_ref, o_ref,
                 u_scr, ush_scr, c_scr, *, ts, tiles_per_seq, row_blk):
    i = pl.program_id(0)
    glu = lambda v, g: v.astype(F32) * jax.nn.sigmoid(g.astype(F32))
    keep = jnp.where(i % tiles_per_seq == 0, 0.0, 1.0)
    u_scr[0:CONV_HALO, :] = glu(pval_ref[...], pgate_ref[...]) * keep
    u_scr[CONV_HALO:, :] = glu(val_ref[...], gate_ref[...])
    shift = CONV_HALO - (CONV_WIDTH - 1)
    n_ch = val_ref.shape[1]
    n_sh = ush_scr.shape[1]
    for b in range(1, SUBLANES):
        ush_scr[b - 1] = u_scr[b:b + n_sh, :]
    for c0 in range(0, n_ch, LANES):
        for r0 in range(0, ts, row_blk):
            acc = jnp.broadcast_to(b_ref[:, c0:c0 + LANES], (row_blk, LANES))
            for j in range(CONV_WIDTH):
                a, b = divmod(shift + j, SUBLANES)
                src = u_scr if b == 0 else ush_scr.at[b - 1]
                r = r0 + a * SUBLANES
                acc = acc + w_ref[j:j + 1, c0:c0 + LANES] * src[r:r + row_blk, c0:c0 + LANES]
            c_scr[r0:r0 + row_blk, c0:c0 + LANES] = acc
    c = c_scr[...]
    mu = jnp.mean(c, axis=-1, keepdims=True)
    xc = c - mu
    var = jnp.mean(xc * xc, axis=-1, keepdims=True)
    y = xc * lax.rsqrt(var + EPS) * lg_ref[...] + lb_ref[...]
    o_ref[...] = (y * jax.nn.sigmoid(y)).astype(o_ref.dtype)


def conv_module(proj, conv_w, conv_b, ln_g, ln_b, seq, ts=256):
    t = proj.shape[0]
    ts = min(ts, seq)
    n_ch = C_CHANNELS
    halo_blocks = ts // CONV_HALO
    row = lambda a: a.reshape(1, n_ch).astype(F32)
    small = pl.BlockSpec((1, n_ch), lambda i: (0, 0))
    prev = lambda col: pl.BlockSpec((CONV_HALO, n_ch), lambda i: (jnp.maximum(i * halo_blocks - 1, 0), col))
    return pl.pallas_call(
        functools.partial(_conv_kernel, ts=ts, tiles_per_seq=seq // ts, row_blk=64),
        out_shape=jax.ShapeDtypeStruct((t, n_ch), BF16),
        grid=(t // ts,),
        in_specs=[
            pl.BlockSpec((ts, n_ch), lambda i: (i, 0)),
            pl.BlockSpec((ts, n_ch), lambda i: (i, 1)),
            prev(0), prev(1),
            pl.BlockSpec((CONV_WIDTH, n_ch), lambda i: (0, 0)),
            small, small, small,
        ],
        out_specs=pl.BlockSpec((ts, n_ch), lambda i: (i, 0)),
        scratch_shapes=[pltpu.VMEM((ts + CONV_HALO, n_ch), F32),
                        pltpu.VMEM((SUBLANES - 1, ts + CONV_HALO - SUBLANES, n_ch), F32),
                        pltpu.VMEM((ts, n_ch), F32)],
        compiler_params=_cparams(("parallel",)),
        name="conv_module",
    )(proj, proj, proj, proj, conv_w.astype(F32), row(conv_b), row(ln_g), row(ln_b))


def _rel_bias_kernel(tbl_ref, o_ref):
    p = pl.program_id(0)
    kk = lax.broadcasted_iota(I32, (BAND, LANES), 0)
    lane = lax.broadcasted_iota(I32, (BAND, LANES), 1)
    idx = jnp.clip(lane % CHUNK - (kk - LEFT_CHUNKS * CHUNK), -MAX_REL, MAX_REL) + MAX_REL
    first = lane < CHUNK

    def body(j, acc):
        return acc + jnp.where(idx == j, jnp.where(first, tbl_ref[2 * p, j], tbl_ref[2 * p + 1, j]), 0.0)

    o_ref[0] = lax.fori_loop(0, 2 * MAX_REL + 1, body, jnp.zeros((BAND, LANES), F32)) * LOG2E


def rel_bias(rel_table):
    pairs = D_HEADS // 2
    assert 2 * CHUNK == LANES
    return pl.pallas_call(
        _rel_bias_kernel,
        out_shape=jax.ShapeDtypeStruct((pairs, BAND, LANES), F32),
        grid=(pairs,),
        in_specs=[pl.BlockSpec(memory_space=pltpu.SMEM)],
        out_specs=pl.BlockSpec((1, BAND, LANES), lambda p: (p, 0, 0)),
        compiler_params=_cparams(("parallel",)),
        name="rel_bias",
    )(rel_table.astype(F32))


def _chunk_attn_kernel(q_ref, k_ref, v_ref, bias_ref, o_ref, kpad, vpad, *, group, seq):
    i = pl.program_id(2)
    pad = LEFT_CHUNKS * CHUNK

    @pl.when(i == 0)
    def _():
        kpad[0:pad, :] = jnp.zeros((pad, LANES), BF16)
        vpad[0:pad, :] = jnp.zeros((pad, LANES), BF16)
        kpad[pad:, :] = k_ref[...]
        vpad[pad:, :] = v_ref[...]

    lane = lax.broadcasted_iota(I32, (CHUNK, LANES), 1)
    band_chunk = lax.broadcasted_iota(I32, (BAND, LANES), 0) // CHUNK
    bias = bias_ref[0]
    ones = jnp.ones((BAND, LANES), BF16)
    for g in range(group):
        c = i * group + g
        qs = (q_ref[g * CHUNK:(g + 1) * CHUNK, :].astype(F32) * (HEAD_DIM ** -0.5 * LOG2E)).astype(BF16)
        zero = jnp.zeros_like(qs)
        q2 = jnp.concatenate([jnp.where(lane < HEAD_DIM, qs, zero),
                              jnp.where(lane >= HEAD_DIM, qs, zero)], axis=0)
        start = pl.multiple_of(c * CHUNK, CHUNK)
        kw = kpad[pl.ds(start, BAND), :]
        vw = vpad[pl.ds(start, BAND), :]
        st = _nt_dot(kw, q2) + bias
        st = jnp.where(band_chunk >= LEFT_CHUNKS - c, st, NEG)
        pt = jnp.exp2(st - jnp.max(st, axis=0, keepdims=True))
        o2 = lax.dot_general(pt.astype(BF16), jnp.concatenate([vw, ones], axis=1),
                             (((0,), (0,)), ((), ())), preferred_element_type=F32)
        o = o2[:, 0:LANES] / o2[:, LANES:]
        o_ref[g * CHUNK:(g + 1) * CHUNK, :] = jnp.where(lane < HEAD_DIM, o[0:CHUNK, :],
                                                        o[CHUNK:, :]).astype(o_ref.dtype)


def chunk_rel_attention(proj, bias, batch, seq, group=16):
    t = proj.shape[0]
    pairs = D_HEADS // 2
    q_off = (2 * C_CHANNELS) // LANES
    k_off = q_off + pairs
    v_off = k_off + pairs
    tq = group * CHUNK
    nq = seq // tq
    return pl.pallas_call(
        functools.partial(_chunk_attn_kernel, group=group, seq=seq),
        out_shape=jax.ShapeDtypeStruct((t, D_HEADS * HEAD_DIM), BF16),
        grid=(batch, pairs, nq),
        in_specs=[
            pl.BlockSpec((tq, LANES), lambda b, p, i: (b * nq + i, q_off + p)),
            pl.BlockSpec((seq, LANES), lambda b, p, i: (b, k_off + p)),
            pl.BlockSpec((seq, LANES), lambda b, p, i: (b, v_off + p)),
            pl.BlockSpec((1, BAND, LANES), lambda b, p, i: (p, 0, 0)),
        ],
        out_specs=pl.BlockSpec((tq, LANES), lambda b, p, i: (b * nq + i, p)),
        scratch_shapes=[pltpu.VMEM((seq + LEFT_CHUNKS * CHUNK, LANES), BF16),
                        pltpu.VMEM((seq + LEFT_CHUNKS * CHUNK, LANES), BF16)],
        compiler_params=_cparams(("parallel", "parallel", "arbitrary")),
        name="chunk_attn",
    )(proj, proj, proj, bias)


def _store_row_tiles(ref, val):
    n, d = val.shape
    sub = d // LANES
    for k in range(sub):
        ref[pl.ds(k, n, stride=sub), :] = val[:, k * LANES:(k + 1) * LANES]


def _load_row_tiles(ref, n, d):
    sub = d // LANES
    return jnp.concatenate([ref[pl.ds(k, n, stride=sub), :] for k in range(sub)], axis=1)


def _router_kernel(x_ref, g_ref, wr_ref, h_ref, meta_ref, cnt_ref, carry_scr, tri_scr, *, tr):
    i = pl.program_id(0)

    @pl.when(i == 0)
    def _():
        carry_scr[...] = jnp.zeros(carry_scr.shape, F32)
        r_i = lax.broadcasted_iota(I32, (tr, tr), 0)
        c_i = lax.broadcasted_iota(I32, (tr, tr), 1)
        tri_scr[...] = jnp.where(r_i < c_i, 1.0, 0.0).astype(BF16)

    x = x_ref[...]
    hn = x * lax.rsqrt(jnp.mean(x * x, axis=-1, keepdims=True) + EPS) * g_ref[...]
    _store_row_tiles(h_ref, hn)
    h_hi = hn.astype(BF16)
    h_lo = (hn - h_hi.astype(F32)).astype(BF16)
    w = wr_ref[...]
    w_hi = w.astype(BF16)
    w_lo = (w - w_hi.astype(F32)).astype(BF16)
    logits = _nt_dot(w_hi, h_hi) + (_nt_dot(w_hi, h_lo) + _nt_dot(w_lo, h_hi))
    eidx = lax.broadcasted_iota(I32, logits.shape, 0).astype(F32)
    m1 = jnp.max(logits, axis=0, keepdims=True)
    i1 = jnp.min(jnp.where(logits == m1, eidx, float(N_EXPERTS)), axis=0, keepdims=True)
    rest = jnp.where(eidx == i1, -jnp.inf, logits)
    m2 = jnp.max(rest, axis=0, keepdims=True)
    i2 = jnp.min(jnp.where(rest == m2, eidx, float(N_EXPERTS)), axis=0, keepdims=True)
    e = jnp.exp(m2 - m1)
    g1 = 1.0 / (1.0 + e)
    g2 = e / (1.0 + e)
    sel1 = eidx == i1
    sel2 = eidx == i2
    onehot = jnp.where(sel1 | sel2, 1.0, 0.0)
    rank = jnp.dot(onehot.astype(BF16), tri_scr[...], preferred_element_type=F32) + carry_scr[...]
    r1 = jnp.sum(jnp.where(sel1, rank, 0.0), axis=0, keepdims=True)
    r2 = jnp.sum(jnp.where(sel2, rank, 0.0), axis=0, keepdims=True)
    carry = carry_scr[...] + jnp.sum(onehot, axis=1, keepdims=True)
    carry_scr[...] = carry
    zero = jnp.zeros_like(g1)
    meta_ref[0] = jnp.concatenate([i1, i2, r1, r2, g1, g2, zero, zero], axis=0)
    cnt_ref[...] = jnp.broadcast_to(carry, cnt_ref.shape)


def router(x2, g, w_router, tr=512):
    t, d = x2.shape
    tr = min(tr, t)
    sub = d // LANES
    return pl.pallas_call(
        functools.partial(_router_kernel, tr=tr),
        out_shape=(jax.ShapeDtypeStruct((t * sub, LANES), F32),
                   jax.ShapeDtypeStruct((t // tr, 8, tr), F32),
                   jax.ShapeDtypeStruct((N_EXPERTS, LANES), F32)),
        grid=(t // tr,),
        in_specs=[
            pl.BlockSpec((tr, d), lambda i: (i, 0)),
            pl.BlockSpec((1, d), lambda i: (0, 0)),
            pl.BlockSpec((N_EXPERTS, d), lambda i: (0, 0)),
        ],
        out_specs=(pl.BlockSpec((tr * sub, LANES), lambda i: (i, 0)),
                   pl.BlockSpec((1, 8, tr), lambda i: (i, 0, 0)),
                   pl.BlockSpec((N_EXPERTS, LANES), lambda i: (0, 0))),
        scratch_shapes=[pltpu.VMEM((N_EXPERTS, 1), F32), pltpu.VMEM((tr, tr), BF16)],
        compiler_params=_cparams(("arbitrary",)),
        name="router",
    )(x2, g.reshape(1, d), w_router.T.astype(F32))


ROW_DMA_UNROLL = 8


def _rows(ref, row, sub):
    return ref.at[pl.ds(pl.multiple_of(row * sub, sub), sub)]


def _scatter_rows_kernel(ends_ref, idx1_ref, idx2_ref, src_ref, dst_ref, zero_scr, sem, *, rows, sub, pad_rows):
    @pl.when(pl.program_id(0) == 0)
    def _():
        zero_scr[...] = jnp.zeros(zero_scr.shape, zero_scr.dtype)

        def tail(e):
            start = jnp.maximum(ends_ref[e] - pad_rows, 0)
            return pltpu.make_async_copy(zero_scr, dst_ref.at[pl.ds(pl.multiple_of(start * sub, sub),
                                                                  pad_rows * sub)], sem)

        n_exp = ends_ref.shape[0]
        for e in range(n_exp):
            tail(e).start()
        for e in range(n_exp):
            tail(e).wait()
        n_rows = dst_ref.shape[0] // sub
        for t in range(n_exp):
            start = ends_ref[n_exp - 1] + t * pad_rows

            @pl.when(start < n_rows)
            def _():
                spare = pltpu.make_async_copy(
                    zero_scr, dst_ref.at[pl.ds(pl.multiple_of(start * sub, sub), pad_rows * sub)], sem)
                spare.start()
                spare.wait()

    def issue(r, carry):
        pltpu.make_async_copy(_rows(src_ref, r, sub), _rows(dst_ref, idx1_ref[0, 0, r], sub), sem).start()
        pltpu.make_async_copy(_rows(src_ref, r, sub), _rows(dst_ref, idx2_ref[0, 0, r], sub), sem).start(priority=1)
        return carry

    lax.fori_loop(0, rows, issue, 0, unroll=ROW_DMA_UNROLL)
    for _ in range(2):
        pltpu.make_async_copy(src_ref, dst_ref.at[pl.ds(0, rows * sub)], sem).wait()


def scatter_rows(src, idx1, idx2, ends, n_rows, pad_rows, sub, rows=512):
    n = idx1.shape[0]
    rows = min(rows, n)
    assert n % rows == 0 and rows % ROW_DMA_UNROLL == 0 and n_rows >= max(rows, pad_rows)
    idx_spec = pl.BlockSpec((1, 1, rows), lambda i, ends: (i, 0, 0), memory_space=pltpu.SMEM)
    grid_spec = pltpu.PrefetchScalarGridSpec(
        num_scalar_prefetch=1,
        grid=(n // rows,),
        in_specs=[idx_spec, idx_spec, pl.BlockSpec((rows * sub, LANES), lambda i, ends: (i, 0))],
        out_specs=pl.BlockSpec(memory_space=pl.ANY),
        scratch_shapes=[pltpu.VMEM((pad_rows * sub, LANES), src.dtype), pltpu.SemaphoreType.DMA(())],
    )
    return pl.pallas_call(
        functools.partial(_scatter_rows_kernel, rows=rows, sub=sub, pad_rows=pad_rows),
        out_shape=jax.ShapeDtypeStruct((n_rows * sub, LANES), src.dtype),
        grid_spec=grid_spec,
        compiler_params=_cparams(("arbitrary",)),
        name="row_scatter",
    )(ends, idx1.reshape(n // rows, 1, rows), idx2.reshape(n // rows, 1, rows), src)


def _expert_ffn_kernel(te_ref, nt_ref, x_ref, wg_ref, wu_ref, wd_ref, o_ref, xn_scr, h_scr, acc_scr, *, ff_chunk):
    n = pl.program_id(0)
    f = pl.program_id(1)
    tm, d = acc_scr.shape

    @pl.when(n < nt_ref[0])
    def _():
        @pl.when(f == 0)
        def _():
            xn_scr[...] = _load_row_tiles(x_ref, tm, d).astype(BF16)

        _swiglu_chunks(xn_scr[...], wg_ref.at[0], wu_ref.at[0], h_scr, ff_chunk)
        part = jnp.dot(h_scr[...], wd_ref[0], preferred_element_type=F32)

        @pl.when(f == 0)
        def _():
            acc_scr[...] = part

        @pl.when(f > 0)
        def _():
            acc_scr[...] += part

        @pl.when(f == pl.num_programs(1) - 1)
        def _():
            _store_row_tiles(o_ref, acc_scr[...])

    @pl.when((n >= nt_ref[0]) & (f == pl.num_programs(1) - 1))
    def _():
        o_ref[...] = jnp.zeros(o_ref.shape, o_ref.dtype)


def expert_ffn(xs, tile_expert, n_tiles_used, wg, wu, wd, tm, ff_split=2):
    d, d_ff = wg.shape[1], wg.shape[2]
    sub = d // LANES
    nr = xs.shape[0] // sub
    ff_blk = d_ff // ff_split
    ff_chunk = 256
    assert nr % tm == 0 and ff_blk % ff_chunk == 0
    grid_spec = pltpu.PrefetchScalarGridSpec(
        num_scalar_prefetch=2,
        grid=(nr // tm, ff_split),
        in_specs=[
            pl.BlockSpec((tm * sub, LANES), lambda n, f, te, nt: (jnp.minimum(n, nt[0] - 1), 0)),
            pl.BlockSpec((1, d, ff_blk), lambda n, f, te, nt: (te[n], 0, f)),
            pl.BlockSpec((1, d, ff_blk), lambda n, f, te, nt: (te[n], 0, f)),
            pl.BlockSpec((1, ff_blk, d), lambda n, f, te, nt: (te[n], f, 0)),
        ],
        out_specs=pl.BlockSpec((tm * sub, LANES), lambda n, f, te, nt: (n, 0)),
        scratch_shapes=[pltpu.VMEM((tm, d), BF16), pltpu.VMEM((tm, ff_blk), BF16), pltpu.VMEM((tm, d), F32)],
    )
    return pl.pallas_call(
        functools.partial(_expert_ffn_kernel, ff_chunk=ff_chunk),
        out_shape=jax.ShapeDtypeStruct((nr * sub, LANES), F32),
        grid_spec=grid_spec,
        compiler_params=_cparams(("arbitrary", "arbitrary")),
        name="expert_ffn",
    )(tile_expert, n_tiles_used, xs, wg, wu, wd)


def _combine_kernel(idx1_ref, idx2_ref, x_ref, ys_ref, g1_ref, g2_ref, gn_ref, o_ref, y1_buf, y2_buf, sem, *,
                    final_norm):
    tm, d = x_ref.shape
    sub = d // LANES

    def issue(r, carry):
        pltpu.make_async_copy(_rows(ys_ref, idx1_ref[0, 0, r], sub), _rows(y1_buf, r, sub), sem).start()
        pltpu.make_async_copy(_rows(ys_ref, idx2_ref[0, 0, r], sub), _rows(y2_buf, r, sub), sem).start(priority=1)
        return carry

    lax.fori_loop(0, tm, issue, 0, unroll=ROW_DMA_UNROLL)
    for buf in (y1_buf, y2_buf):
        pltpu.make_async_copy(ys_ref.at[pl.ds(0, tm * sub)], buf, sem).wait()
    y = (x_ref[...] + g1_ref[...] * _load_row_tiles(y1_buf, tm, d)
         + g2_ref[...] * _load_row_tiles(y2_buf, tm, d))
    if final_norm:
        y = y * lax.rsqrt(jnp.mean(y * y, axis=-1, keepdims=True) + EPS) * gn_ref[...]
    o_ref[...] = y


def combine(x2, ys, idx1, idx2, g1, g2, gn, final_norm, tm=512):
    t, d = x2.shape
    tm = min(tm, t)
    sub = d // LANES
    assert t % tm == 0 and tm % ROW_DMA_UNROLL == 0 and ys.shape[0] >= tm * sub
    big = pl.BlockSpec((tm, d), lambda i: (i, 0))
    col = pl.BlockSpec((tm, 1), lambda i: (i, 0))
    idx_spec = pl.BlockSpec((1, 1, tm), lambda i: (i, 0, 0), memory_space=pltpu.SMEM)
    return pl.pallas_call(
        functools.partial(_combine_kernel, final_norm=final_norm),
        out_shape=jax.ShapeDtypeStruct((t, d), F32),
        grid=(t // tm,),
        in_specs=[idx_spec, idx_spec, big, pl.BlockSpec(memory_space=pl.ANY), col, col,
                  pl.BlockSpec((1, d), lambda i: (0, 0))],
        out_specs=big,
        scratch_shapes=[pltpu.VMEM((tm * sub, LANES), F32), pltpu.VMEM((tm * sub, LANES), F32),
                        pltpu.SemaphoreType.DMA(())],
        compiler_params=_cparams(("arbitrary",)),
        name="moe_combine",
    )(idx1.reshape(t // tm, 1, tm), idx2.reshape(t // tm, 1, tm), x2, ys, g1, g2, gn.reshape(1, d))


def moe_residual(x2, g, w_router, wg, wu, wd, expert_base, final_gain, final_norm, tm=512):
    t, d = x2.shape
    tm = min(tm, t)
    hn, meta, counts = router(x2, g, w_router)
    tr = meta.shape[2]
    field = lambda r: meta[:, r, :].reshape(t)
    idx1, idx2 = field(0).astype(I32), field(1).astype(I32)
    rank1, rank2 = field(2).astype(I32), field(3).astype(I32)
    gate1, gate2 = field(4), field(5)
    cnt = counts[:, 0].astype(I32)
    padded = ((cnt + tm - 1) // tm) * tm
    ends = jnp.cumsum(padded)
    offs = ends - padded
    pos1 = offs[idx1] + rank1
    pos2 = offs[idx2] + rank2
    n_rows = TOP_K * t + N_EXPERTS * tm
    tile_start = jnp.arange(n_rows // tm, dtype=I32) * tm
    tile_expert = jnp.minimum(jnp.sum(tile_start[:, None] >= ends[None, :], axis=1), N_EXPERTS - 1).astype(I32)
    tile_expert = tile_expert + expert_base
    n_tiles_used = (ends[-1:] // tm).astype(I32)

    sub = d // LANES
    xs = scatter_rows(hn, pos1, pos2, ends.astype(I32), n_rows, tm, sub)
    ys = expert_ffn(xs, tile_expert, n_tiles_used, wg, wu, wd, tm)
    return combine(x2, ys, pos1, pos2, gate1.reshape(t, 1), gate2.reshape(t, 1), final_gain, final_norm)


def kernel(x, ev_norm_mix, ev_w_in, ev_lambda_q1, ev_lambda_k1, ev_lambda_q2, ev_lambda_k2, ev_subln, ev_w_out,
           ev_norm_ffn, ev_ffn_gate, ev_ffn_up, ev_ffn_down, od_norm_mix, od_w_in, od_conv_w, od_conv_b,
           od_conv_ln_g, od_conv_ln_b, od_rel_bias, od_w_out, od_norm_ffn, od_router, od_exp_gate, od_exp_up,
           od_exp_down, final_norm):
    batch, seq, d = x.shape
    depth = 2 * ev_w_in.shape[0]
    assert od_w_in.shape[0] * 2 == depth
    x2 = x.reshape(batch * seq, d)
    bf = lambda a: a.astype(BF16)
    a_qk = A_HEADS * 2 * HEAD_DIM
    a_v = A_HEADS * 2 * HEAD_DIM
    b_w = B_HEADS * HEAD_DIM
    attn_k_block = 256
    all_experts = lambda w: bf(w).reshape((-1,) + w.shape[2:])
    exp_gate, exp_up, exp_down = all_experts(od_exp_gate), all_experts(od_exp_up), all_experts(od_exp_down)
    for layer in range(depth):
        i = layer // 2
        if layer % 2 == 0:
            lambda_init = 0.8 - 0.6 * math.exp(-0.3 * layer)
            w = ev_w_in[i]
            va0, qb0 = 2 * a_qk, 2 * a_qk + a_v
            vb0 = qb0 + 2 * b_w
            w_qk = bf(jnp.concatenate([w[:, 0:va0], w[:, qb0:vb0]], axis=1))
            w_vt = bf(jnp.concatenate([w[:, va0:qb0], w[:, vb0:]], axis=1).T)
            proj, vt = norm_proj(x2, ev_norm_mix[i], w_qk, seq, rope_cols=2 * a_qk, wvt_bf16=w_vt,
                                 tk=attn_k_block)
            oa = diff_attention(proj, vt, ev_lambda_q1[i], ev_lambda_k1[i], ev_lambda_q2[i], ev_lambda_k2[i],
                                ev_subln[i], batch, seq, lambda_init)
            ob = stick_breaking(proj, vt, batch, seq)
            x2 = out_proj_residual(x2, oa, ob, bf(ev_w_out[i]))
            x2 = ffn_residual(x2, ev_norm_ffn[i], bf(ev_ffn_gate[i]), bf(ev_ffn_up[i]), bf(ev_ffn_down[i]))
        else:
            proj = norm_proj(x2, od_norm_mix[i], bf(od_w_in[i]), seq)
            c = conv_module(proj, od_conv_w[i], od_conv_b[i], od_conv_ln_g[i], od_conv_ln_b[i], seq)
            od = chunk_rel_attention(proj, rel_bias(od_rel_bias[i]), batch, seq)
            x2 = out_proj_residual(x2, c, od, bf(od_w_out[i]))
            x2 = moe_residual(x2, od_norm_ffn[i], od_router[i], exp_gate, exp_up, exp_down, i * N_EXPERTS,
                              final_norm, final_norm=(layer == depth - 1))
    return x2.reshape(batch, seq, d)
```

```python
import functools
import math

import numpy as np
import jax
import jax.numpy as jnp
from jax import lax
from jax.experimental import pallas as pl
from jax.experimental.pallas import tpu as pltpu

F32 = jnp.float32
BF16 = jnp.bfloat16
I32 = jnp.int32

HEAD_DIM = 64
CHUNK = 64
ROPE_THETA = 10000.0
EPS = 1e-6
A_HEADS = 4
B_HEADS = 8
C_CHANNELS = 512
CONV_WIDTH = 31
D_HEADS = 8
LEFT_CHUNKS = 8
BAND = (LEFT_CHUNKS + 1) * CHUNK
MAX_REL = 128
N_EXPERTS = 8
TOP_K = 2

LANES = 128
SUBLANES = 8
NEG = -1e30
LOG2E = math.log2(math.e)
SIGN_BIT = np.int32(-2 ** 31)
VMEM_LIMIT = 56 * 1024 * 1024


def _cparams(sem):
    return pltpu.CompilerParams(dimension_semantics=sem, vmem_limit_bytes=VMEM_LIMIT)


def _nt_dot(a, b):
    return lax.dot_general(a, b, (((1,), (1,)), ((), ())), preferred_element_type=F32)


def _norm_proj_kernel(x_ref, g_ref, w_ref, cos_ref, sin_ref, *rest, col_chunk, rope_cols, tk):
    if tk:
        wvt_ref, o_ref, vt_ref = rest
    else:
        (o_ref,) = rest
    x = x_ref[...]
    xn = (x * lax.rsqrt(jnp.mean(x * x, axis=-1, keepdims=True) + EPS) * g_ref[...]).astype(BF16)
    n_out = o_ref.shape[1]
    if tk:
        for r0 in range(0, wvt_ref.shape[0], col_chunk):
            res = _nt_dot(wvt_ref[r0:r0 + col_chunk, :], xn)
            for b in range(vt_ref.shape[0]):
                vt_ref[b, r0:r0 + col_chunk, :] = res[:, b * tk:(b + 1) * tk].astype(BF16)
    if rope_cols:
        cos = cos_ref[...]
        sin = sin_ref[...]
        lane = lax.broadcasted_iota(I32, cos.shape, 1)
        first_half = (lane % HEAD_DIM) < (HEAD_DIM // 2)
    for c0 in range(0, n_out, col_chunk):
        r = jnp.dot(xn, w_ref[:, c0:c0 + col_chunk], preferred_element_type=F32)
        if c0 < rope_cols:
            parts = []
            for l0 in range(0, col_chunk, LANES):
                seg = r[:, l0:l0 + LANES]
                partner = jnp.where(first_half,
                                    pltpu.roll(seg, LANES - HEAD_DIM // 2, 1),
                                    pltpu.roll(seg, HEAD_DIM // 2, 1))
                parts.append(seg * cos + partner * sin)
            r = jnp.concatenate(parts, axis=1)
        o_ref[:, c0:c0 + col_chunk] = r.astype(o_ref.dtype)


def _rope_tables(seq):
    half = HEAD_DIM // 2
    inv_freq = ROPE_THETA ** (-jnp.arange(half, dtype=F32) * 2.0 / HEAD_DIM)
    ang = jnp.arange(seq, dtype=F32)[:, None] * inv_freq[None, :]
    cos, sin = jnp.cos(ang), jnp.sin(ang)
    cos_t = jnp.tile(jnp.concatenate([cos, cos], axis=1), (1, LANES // HEAD_DIM))
    sin_t = jnp.tile(jnp.concatenate([-sin, sin], axis=1), (1, LANES // HEAD_DIM))
    return cos_t, sin_t


def norm_proj(x2, g, w_bf16, seq, rope_cols=0, wvt_bf16=None, tk=0, tm=512):
    t, d = x2.shape
    n_out = w_bf16.shape[1]
    tm = min(tm, seq)
    col_chunk = 512
    assert t % tm == 0 and seq % tm == 0 and n_out % col_chunk == 0 and rope_cols % col_chunk == 0
    cos_t, sin_t = _rope_tables(seq)
    tiles_per_seq = seq // tm
    in_specs = [
        pl.BlockSpec((tm, d), lambda i: (i, 0)),
        pl.BlockSpec((1, d), lambda i: (0, 0)),
        pl.BlockSpec((d, n_out), lambda i: (0, 0)),
        pl.BlockSpec((tm, LANES), lambda i: (i % tiles_per_seq, 0)),
        pl.BlockSpec((tm, LANES), lambda i: (i % tiles_per_seq, 0)),
    ]
    args = [x2, g.reshape(1, d), w_bf16, cos_t, sin_t]
    out_shape = jax.ShapeDtypeStruct((t, n_out), BF16)
    out_specs = pl.BlockSpec((tm, n_out), lambda i: (i, 0))
    if wvt_bf16 is not None:
        n_v = wvt_bf16.shape[0]
        tk = min(tk, tm)
        assert tm % tk == 0 and n_v % col_chunk == 0
        in_specs.append(pl.BlockSpec((n_v, d), lambda i: (0, 0)))
        args.append(wvt_bf16)
        out_shape = (out_shape, jax.ShapeDtypeStruct((t // tk, n_v, tk), BF16))
        out_specs = (out_specs, pl.BlockSpec((tm // tk, n_v, tk), lambda i: (i, 0, 0)))
    return pl.pallas_call(
        functools.partial(_norm_proj_kernel, col_chunk=col_chunk, rope_cols=rope_cols,
                          tk=tk if wvt_bf16 is not None else 0),
        out_shape=out_shape,
        grid=(t // tm,),
        in_specs=in_specs,
        out_specs=out_specs,
        compiler_params=_cparams(("parallel",)),
        name="norm_proj",
    )(*args)


def _stack_query_pair(q_ref, q2_scr, tq):
    lane = lax.broadcasted_iota(I32, (tq, LANES), 1)
    qs = (q_ref[...].astype(F32) * (HEAD_DIM ** -0.5 * LOG2E)).astype(BF16)
    zero = jnp.zeros_like(qs)
    q2_scr[0:tq, :] = jnp.where(lane < HEAD_DIM, qs, zero)
    q2_scr[tq:, :] = jnp.where(lane >= HEAD_DIM, qs, zero)


def _diag_visible(tq, tk, rule):
    key = np.arange(tq)[:, None]
    qry = (np.arange(2 * tq) % tq)[None, :]
    return rule(key, qry).reshape(tq // tk, tk, 2 * tq)


def _diff_attn_kernel(q_ref, k_ref, vt_ref, mask_ref, lq1_ref, lk1_ref, lq2_ref, lk2_ref, g_ref, o_ref,
                      q2_scr, m_scr, l_scr, acc_scr, st0_scr, st1_scr, *, tq, tk, lambda_init):
    i = pl.program_id(2)
    assert tq == 2 * tk
    _stack_query_pair(q_ref, q2_scr, tq)
    m_scr[...] = jnp.full(m_scr.shape, NEG, F32)
    l_scr[...] = jnp.zeros(l_scr.shape, F32)
    late_queries = (slice(tk, tq), slice(tq + tk, 2 * tq))
    acc_scr[...] = jnp.zeros(acc_scr.shape, F32)

    def scores(j, st_scr):
        k = k_ref[pl.ds(pl.multiple_of(j * tk, tk), tk), :]
        st_scr[...] = _nt_dot(k, q2_scr[...])

    def consume(j, st_scr, diag=None, cols=(slice(None),)):
        for c in cols:
            st = st_scr[:, c]
            if diag is not None:
                st = st + mask_ref[diag, :, c]
            m_old = m_scr[:, c]
            m_new = jnp.maximum(m_old, jnp.max(st, axis=0, keepdims=True))
            alpha = jnp.exp2(m_old - m_new)
            pt = jnp.exp2(st - m_new)
            l_scr[:, c] = alpha * l_scr[:, c] + jnp.sum(pt, axis=0, keepdims=True)
            acc_scr[:, c] = alpha * acc_scr[:, c] + jnp.dot(vt_ref[j], pt.astype(BF16),
                                                            preferred_element_type=F32)
            m_scr[:, c] = m_new

    scores(0, st0_scr)

    def body(n, carry):
        scores(2 * n + 1, st1_scr)
        consume(2 * n, st0_scr)
        scores(2 * n + 2, st0_scr)
        consume(2 * n + 1, st1_scr)
        return carry

    lax.fori_loop(0, i, body, 0)
    scores(2 * i + 1, st1_scr)
    consume(2 * i, st0_scr, diag=0)
    consume(2 * i + 1, st1_scr, diag=1, cols=late_queries)

    ot = acc_scr[...] / l_scr[...]
    lam = (jnp.exp(jnp.sum(lq1_ref[...] * lk1_ref[...], axis=1, keepdims=True))
           - jnp.exp(jnp.sum(lq2_ref[...] * lk2_ref[...], axis=1, keepdims=True)) + lambda_init)
    od = ot[:, 0:tq].T - lam * ot[:, tq:].T
    y = od * lax.rsqrt(jnp.mean(od * od, axis=-1, keepdims=True) + EPS) * g_ref[...]
    o_ref[...] = (y * (1.0 - lambda_init)).astype(o_ref.dtype)


def diff_attention(proj, vt, lq1, lk1, lq2, lk2, subln_g, batch, seq, lambda_init):
    t = proj.shape[0]
    tk = vt.shape[2]
    tq = 2 * tk
    nq = seq // tq
    assert seq % tq == 0 and tk % CHUNK == 0
    k_off = (A_HEADS * 2 * HEAD_DIM) // LANES
    vec = lambda a: a.reshape(1, -1).astype(F32)
    small = lambda n: pl.BlockSpec((1, n), lambda b, h, i: (0, 0))
    mask = jnp.asarray(np.where(_diag_visible(tq, tk, lambda key, qry: key // CHUNK <= qry // CHUNK), 0.0, NEG), F32)
    return pl.pallas_call(
        functools.partial(_diff_attn_kernel, tq=tq, tk=tk, lambda_init=lambda_init),
        out_shape=jax.ShapeDtypeStruct((t, A_HEADS * 2 * HEAD_DIM), BF16),
        grid=(batch, A_HEADS, nq),
        in_specs=[
            pl.BlockSpec((tq, LANES), lambda b, h, i: (b * nq + i, h)),
            pl.BlockSpec((seq, LANES), lambda b, h, i: (b, k_off + h)),
            pl.BlockSpec((seq // tk, LANES, tk), lambda b, h, i: (b, h, 0)),
            pl.BlockSpec((tq // tk, tk, 2 * tq), lambda b, h, i: (0, 0, 0)),
            small(HEAD_DIM), small(HEAD_DIM), small(HEAD_DIM), small(HEAD_DIM), small(2 * HEAD_DIM),
        ],
        out_specs=pl.BlockSpec((tq, LANES), lambda b, h, i: (b * nq + i, h)),
        scratch_shapes=[
            pltpu.VMEM((2 * tq, LANES), BF16),
            pltpu.VMEM((1, 2 * tq), F32),
            pltpu.VMEM((1, 2 * tq), F32),
            pltpu.VMEM((LANES, 2 * tq), F32),
            pltpu.VMEM((tk, 2 * tq), F32),
            pltpu.VMEM((tk, 2 * tq), F32),
        ],
        compiler_params=_cparams(("parallel", "parallel", "arbitrary")),
        name="diff_attn",
    )(proj, proj, vt, mask, vec(lq1), vec(lk1), vec(lq2), vec(lk2), vec(subln_g))


def _stick_kernel(q_ref, k_ref, vt_ref, mask_ref, o_ref, q2_scr, tri_scr, c_scr, acc_scr, st0_scr, st1_scr, *, tq, tk):
    i = pl.program_id(2)
    assert tq == 2 * tk
    _stack_query_pair(q_ref, q2_scr, tq)
    r_i = lax.broadcasted_iota(I32, (tk, tk), 0)
    c_i = lax.broadcasted_iota(I32, (tk, tk), 1)
    tri_scr[...] = jnp.where(c_i > r_i, 1.0, 0.0).astype(BF16)
    c_scr[...] = jnp.zeros(c_scr.shape, F32)
    acc_scr[...] = jnp.zeros(acc_scr.shape, F32)

    def scores(j, st_scr):
        k = k_ref[pl.ds(pl.multiple_of(j * tk, tk), tk), :]
        st_scr[...] = _nt_dot(k, q2_scr[...])

    def consume(j, st_scr, diag=None, cols=(slice(None),)):
        for c in cols:
            zt = st_scr[:, c]
            neg_abs = lax.bitcast_convert_type(lax.bitcast_convert_type(zt, I32) | SIGN_BIT, F32)
            sp = jnp.maximum(zt, 0.0) + jnp.log2(1.0 + jnp.exp2(neg_abs))
            if diag is not None:
                visible = mask_ref[diag, :, c]
                spm = sp * visible
            else:
                spm = sp
            after = jnp.dot(tri_scr[...], spm.astype(BF16), preferred_element_type=F32)
            w = jnp.exp2(zt - sp - after - c_scr[:, c])
            if diag is not None:
                w = w * visible
            acc_scr[:, c] += jnp.dot(vt_ref[j], w.astype(BF16), preferred_element_type=F32)
            c_scr[:, c] += jnp.sum(spm, axis=0, keepdims=True)

    scores(2 * i + 1, st0_scr)
    scores(2 * i, st1_scr)
    consume(2 * i + 1, st0_scr, diag=1, cols=(slice(tk, tq), slice(tq + tk, 2 * tq)))
    scores(jnp.maximum(2 * i - 1, 0), st0_scr)
    consume(2 * i, st1_scr, diag=0)

    def body(n, carry):
        j = 2 * (i - n) - 1
        scores(j - 1, st1_scr)
        consume(j, st0_scr)
        scores(jnp.maximum(j - 2, 0), st0_scr)
        consume(j - 1, st1_scr)
        return carry

    lax.fori_loop(0, i, body, 0)
    acc = acc_scr[...]
    vrow = lax.broadcasted_iota(I32, (LANES, tq), 0)
    ot = jnp.where(vrow < HEAD_DIM, acc[:, 0:tq], acc[:, tq:])
    o_ref[...] = ot.T.astype(o_ref.dtype)


def stick_breaking(proj, vt, batch, seq):
    t = proj.shape[0]
    tk = vt.shape[2]
    tq = 2 * tk
    nq = seq // tq
    assert seq % tq == 0
    pairs = (B_HEADS * HEAD_DIM) // LANES
    a_blocks = (A_HEADS * 2 * HEAD_DIM) // LANES
    q_off = 2 * a_blocks
    k_off = q_off + pairs
    v_off = a_blocks
    mask = jnp.asarray(_diag_visible(tq, tk, lambda key, qry: key < qry), F32)
    return pl.pallas_call(
        functools.partial(_stick_kernel, tq=tq, tk=tk),
        out_shape=jax.ShapeDtypeStruct((t, B_HEADS * HEAD_DIM), BF16),
        grid=(batch, pairs, nq),
        in_specs=[
            pl.BlockSpec((tq, LANES), lambda b, p, i: (b * nq + i, q_off + p)),
            pl.BlockSpec((seq, LANES), lambda b, p, i: (b, k_off + p)),
            pl.BlockSpec((seq // tk, LANES, tk), lambda b, p, i: (b, v_off + p, 0)),
            pl.BlockSpec((tq // tk, tk, 2 * tq), lambda b, p, i: (0, 0, 0)),
        ],
        out_specs=pl.BlockSpec((tq, LANES), lambda b, p, i: (b * nq + i, p)),
        scratch_shapes=[
            pltpu.VMEM((2 * tq, LANES), BF16),
            pltpu.VMEM((tk, tk), BF16),
            pltpu.VMEM((1, 2 * tq), F32),
            pltpu.VMEM((LANES, 2 * tq), F32),
            pltpu.VMEM((tk, 2 * tq), F32),
            pltpu.VMEM((tk, 2 * tq), F32),
        ],
        compiler_params=_cparams(("parallel", "parallel", "arbitrary")),
        name="stick_attn",
    )(proj, proj, vt, mask)


def _out_proj_kernel(x_ref, a_ref, b_ref, w_ref, o_ref):
    half = a_ref.shape[1]
    o_ref[...] = (x_ref[...]
                  + jnp.dot(a_ref[...], w_ref[0:half, :], preferred_element_type=F32)
                  + jnp.dot(b_ref[...], w_ref[half:, :], preferred_element_type=F32))


def out_proj_residual(x2, a, b, w_bf16, tm=512):
    t, d = x2.shape
    half = a.shape[1]
    tm = min(tm, t)
    return pl.pallas_call(
        _out_proj_kernel,
        out_shape=jax.ShapeDtypeStruct((t, d), F32),
        grid=(t // tm,),
        in_specs=[
            pl.BlockSpec((tm, d), lambda i: (i, 0)),
            pl.BlockSpec((tm, half), lambda i: (i, 0)),
            pl.BlockSpec((tm, half), lambda i: (i, 0)),
            pl.BlockSpec((2 * half, d), lambda i: (0, 0)),
        ],
        out_specs=pl.BlockSpec((tm, d), lambda i: (i, 0)),
        compiler_params=_cparams(("parallel",)),
        name="out_proj",
    )(x2, a, b, w_bf16)


def _swiglu_chunks(xn, wg_ref, wu_ref, h_scr, ff_chunk):
    d_ff = h_scr.shape[1]
    for c0 in range(0, d_ff, ff_chunk):
        g = jnp.dot(xn, wg_ref[:, c0:c0 + ff_chunk], preferred_element_type=F32)
        u = jnp.dot(xn, wu_ref[:, c0:c0 + ff_chunk], preferred_element_type=F32)
        h_scr[:, c0:c0 + ff_chunk] = (g * jax.nn.sigmoid(g) * u).astype(BF16)


def _ffn_kernel(x_ref, g_ref, wg_ref, wu_ref, wd_ref, o_ref, h_scr, *, ff_chunk):
    x = x_ref[...]
    xn = (x * lax.rsqrt(jnp.mean(x * x, axis=-1, keepdims=True) + EPS) * g_ref[...]).astype(BF16)
    _swiglu_chunks(xn, wg_ref, wu_ref, h_scr, ff_chunk)
    o_ref[...] = x + jnp.dot(h_scr[...], wd_ref[...], preferred_element_type=F32)


def ffn_residual(x2, g, wg, wu, wd, tm=512):
    t, d = x2.shape
    d_ff = wg.shape[1]
    tm = min(tm, t)
    ff_chunk = 256
    assert d_ff % ff_chunk == 0
    resident = lambda shape: pl.BlockSpec(shape, lambda i: (0, 0), pipeline_mode=pl.Buffered(1))
    return pl.pallas_call(
        functools.partial(_ffn_kernel, ff_chunk=ff_chunk),
        out_shape=jax.ShapeDtypeStruct((t, d), F32),
        grid=(t // tm,),
        in_specs=[
            pl.BlockSpec((tm, d), lambda i: (i, 0)),
            pl.BlockSpec((1, d), lambda i: (0, 0)),
            resident((d, d_ff)), resident((d, d_ff)), resident((d_ff, d)),
        ],
        out_specs=pl.BlockSpec((tm, d), lambda i: (i, 0)),
        scratch_shapes=[pltpu.VMEM((tm, d_ff), BF16)],
        compiler_params=_cparams(("parallel",)),
        name="ffn",
    )(x2, g.reshape(1, d), wg, wu, wd)


CONV_HALO = 32


def _conv_kernel(val_ref, gate_ref, pval_ref, pgate_ref, w_ref, b_ref, lg_ref, lb_ref, o_ref,
                 u_scr, ush_scr, c_scr, *, ts, tiles_per_seq, row_blk):
    i = pl.program_id(0)
    glu = lambda v, g: v.astype(F32) * jax.nn.sigmoid(g.astype(F32))
    keep = jnp.where(i % tiles_per_seq == 0, 0.0, 1.0)
    u_scr[0:CONV_HALO, :] = glu(pval_ref[...], pgate_ref[...]) * keep
    u_scr[CONV_HALO:, :] = glu(val_ref[...], gate_ref[...])
    shift = CONV_HALO - (CONV_WIDTH - 1)
    n_ch = val_ref.shape[1]
    n_sh = ush_scr.shape[1]
    for b in range(1, SUBLANES):
        ush_scr[b - 1] = u_scr[b:b + n_sh, :]
    for c0 in range(0, n_ch, LANES):
        for r0 in range(0, ts, row_blk):
            acc = jnp.broadcast_to(b_ref[:, c0:c0 + LANES], (row_blk, LANES))
            for j in range(CONV_WIDTH):
                a, b = divmod(shift + j, SUBLANES)
                src = u_scr if b == 0 else ush_scr.at[b - 1]
                r = r0 + a * SUBLANES
                acc = acc + w_ref[j:j + 1, c0:c0 + LANES] * src[r:r + row_blk, c0:c0 + LANES]
            c_scr[r0:r0 + row_blk, c0:c0 + LANES] = acc
    c = c_scr[...]
    mu = jnp.mean(c, axis=-1, keepdims=True)
    xc = c - mu
    var = jnp.mean(xc * xc, axis=-1, keepdims=True)
    y = xc * lax.rsqrt(var + EPS) * lg_ref[...] + lb_ref[...]
    o_ref[...] = (y * jax.nn.sigmoid(y)).astype(o_ref.dtype)


def conv_module(proj, conv_w, conv_b, ln_g, ln_b, seq, ts=256):
    t = proj.shape[0]
    ts = min(ts, seq)
    n_ch = C_CHANNELS
    halo_blocks = ts // CONV_HALO
    row = lambda a: a.reshape(1, n_ch).astype(F32)
    small = pl.BlockSpec((1, n_ch), lambda i: (0, 0))
    prev = lambda col: pl.BlockSpec((CONV_HALO, n_ch), lambda i: (jnp.maximum(i * halo_blocks - 1, 0), col))
    return pl.pallas_call(
        functools.partial(_conv_kernel, ts=ts, tiles_per_seq=seq // ts, row_blk=64),
        out_shape=jax.ShapeDtypeStruct((t, n_ch), BF16),
        grid=(t // ts,),
        in_specs=[
            pl.BlockSpec((ts, n_ch), lambda i: (i, 0)),
            pl.BlockSpec((ts, n_ch), lambda i: (i, 1)),
            prev(0), prev(1),
            pl.BlockSpec((CONV_WIDTH, n_ch), lambda i: (0, 0)),
            small, small, small,
        ],
        out_specs=pl.BlockSpec((ts, n_ch), lambda i: (i, 0)),
        scratch_shapes=[pltpu.VMEM((ts + CONV_HALO, n_ch), F32),
                        pltpu.VMEM((SUBLANES - 1, ts + CONV_HALO - SUBLANES, n_ch), F32),
                        pltpu.VMEM((ts, n_ch), F32)],
        compiler_params=_cparams(("parallel",)),
        name="conv_module",
    )(proj, proj, proj, proj, conv_w.astype(F32), row(conv_b), row(ln_g), row(ln_b))


def _rel_bias_kernel(tbl_ref, o_ref):
    p = pl.program_id(0)
    kk = lax.broadcasted_iota(I32, (BAND, LANES), 0)
    lane = lax.broadcasted_iota(I32, (BAND, LANES), 1)
    idx = jnp.clip(lane % CHUNK - (kk - LEFT_CHUNKS * CHUNK), -MAX_REL, MAX_REL) + MAX_REL
    first = lane < CHUNK

    def body(j, acc):
        return acc + jnp.where(idx == j, jnp.where(first, tbl_ref[2 * p, j], tbl_ref[2 * p + 1, j]), 0.0)

    o_ref[0] = lax.fori_loop(0, 2 * MAX_REL + 1, body, jnp.zeros((BAND, LANES), F32)) * LOG2E


def rel_bias(rel_table):
    pairs = D_HEADS // 2
    assert 2 * CHUNK == LANES
    return pl.pallas_call(
        _rel_bias_kernel,
        out_shape=jax.ShapeDtypeStruct((pairs, BAND, LANES), F32),
        grid=(pairs,),
        in_specs=[pl.BlockSpec(memory_space=pltpu.SMEM)],
        out_specs=pl.BlockSpec((1, BAND, LANES), lambda p: (p, 0, 0)),
        compiler_params=_cparams(("parallel",)),
        name="rel_bias",
    )(rel_table.astype(F32))


def _chunk_attn_kernel(q_ref, k_ref, v_ref, bias_ref, o_ref, kpad, vpad, *, group, seq):
    i = pl.program_id(2)
    pad = LEFT_CHUNKS * CHUNK

    @pl.when(i == 0)
    def _():
        kpad[0:pad, :] = jnp.zeros((pad, LANES), BF16)
        vpad[0:pad, :] = jnp.zeros((pad, LANES), BF16)
        kpad[pad:, :] = k_ref[...]
        vpad[pad:, :] = v_ref[...]

    lane = lax.broadcasted_iota(I32, (CHUNK, LANES), 1)
    band_chunk = lax.broadcasted_iota(I32, (BAND, LANES), 0) // CHUNK
    bias = bias_ref[0]
    ones = jnp.ones((BAND, LANES), BF16)
    for g in range(group):
        c = i * group + g
        qs = (q_ref[g * CHUNK:(g + 1) * CHUNK, :].astype(F32) * (HEAD_DIM ** -0.5 * LOG2E)).astype(BF16)
        zero = jnp.zeros_like(qs)
        q2 = jnp.concatenate([jnp.where(lane < HEAD_DIM, qs, zero),
                              jnp.where(lane >= HEAD_DIM, qs, zero)], axis=0)
        start = pl.multiple_of(c * CHUNK, CHUNK)
        kw = kpad[pl.ds(start, BAND), :]
        vw = vpad[pl.ds(start, BAND), :]
        st = _nt_dot(kw, q2) + bias
        st = jnp.where(band_chunk >= LEFT_CHUNKS - c, st, NEG)
        pt = jnp.exp2(st - jnp.max(st, axis=0, keepdims=True))
        o2 = lax.dot_general(pt.astype(BF16), jnp.concatenate([vw, ones], axis=1),
                             (((0,), (0,)), ((), ())), preferred_element_type=F32)
        o = o2[:, 0:LANES] / o2[:, LANES:]
        o_ref[g * CHUNK:(g + 1) * CHUNK, :] = jnp.where(lane < HEAD_DIM, o[0:CHUNK, :],
                                                        o[CHUNK:, :]).astype(o_ref.dtype)


def chunk_rel_attention(proj, bias, batch, seq, group=16):
    t = proj.shape[0]
    pairs = D_HEADS // 2
    q_off = (2 * C_CHANNELS) // LANES
    k_off = q_off + pairs
    v_off = k_off + pairs
    tq = group * CHUNK
    nq = seq // tq
    return pl.pallas_call(
        functools.partial(_chunk_attn_kernel, group=group, seq=seq),
        out_shape=jax.ShapeDtypeStruct((t, D_HEADS * HEAD_DIM), BF16),
        grid=(batch, pairs, nq),
        in_specs=[
            pl.BlockSpec((tq, LANES), lambda b, p, i: (b * nq + i, q_off + p)),
            pl.BlockSpec((seq, LANES), lambda b, p, i: (b, k_off + p)),
            pl.BlockSpec((seq, LANES), lambda b, p, i: (b, v_off + p)),
            pl.BlockSpec((1, BAND, LANES), lambda b, p, i: (p, 0, 0)),
        ],
        out_specs=pl.BlockSpec((tq, LANES), lambda b, p, i: (b * nq + i, p)),
        scratch_shapes=[pltpu.VMEM((seq + LEFT_CHUNKS * CHUNK, LANES), BF16),
                        pltpu.VMEM((seq + LEFT_CHUNKS * CHUNK, LANES), BF16)],
        compiler_params=_cparams(("parallel", "parallel", "arbitrary")),
        name="chunk_attn",
    )(proj, proj, proj, bias)


def _store_row_tiles(ref, val):
    n, d = val.shape
    sub = d // LANES
    for k in range(sub):
        ref[pl.ds(k, n, stride=sub), :] = val[:, k * LANES:(k + 1) * LANES]


def _load_row_tiles(ref, n, d):
    sub = d // LANES
    return jnp.concatenate([ref[pl.ds(k, n, stride=sub), :] for k in range(sub)], axis=1)


def _router_kernel(x_ref, g_ref, wr_ref, h_ref, meta_ref, cnt_ref, carry_scr, tri_scr, *, tr):
    i = pl.program_id(0)

    @pl.when(i == 0)
    def _():
        carry_scr[...] = jnp.zeros(carry_scr.shape, F32)
        r_i = lax.broadcasted_iota(I32, (tr, tr), 0)
        c_i = lax.broadcasted_iota(I32, (tr, tr), 1)
        tri_scr[...] = jnp.where(r_i < c_i, 1.0, 0.0).astype(BF16)

    x = x_ref[...]
    hn = x * lax.rsqrt(jnp.mean(x * x, axis=-1, keepdims=True) + EPS) * g_ref[...]
    _store_row_tiles(h_ref, hn)
    h_hi = hn.astype(BF16)
    h_lo = (hn - h_hi.astype(F32)).astype(BF16)
    w = wr_ref[...]
    w_hi = w.astype(BF16)
    w_lo = (w - w_hi.astype(F32)).astype(BF16)
    logits = _nt_dot(w_hi, h_hi) + (_nt_dot(w_hi, h_lo) + _nt_dot(w_lo, h_hi))
    eidx = lax.broadcasted_iota(I32, logits.shape, 0).astype(F32)
    m1 = jnp.max(logits, axis=0, keepdims=True)
    i1 = jnp.min(jnp.where(logits == m1, eidx, float(N_EXPERTS)), axis=0, keepdims=True)
    rest = jnp.where(eidx == i1, -jnp.inf, logits)
    m2 = jnp.max(rest, axis=0, keepdims=True)
    i2 = jnp.min(jnp.where(rest == m2, eidx, float(N_EXPERTS)), axis=0, keepdims=True)
    e = jnp.exp(m2 - m1)
    g1 = 1.0 / (1.0 + e)
    g2 = e / (1.0 + e)
    sel1 = eidx == i1
    sel2 = eidx == i2
    onehot = jnp.where(sel1 | sel2, 1.0, 0.0)
    rank = jnp.dot(onehot.astype(BF16), tri_scr[...], preferred_element_type=F32) + carry_scr[...]
    r1 = jnp.sum(jnp.where(sel1, rank, 0.0), axis=0, keepdims=True)
    r2 = jnp.sum(jnp.where(sel2, rank, 0.0), axis=0, keepdims=True)
    carry = carry_scr[...] + jnp.sum(onehot, axis=1, keepdims=True)
    carry_scr[...] = carry
    zero = jnp.zeros_like(g1)
    meta_ref[0] = jnp.concatenate([i1, i2, r1, r2, g1, g2, zero, zero], axis=0)
    cnt_ref[...] = jnp.broadcast_to(carry, cnt_ref.shape)


def router(x2, g, w_router, tr=512):
    t, d = x2.shape
    tr = min(tr, t)
    sub = d // LANES
    return pl.pallas_call(
        functools.partial(_router_kernel, tr=tr),
        out_shape=(jax.ShapeDtypeStruct((t * sub, LANES), F32),
                   jax.ShapeDtypeStruct((t // tr, 8, tr), F32),
                   jax.ShapeDtypeStruct((N_EXPERTS, LANES), F32)),
        grid=(t // tr,),
        in_specs=[
            pl.BlockSpec((tr, d), lambda i: (i, 0)),
            pl.BlockSpec((1, d), lambda i: (0, 0)),
            pl.BlockSpec((N_EXPERTS, d), lambda i: (0, 0)),
        ],
        out_specs=(pl.BlockSpec((tr * sub, LANES), lambda i: (i, 0)),
                   pl.BlockSpec((1, 8, tr), lambda i: (i, 0, 0)),
                   pl.BlockSpec((N_EXPERTS, LANES), lambda i: (0, 0))),
        scratch_shapes=[pltpu.VMEM((N_EXPERTS, 1), F32), pltpu.VMEM((tr, tr), BF16)],
        compiler_params=_cparams(("arbitrary",)),
        name="router",
    )(x2, g.reshape(1, d), w_router.T.astype(F32))


ROW_DMA_UNROLL = 8


def _rows(ref, row, sub):
    return ref.at[pl.ds(pl.multiple_of(row * sub, sub), sub)]


def _scatter_rows_kernel(ends_ref, idx1_ref, idx2_ref, src_ref, dst_ref, zero_scr, sem, *, rows, sub, pad_rows):
    @pl.when(pl.program_id(0) == 0)
    def _():
        zero_scr[...] = jnp.zeros(zero_scr.shape, zero_scr.dtype)

        def tail(e):
            start = jnp.maximum(ends_ref[e] - pad_rows, 0)
            return pltpu.make_async_copy(zero_scr, dst_ref.at[pl.ds(pl.multiple_of(start * sub, sub),
                                                                  pad_rows * sub)], sem)

        n_exp = ends_ref.shape[0]
        for e in range(n_exp):
            tail(e).start()
        for e in range(n_exp):
            tail(e).wait()
        n_rows = dst_ref.shape[0] // sub
        for t in range(n_exp):
            start = ends_ref[n_exp - 1] + t * pad_rows

            @pl.when(start < n_rows)
            def _():
                spare = pltpu.make_async_copy(
                    zero_scr, dst_ref.at[pl.ds(pl.multiple_of(start * sub, sub), pad_rows * sub)], sem)
                spare.start()
                spare.wait()

    def issue(r, carry):
        pltpu.make_async_copy(_rows(src_ref, r, sub), _rows(dst_ref, idx1_ref[0, 0, r], sub), sem).start()
        pltpu.make_async_copy(_rows(src_ref, r, sub), _rows(dst_ref, idx2_ref[0, 0, r], sub), sem).start(priority=1)
        return carry

    lax.fori_loop(0, rows, issue, 0, unroll=ROW_DMA_UNROLL)
    for _ in range(2):
        pltpu.make_async_copy(src_ref, dst_ref.at[pl.ds(0, rows * sub)], sem).wait()


def scatter_rows(src, idx1, idx2, ends, n_rows, pad_rows, sub, rows=512):
    n = idx1.shape[0]
    rows = min(rows, n)
    assert n % rows == 0 and rows % ROW_DMA_UNROLL == 0 and n_rows >= max(rows, pad_rows)
    idx_spec = pl.BlockSpec((1, 1, rows), lambda i, ends: (i, 0, 0), memory_space=pltpu.SMEM)
    grid_spec = pltpu.PrefetchScalarGridSpec(
        num_scalar_prefetch=1,
        grid=(n // rows,),
        in_specs=[idx_spec, idx_spec, pl.BlockSpec((rows * sub, LANES), lambda i, ends: (i, 0))],
        out_specs=pl.BlockSpec(memory_space=pl.ANY),
        scratch_shapes=[pltpu.VMEM((pad_rows * sub, LANES), src.dtype), pltpu.SemaphoreType.DMA(())],
    )
    return pl.pallas_call(
        functools.partial(_scatter_rows_kernel, rows=rows, sub=sub, pad_rows=pad_rows),
        out_shape=jax.ShapeDtypeStruct((n_rows * sub, LANES), src.dtype),
        grid_spec=grid_spec,
        compiler_params=_cparams(("arbitrary",)),
        name="row_scatter",
    )(ends, idx1.reshape(n // rows, 1, rows), idx2.reshape(n // rows, 1, rows), src)


def _expert_ffn_kernel(te_ref, nt_ref, x_ref, wg_ref, wu_ref, wd_ref, o_ref, xn_scr, h_scr, acc_scr, *, ff_chunk):
    n = pl.program_id(0)
    f = pl.program_id(1)
    tm, d = acc_scr.shape

    @pl.when(n < nt_ref[0])
    def _():
        @pl.when(f == 0)
        def _():
            xn_scr[...] = _load_row_tiles(x_ref, tm, d).astype(BF16)

        _swiglu_chunks(xn_scr[...], wg_ref.at[0], wu_ref.at[0], h_scr, ff_chunk)
        part = jnp.dot(h_scr[...], wd_ref[0], preferred_element_type=F32)

        @pl.when(f == 0)
        def _():
            acc_scr[...] = part

        @pl.when(f > 0)
        def _():
            acc_scr[...] += part

        @pl.when(f == pl.num_programs(1) - 1)
        def _():
            _store_row_tiles(o_ref, acc_scr[...])

    @pl.when((n >= nt_ref[0]) & (f == pl.num_programs(1) - 1))
    def _():
        o_ref[...] = jnp.zeros(o_ref.shape, o_ref.dtype)


def expert_ffn(xs, tile_expert, n_tiles_used, wg, wu, wd, tm, ff_split=2):
    d, d_ff = wg.shape[1], wg.shape[2]
    sub = d // LANES
    nr = xs.shape[0] // sub
    ff_blk = d_ff // ff_split
    ff_chunk = 256
    assert nr % tm == 0 and ff_blk % ff_chunk == 0
    grid_spec = pltpu.PrefetchScalarGridSpec(
        num_scalar_prefetch=2,
        grid=(nr // tm, ff_split),
        in_specs=[
            pl.BlockSpec((tm * sub, LANES), lambda n, f, te, nt: (jnp.minimum(n, nt[0] - 1), 0)),
            pl.BlockSpec((1, d, ff_blk), lambda n, f, te, nt: (te[n], 0, f)),
            pl.BlockSpec((1, d, ff_blk), lambda n, f, te, nt: (te[n], 0, f)),
            pl.BlockSpec((1, ff_blk, d), lambda n, f, te, nt: (te[n], f, 0)),
        ],
        out_specs=pl.BlockSpec((tm * sub, LANES), lambda n, f, te, nt: (n, 0)),
        scratch_shapes=[pltpu.VMEM((tm, d), BF16), pltpu.VMEM((tm, ff_blk), BF16), pltpu.VMEM((tm, d), F32)],
    )
    return pl.pallas_call(
        functools.partial(_expert_ffn_kernel, ff_chunk=ff_chunk),
        out_shape=jax.ShapeDtypeStruct((nr * sub, LANES), F32),
        grid_spec=grid_spec,
        compiler_params=_cparams(("arbitrary", "arbitrary")),
        name="expert_ffn",
    )(tile_expert, n_tiles_used, xs, wg, wu, wd)


def _combine_kernel(idx1_ref, idx2_ref, x_ref, ys_ref, g1_ref, g2_ref, gn_ref, o_ref, y1_buf, y2_buf, sem, *,
                    final_norm):
    tm, d = x_ref.shape
    sub = d // LANES

    def issue(r, carry):
        pltpu.make_async_copy(_rows(ys_ref, idx1_ref[0, 0, r], sub), _rows(y1_buf, r, sub), sem).start()
        pltpu.make_async_copy(_rows(ys_ref, idx2_ref[0, 0, r], sub), _rows(y2_buf, r, sub), sem).start(priority=1)
        return carry

    lax.fori_loop(0, tm, issue, 0, unroll=ROW_DMA_UNROLL)
    for buf in (y1_buf, y2_buf):
        pltpu.make_async_copy(ys_ref.at[pl.ds(0, tm * sub)], buf, sem).wait()
    y = (x_ref[...] + g1_ref[...] * _load_row_tiles(y1_buf, tm, d)
         + g2_ref[...] * _load_row_tiles(y2_buf, tm, d))
    if final_norm:
        y = y * lax.rsqrt(jnp.mean(y * y, axis=-1, keepdims=True) + EPS) * gn_ref[...]
    o_ref[...] = y


def combine(x2, ys, idx1, idx2, g1, g2, gn, final_norm, tm=512):
    t, d = x2.shape
    tm = min(tm, t)
    sub = d // LANES
    assert t % tm == 0 and tm % ROW_DMA_UNROLL == 0 and ys.shape[0] >= tm * sub
    big = pl.BlockSpec((tm, d), lambda i: (i, 0))
    col = pl.BlockSpec((tm, 1), lambda i: (i, 0))
    idx_spec = pl.BlockSpec((1, 1, tm), lambda i: (i, 0, 0), memory_space=pltpu.SMEM)
    return pl.pallas_call(
        functools.partial(_combine_kernel, final_norm=final_norm),
        out_shape=jax.ShapeDtypeStruct((t, d), F32),
        grid=(t // tm,),
        in_specs=[idx_spec, idx_spec, big, pl.BlockSpec(memory_space=pl.ANY), col, col,
                  pl.BlockSpec((1, d), lambda i: (0, 0))],
        out_specs=big,
        scratch_shapes=[pltpu.VMEM((tm * sub, LANES), F32), pltpu.VMEM((tm * sub, LANES), F32),
                        pltpu.SemaphoreType.DMA(())],
        compiler_params=_cparams(("arbitrary",)),
        name="moe_combine",
    )(idx1.reshape(t // tm, 1, tm), idx2.reshape(t // tm, 1, tm), x2, ys, g1, g2, gn.reshape(1, d))


def moe_residual(x2, g, w_router, wg, wu, wd, expert_base, final_gain, final_norm, tm=512):
    t, d = x2.shape
    tm = min(tm, t)
    hn, meta, counts = router(x2, g, w_router)
    tr = meta.shape[2]
    field = lambda r: meta[:, r, :].reshape(t)
    idx1, idx2 = field(0).astype(I32), field(1).astype(I32)
    rank1, rank2 = field(2).astype(I32), field(3).astype(I32)
    gate1, gate2 = field(4), field(5)
    cnt = counts[:, 0].astype(I32)
    padded = ((cnt + tm - 1) // tm) * tm
    ends = jnp.cumsum(padded)
    offs = ends - padded
    pos1 = offs[idx1] + rank1
    pos2 = offs[idx2] + rank2
    n_rows = TOP_K * t + N_EXPERTS * tm
    tile_start = jnp.arange(n_rows // tm, dtype=I32) * tm
    tile_expert = jnp.minimum(jnp.sum(tile_start[:, None] >= ends[None, :], axis=1), N_EXPERTS - 1).astype(I32)
    tile_expert = tile_expert + expert_base
    n_tiles_used = (ends[-1:] // tm).astype(I32)

    sub = d // LANES
    xs = scatter_rows(hn, pos1, pos2, ends.astype(I32), n_rows, tm, sub)
    ys = expert_ffn(xs, tile_expert, n_tiles_used, wg, wu, wd, tm)
    return combine(x2, ys, pos1, pos2, gate1.reshape(t, 1), gate2.reshape(t, 1), final_gain, final_norm)


def kernel(x, ev_norm_mix, ev_w_in, ev_lambda_q1, ev_lambda_k1, ev_lambda_q2, ev_lambda_k2, ev_subln, ev_w_out,
           ev_norm_ffn, ev_ffn_gate, ev_ffn_up, ev_ffn_down, od_norm_mix, od_w_in, od_conv_w, od_conv_b,
           od_conv_ln_g, od_conv_ln_b, od_rel_bias, od_w_out, od_norm_ffn, od_router, od_exp_gate, od_exp_up,
           od_exp_down, final_norm):
    batch, seq, d = x.shape
    depth = 2 * ev_w_in.shape[0]
    assert od_w_in.shape[0] * 2 == depth
    x2 = x.reshape(batch * seq, d)
    bf = lambda a: a.astype(BF16)
    a_qk = A_HEADS * 2 * HEAD_DIM
    a_v = A_HEADS * 2 * HEAD_DIM
    b_w = B_HEADS * HEAD_DIM
    attn_k_block = 256
    all_experts = lambda w: bf(w).reshape((-1,) + w.shape[2:])
    exp_gate, exp_up, exp_down = all_experts(od_exp_gate), all_experts(od_exp_up), all_experts(od_exp_down)
    for layer in range(depth):
        i = layer // 2
        if layer % 2 == 0:
            lambda_init = 0.8 - 0.6 * math.exp(-0.3 * layer)
            w = ev_w_in[i]
            va0, qb0 = 2 * a_qk, 2 * a_qk + a_v
            vb0 = qb0 + 2 * b_w
            w_qk = bf(jnp.concatenate([w[:, 0:va0], w[:, qb0:vb0]], axis=1))
            w_vt = bf(jnp.concatenate([w[:, va0:qb0], w[:, vb0:]], axis=1).T)
            proj, vt = norm_proj(x2, ev_norm_mix[i], w_qk, seq, rope_cols=2 * a_qk, wvt_bf16=w_vt,
                                 tk=attn_k_block)
            oa = diff_attention(proj, vt, ev_lambda_q1[i], ev_lambda_k1[i], ev_lambda_q2[i], ev_lambda_k2[i],
                                ev_subln[i], batch, seq, lambda_init)
            ob = stick_breaking(proj, vt, batch, seq)
            x2 = out_proj_residual(x2, oa, ob, bf(ev_w_out[i]))
            x2 = ffn_residual(x2, ev_norm_ffn[i], bf(ev_ffn_gate[i]), bf(ev_ffn_up[i]), bf(ev_ffn_down[i]))
        else:
            proj = norm_proj(x2, od_norm_mix[i], bf(od_w_in[i]), seq)
            c = conv_module(proj, od_conv_w[i], od_conv_b[i], od_conv_ln_g[i], od_conv_ln_b[i], seq)
            od = chunk_rel_attention(proj, rel_bias(od_rel_bias[i]), batch, seq)
            x2 = out_proj_residual(x2, c, od, bf(od_w_out[i]))
            x2 = moe_residual(x2, od_norm_ffn[i], od_router[i], exp_gate, exp_up, exp_down, i * N_EXPERTS,
                              final_norm, final_norm=(layer == depth - 1))
    return x2.reshape(batch, seq, d)
```

```python
import functools
import math

import numpy as np
import jax
import jax.numpy as jnp
from jax import lax
from jax.experimental import pallas as pl
from jax.experimental.pallas import tpu as pltpu

F32 = jnp.float32
BF16 = jnp.bfloat16
I32 = jnp.int32

HEAD_DIM = 64
CHUNK = 64
ROPE_THETA = 10000.0
EPS = 1e-6
A_HEADS = 4
B_HEADS = 8
C_CHANNELS = 512
CONV_WIDTH = 31
D_HEADS = 8
LEFT_CHUNKS = 8
BAND = (LEFT_CHUNKS + 1) * CHUNK
MAX_REL = 128
N_EXPERTS = 8
TOP_K = 2

LANES = 128
SUBLANES = 8
NEG = -1e30
LOG2E = math.log2(math.e)
SIGN_BIT = np.int32(-2 ** 31)
VMEM_LIMIT = 56 * 1024 * 1024


def _cparams(sem):
    return pltpu.CompilerParams(dimension_semantics=sem, vmem_limit_bytes=VMEM_LIMIT)


def _nt_dot(a, b):
    return lax.dot_general(a, b, (((1,), (1,)), ((), ())), preferred_element_type=F32)


def _norm_proj_kernel(x_ref, g_ref, w_ref, cos_ref, sin_ref, *rest, col_chunk, rope_cols, tk):
    if tk:
        wvt_ref, o_ref, vt_ref = rest
    else:
        (o_ref,) = rest
    x = x_ref[...]
    xn = (x * lax.rsqrt(jnp.mean(x * x, axis=-1, keepdims=True) + EPS) * g_ref[...]).astype(BF16)
    n_out = o_ref.shape[1]
    if tk:
        for r0 in range(0, wvt_ref.shape[0], col_chunk):
            res = _nt_dot(wvt_ref[r0:r0 + col_chunk, :], xn)
            for b in range(vt_ref.shape[0]):
                vt_ref[b, r0:r0 + col_chunk, :] = res[:, b * tk:(b + 1) * tk].astype(BF16)
    if rope_cols:
        cos = cos_ref[...]
        sin = sin_ref[...]
        lane = lax.broadcasted_iota(I32, cos.shape, 1)
        first_half = (lane % HEAD_DIM) < (HEAD_DIM // 2)
    for c0 in range(0, n_out, col_chunk):
        r = jnp.dot(xn, w_ref[:, c0:c0 + col_chunk], preferred_element_type=F32)
        if c0 < rope_cols:
            parts = []
            for l0 in range(0, col_chunk, LANES):
                seg = r[:, l0:l0 + LANES]
                partner = jnp.where(first_half,
                                    pltpu.roll(seg, LANES - HEAD_DIM // 2, 1),
                                    pltpu.roll(seg, HEAD_DIM // 2, 1))
                parts.append(seg * cos + partner * sin)
            r = jnp.concatenate(parts, axis=1)
        o_ref[:, c0:c0 + col_chunk] = r.astype(o_ref.dtype)


def _rope_tables(seq):
    half = HEAD_DIM // 2
    inv_freq = ROPE_THETA ** (-jnp.arange(half, dtype=F32) * 2.0 / HEAD_DIM)
    ang = jnp.arange(seq, dtype=F32)[:, None] * inv_freq[None, :]
    cos, sin = jnp.cos(ang), jnp.sin(ang)
    cos_t = jnp.tile(jnp.concatenate([cos, cos], axis=1), (1, LANES // HEAD_DIM))
    sin_t = jnp.tile(jnp.concatenate([-sin, sin], axis=1), (1, LANES // HEAD_DIM))
    return cos_t, sin_t


def norm_proj(x2, g, w_bf16, seq, rope_cols=0, wvt_bf16=None, tk=0, tm=512):
    t, d = x2.shape
    n_out = w_bf16.shape[1]
    tm = min(tm, seq)
    col_chunk = 512
    assert t % tm == 0 and seq % tm == 0 and n_out % col_chunk == 0 and rope_cols % col_chunk == 0
    cos_t, sin_t = _rope_tables(seq)
    tiles_per_seq = seq // tm
    in_specs = [
        pl.BlockSpec((tm, d), lambda i: (i, 0)),
        pl.BlockSpec((1, d), lambda i: (0, 0)),
        pl.BlockSpec((d, n_out), lambda i: (0, 0)),
        pl.BlockSpec((tm, LANES), lambda i: (i % tiles_per_seq, 0)),
        pl.BlockSpec((tm, LANES), lambda i: (i % tiles_per_seq, 0)),
    ]
    args = [x2, g.reshape(1, d), w_bf16, cos_t, sin_t]
    out_shape = jax.ShapeDtypeStruct((t, n_out), BF16)
    out_specs = pl.BlockSpec((tm, n_out), lambda i: (i, 0))
    if wvt_bf16 is not None:
        n_v = wvt_bf16.shape[0]
        tk = min(tk, tm)
        assert tm % tk == 0 and n_v % col_chunk == 0
        in_specs.append(pl.BlockSpec((n_v, d), lambda i: (0, 0)))
        args.append(wvt_bf16)
        out_shape = (out_shape, jax.ShapeDtypeStruct((t // tk, n_v, tk), BF16))
        out_specs = (out_specs, pl.BlockSpec((tm // tk, n_v, tk), lambda i: (i, 0, 0)))
    return pl.pallas_call(
        functools.partial(_norm_proj_kernel, col_chunk=col_chunk, rope_cols=rope_cols,
                          tk=tk if wvt_bf16 is not None else 0),
        out_shape=out_shape,
        grid=(t // tm,),
        in_specs=in_specs,
        out_specs=out_specs,
        compiler_params=_cparams(("parallel",)),
        name="norm_proj",
    )(*args)


def _stack_query_pair(q_ref, q2_scr, tq):
    lane = lax.broadcasted_iota(I32, (tq, LANES), 1)
    qs = (q_ref[...].astype(F32) * (HEAD_DIM ** -0.5 * LOG2E)).astype(BF16)
    zero = jnp.zeros_like(qs)
    q2_scr[0:tq, :] = jnp.where(lane < HEAD_DIM, qs, zero)
    q2_scr[tq:, :] = jnp.where(lane >= HEAD_DIM, qs, zero)


def _diag_visible(tq, tk, rule):
    key = np.arange(tq)[:, None]
    qry = (np.arange(2 * tq) % tq)[None, :]
    return rule(key, qry).reshape(tq // tk, tk, 2 * tq)


def _diff_attn_kernel(q_ref, k_ref, vt_ref, mask_ref, lq1_ref, lk1_ref, lq2_ref, lk2_ref, g_ref, o_ref,
                      q2_scr, m_scr, l_scr, acc_scr, st0_scr, st1_scr, *, tq, tk, lambda_init):
    i = pl.program_id(2)
    assert tq == 2 * tk
    _stack_query_pair(q_ref, q2_scr, tq)
    m_scr[...] = jnp.full(m_scr.shape, NEG, F32)
    l_scr[...] = jnp.zeros(l_scr.shape, F32)
    late_queries = (slice(tk, tq), slice(tq + tk, 2 * tq))
    acc_scr[...] = jnp.zeros(acc_scr.shape, F32)

    def scores(j, st_scr):
        k = k_ref[pl.ds(pl.multiple_of(j * tk, tk), tk), :]
        st_scr[...] = _nt_dot(k, q2_scr[...])

    def consume(j, st_scr, diag=None, cols=(slice(None),)):
        for c in cols:
            st = st_scr[:, c]
            if diag is not None:
                st = st + mask_ref[diag, :, c]
            m_old = m_scr[:, c]
            m_new = jnp.maximum(m_old, jnp.max(st, axis=0, keepdims=True))
            alpha = jnp.exp2(m_old - m_new)
            pt = jnp.exp2(st - m_new)
            l_scr[:, c] = alpha * l_scr[:, c] + jnp.sum(pt, axis=0, keepdims=True)
            acc_scr[:, c] = alpha * acc_scr[:, c] + jnp.dot(vt_ref[j], pt.astype(BF16),
                                                            preferred_element_type=F32)
            m_scr[:, c] = m_new

    scores(0, st0_scr)

    def body(n, carry):
        scores(2 * n + 1, st1_scr)
        consume(2 * n, st0_scr)
        scores(2 * n + 2, st0_scr)
        consume(2 * n + 1, st1_scr)
        return carry

    lax.fori_loop(0, i, body, 0)
    scores(2 * i + 1, st1_scr)
    consume(2 * i, st0_scr, diag=0)
    consume(2 * i + 1, st1_scr, diag=1, cols=late_queries)

    ot = acc_scr[...] / l_scr[...]
    lam = (jnp.exp(jnp.sum(lq1_ref[...] * lk1_ref[...], axis=1, keepdims=True))
           - jnp.exp(jnp.sum(lq2_ref[...] * lk2_ref[...], axis=1, keepdims=True)) + lambda_init)
    od = ot[:, 0:tq].T - lam * ot[:, tq:].T
    y = od * lax.rsqrt(jnp.mean(od * od, axis=-1, keepdims=True) + EPS) * g_ref[...]
    o_ref[...] = (y * (1.0 - lambda_init)).astype(o_ref.dtype)


def diff_attention(proj, vt, lq1, lk1, lq2, lk2, subln_g, batch, seq, lambda_init):
    t = proj.shape[0]
    tk = vt.shape[2]
    tq = 2 * tk
    nq = seq // tq
    assert seq % tq == 0 and tk % CHUNK == 0
    k_off = (A_HEADS * 2 * HEAD_DIM) // LANES
    vec = lambda a: a.reshape(1, -1).astype(F32)
    small = lambda n: pl.BlockSpec((1, n), lambda b, h, i: (0, 0))
    mask = jnp.asarray(np.where(_diag_visible(tq, tk, lambda key, qry: key // CHUNK <= qry // CHUNK), 0.0, NEG), F32)
    return pl.pallas_call(
        functools.partial(_diff_attn_kernel, tq=tq, tk=tk, lambda_init=lambda_init),
        out_shape=jax.ShapeDtypeStruct((t, A_HEADS * 2 * HEAD_DIM), BF16),
        grid=(batch, A_HEADS, nq),
        in_specs=[
            pl.BlockSpec((tq, LANES), lambda b, h, i: (b * nq + i, h)),
            pl.BlockSpec((seq, LANES), lambda b, h, i: (b, k_off + h)),
            pl.BlockSpec((seq // tk, LANES, tk), lambda b, h, i: (b, h, 0)),
            pl.BlockSpec((tq // tk, tk, 2 * tq), lambda b, h, i: (0, 0, 0)),
            small(HEAD_DIM), small(HEAD_DIM), small(HEAD_DIM), small(HEAD_DIM), small(2 * HEAD_DIM),
        ],
        out_specs=pl.BlockSpec((tq, LANES), lambda b, h, i: (b * nq + i, h)),
        scratch_shapes=[
            pltpu.VMEM((2 * tq, LANES), BF16),
            pltpu.VMEM((1, 2 * tq), F32),
            pltpu.VMEM((1, 2 * tq), F32),
            pltpu.VMEM((LANES, 2 * tq), F32),
            pltpu.VMEM((tk, 2 * tq), F32),
            pltpu.VMEM((tk, 2 * tq), F32),
        ],
        compiler_params=_cparams(("parallel", "parallel", "arbitrary")),
        name="diff_attn",
    )(proj, proj, vt, mask, vec(lq1), vec(lk1), vec(lq2), vec(lk2), vec(subln_g))


def _stick_kernel(q_ref, k_ref, vt_ref, mask_ref, o_ref, q2_scr, tri_scr, c_scr, acc_scr, st0_scr, st1_scr, *, tq, tk):
    i = pl.program_id(2)
    assert tq == 2 * tk
    _stack_query_pair(q_ref, q2_scr, tq)
    r_i = lax.broadcasted_iota(I32, (tk, tk), 0)
    c_i = lax.broadcasted_iota(I32, (tk, tk), 1)
    tri_scr[...] = jnp.where(c_i > r_i, 1.0, 0.0).astype(BF16)
    c_scr[...] = jnp.zeros(c_scr.shape, F32)
    acc_scr[...] = jnp.zeros(acc_scr.shape, F32)

    def scores(j, st_scr):
        k = k_ref[pl.ds(pl.multiple_of(j * tk, tk), tk), :]
        st_scr[...] = _nt_dot(k, q2_scr[...])

    def consume(j, st_scr, diag=None, cols=(slice(None),)):
        for c in cols:
            zt = st_scr[:, c]
            neg_abs = lax.bitcast_convert_type(lax.bitcast_convert_type(zt, I32) | SIGN_BIT, F32)
            sp = jnp.maximum(zt, 0.0) + jnp.log2(1.0 + jnp.exp2(neg_abs))
            if diag is not None:
                visible = mask_ref[diag, :, c]
                spm = sp * visible
            else:
                spm = sp
            after = jnp.dot(tri_scr[...], spm.astype(BF16), preferred_element_type=F32)
            w = jnp.exp2(zt - sp - after - c_scr[:, c])
            if diag is not None:
                w = w * visible
            acc_scr[:, c] += jnp.dot(vt_ref[j], w.astype(BF16), preferred_element_type=F32)
            c_scr[:, c] += jnp.sum(spm, axis=0, keepdims=True)

    scores(2 * i + 1, st0_scr)
    scores(2 * i, st1_scr)
    consume(2 * i + 1, st0_scr, diag=1, cols=(slice(tk, tq), slice(tq + tk, 2 * tq)))
    scores(jnp.maximum(2 * i - 1, 0), st0_scr)
    consume(2 * i, st1_scr, diag=0)

    def body(n, carry):
        j = 2 * (i - n) - 1
        scores(j - 1, st1_scr)
        consume(j, st0_scr)
        scores(jnp.maximum(j - 2, 0), st0_scr)
        consume(j - 1, st1_scr)
        return carry

    lax.fori_loop(0, i, body, 0)
    acc = acc_scr[...]
    vrow = lax.broadcasted_iota(I32, (LANES, tq), 0)
    ot = jnp.where(vrow < HEAD_DIM, acc[:, 0:tq], acc[:, tq:])
    o_ref[...] = ot.T.astype(o_ref.dtype)


def stick_breaking(proj, vt, batch, seq):
    t = proj.shape[0]
    tk = vt.shape[2]
    tq = 2 * tk
    nq = seq // tq
    assert seq % tq == 0
    pairs = (B_HEADS * HEAD_DIM) // LANES
    a_blocks = (A_HEADS * 2 * HEAD_DIM) // LANES
    q_off = 2 * a_blocks
    k_off = q_off + pairs
    v_off = a_blocks
    mask = jnp.asarray(_diag_visible(tq, tk, lambda key, qry: key < qry), F32)
    return pl.pallas_call(
        functools.partial(_stick_kernel, tq=tq, tk=tk),
        out_shape=jax.ShapeDtypeStruct((t, B_HEADS * HEAD_DIM), BF16),
        grid=(batch, pairs, nq),
        in_specs=[
            pl.BlockSpec((tq, LANES), lambda b, p, i: (b * nq + i, q_off + p)),
            pl.BlockSpec((seq, LANES), lambda b, p, i: (b, k_off + p)),
            pl.BlockSpec((seq // tk, LANES, tk), lambda b, p, i: (b, v_off + p, 0)),
            pl.BlockSpec((tq // tk, tk, 2 * tq), lambda b, p, i: (0, 0, 0)),
        ],
        out_specs=pl.BlockSpec((tq, LANES), lambda b, p, i: (b * nq + i, p)),
        scratch_shapes=[
            pltpu.VMEM((2 * tq, LANES), BF16),
            pltpu.VMEM((tk, tk), BF16),
            pltpu.VMEM((1, 2 * tq), F32),
            pltpu.VMEM((LANES, 2 * tq), F32),
            pltpu.VMEM((tk, 2 * tq), F32),
            pltpu.VMEM((tk, 2 * tq), F32),
        ],
        compiler_params=_cparams(("parallel", "parallel", "arbitrary")),
        name="stick_attn",
    )(proj, proj, vt, mask)


def _mixer_residual(x_ref, a_ref, b_ref, wo_ref):
    half = a_ref.shape[1]
    return (x_ref[...]
            + jnp.dot(a_ref[...], wo_ref[0:half, :], preferred_element_type=F32)
            + jnp.dot(b_ref[...], wo_ref[half:, :], preferred_element_type=F32))


def _swiglu_chunks(xn, wg_ref, wu_ref, h_scr, ff_chunk):
    d_ff = h_scr.shape[1]
    for c0 in range(0, d_ff, ff_chunk):
        g = jnp.dot(xn, wg_ref[:, c0:c0 + ff_chunk], preferred_element_type=F32)
        u = jnp.dot(xn, wu_ref[:, c0:c0 + ff_chunk], preferred_element_type=F32)
        h_scr[:, c0:c0 + ff_chunk] = (g * jax.nn.sigmoid(g) * u).astype(BF16)


def _ffn_kernel(x_ref, a_ref, b_ref, wo_ref, g_ref, wg_ref, wu_ref, wd_ref, o_ref, h_scr, *, ff_chunk):
    x = _mixer_residual(x_ref, a_ref, b_ref, wo_ref)
    xn = (x * lax.rsqrt(jnp.mean(x * x, axis=-1, keepdims=True) + EPS) * g_ref[...]).astype(BF16)
    _swiglu_chunks(xn, wg_ref, wu_ref, h_scr, ff_chunk)
    o_ref[...] = x + jnp.dot(h_scr[...], wd_ref[...], preferred_element_type=F32)


def ffn_residual(x2, a, b, w_out, g, wg, wu, wd, tm=512):
    t, d = x2.shape
    half = a.shape[1]
    d_ff = wg.shape[1]
    tm = min(tm, t)
    ff_chunk = 256
    assert d_ff % ff_chunk == 0
    resident = lambda shape: pl.BlockSpec(shape, lambda i: (0, 0), pipeline_mode=pl.Buffered(1))
    return pl.pallas_call(
        functools.partial(_ffn_kernel, ff_chunk=ff_chunk),
        out_shape=jax.ShapeDtypeStruct((t, d), F32),
        grid=(t // tm,),
        in_specs=[
            pl.BlockSpec((tm, d), lambda i: (i, 0)),
            pl.BlockSpec((tm, half), lambda i: (i, 0)),
            pl.BlockSpec((tm, half), lambda i: (i, 0)),
            resident((2 * half, d)),
            pl.BlockSpec((1, d), lambda i: (0, 0)),
            resident((d, d_ff)), resident((d, d_ff)), resident((d_ff, d)),
        ],
        out_specs=pl.BlockSpec((tm, d), lambda i: (i, 0)),
        scratch_shapes=[pltpu.VMEM((tm, d_ff), BF16)],
        compiler_params=_cparams(("parallel",)),
        name="ffn",
    )(x2, a, b, w_out, g.reshape(1, d), wg, wu, wd)


CONV_HALO = 32


def _conv_kernel(val_ref, gate_ref, pval_ref, pgate_ref, w_ref, b_ref, lg_ref, lb_ref, o_ref,
                 u_scr, ush_scr, c_scr, *, ts, tiles_per_seq, row_blk):
    i = pl.program_id(0)
    glu = lambda v, g: v.astype(F32) * jax.nn.sigmoid(g.astype(F32))
    keep = jnp.where(i % tiles_per_seq == 0, 0.0, 1.0)
    u_scr[0:CONV_HALO, :] = glu(pval_ref[...], pgate_ref[...]) * keep
    u_scr[CONV_HALO:, :] = glu(val_ref[...], gate_ref[...])
    shift = CONV_HALO - (CONV_WIDTH - 1)
    n_ch = val_ref.shape[1]
    n_sh = ush_scr.shape[1]
    for b in range(1, SUBLANES):
        ush_scr[b - 1] = u_scr[b:b + n_sh, :]
    for c0 in range(0, n_ch, LANES):
        for r0 in range(0, ts, row_blk):
            acc = jnp.broadcast_to(b_ref[:, c0:c0 + LANES], (row_blk, LANES))
            for j in range(CONV_WIDTH):
                a, b = divmod(shift + j, SUBLANES)
                src = u_scr if b == 0 else ush_scr.at[b - 1]
                r = r0 + a * SUBLANES
                acc = acc + w_ref[j:j + 1, c0:c0 + LANES] * src[r:r + row_blk, c0:c0 + LANES]
            c_scr[r0:r0 + row_blk, c0:c0 + LANES] = acc
    c = c_scr[...]
    mu = jnp.mean(c, axis=-1, keepdims=True)
    xc = c - mu
    var = jnp.mean(xc * xc, axis=-1, keepdims=True)
    y = xc * lax.rsqrt(var + EPS) * lg_ref[...] + lb_ref[...]
    o_ref[...] = (y * jax.nn.sigmoid(y)).astype(o_ref.dtype)


def conv_module(proj, conv_w, conv_b, ln_g, ln_b, seq, ts=256):
    t = proj.shape[0]
    ts = min(ts, seq)
    n_ch = C_CHANNELS
    halo_blocks = ts // CONV_HALO
    row = lambda a: a.reshape(1, n_ch).astype(F32)
    small = pl.BlockSpec((1, n_ch), lambda i: (0, 0))
    prev = lambda col: pl.BlockSpec((CONV_HALO, n_ch), lambda i: (jnp.maximum(i * halo_blocks - 1, 0), col))
    return pl.pallas_call(
        functools.partial(_conv_kernel, ts=ts, tiles_per_seq=seq // ts, row_blk=64),
        out_shape=jax.ShapeDtypeStruct((t, n_ch), BF16),
        grid=(t // ts,),
        in_specs=[
            pl.BlockSpec((ts, n_ch), lambda i: (i, 0)),
            pl.BlockSpec((ts, n_ch), lambda i: (i, 1)),
            prev(0), prev(1),
            pl.BlockSpec((CONV_WIDTH, n_ch), lambda i: (0, 0)),
            small, small, small,
        ],
        out_specs=pl.BlockSpec((ts, n_ch), lambda i: (i, 0)),
        scratch_shapes=[pltpu.VMEM((ts + CONV_HALO, n_ch), F32),
                        pltpu.VMEM((SUBLANES - 1, ts + CONV_HALO - SUBLANES, n_ch), F32),
                        pltpu.VMEM((ts, n_ch), F32)],
        compiler_params=_cparams(("parallel",)),
        name="conv_module",
    )(proj, proj, proj, proj, conv_w.astype(F32), row(conv_b), row(ln_g), row(ln_b))


def _rel_bias_kernel(tbl_ref, o_ref):
    p = pl.program_id(0)
    kk = lax.broadcasted_iota(I32, (BAND, LANES), 0)
    lane = lax.broadcasted_iota(I32, (BAND, LANES), 1)
    idx = jnp.clip(lane % CHUNK - (kk - LEFT_CHUNKS * CHUNK), -MAX_REL, MAX_REL) + MAX_REL
    first = lane < CHUNK

    def body(j, acc):
        return acc + jnp.where(idx == j, jnp.where(first, tbl_ref[2 * p, j], tbl_ref[2 * p + 1, j]), 0.0)

    o_ref[0] = lax.fori_loop(0, 2 * MAX_REL + 1, body, jnp.zeros((BAND, LANES), F32)) * LOG2E


def rel_bias(rel_table):
    pairs = D_HEADS // 2
    assert 2 * CHUNK == LANES
    return pl.pallas_call(
        _rel_bias_kernel,
        out_shape=jax.ShapeDtypeStruct((pairs, BAND, LANES), F32),
        grid=(pairs,),
        in_specs=[pl.BlockSpec(memory_space=pltpu.SMEM)],
        out_specs=pl.BlockSpec((1, BAND, LANES), lambda p: (p, 0, 0)),
        compiler_params=_cparams(("parallel",)),
        name="rel_bias",
    )(rel_table.astype(F32))


def _chunk_attn_kernel(q_ref, k_ref, v_ref, bias_ref, o_ref, kpad, vpad, *, group, seq):
    i = pl.program_id(2)
    pad = LEFT_CHUNKS * CHUNK

    @pl.when(i == 0)
    def _():
        kpad[0:pad, :] = jnp.zeros((pad, LANES), BF16)
        vpad[0:pad, :] = jnp.zeros((pad, LANES), BF16)
        kpad[pad:, :] = k_ref[...]
        vpad[pad:, :] = v_ref[...]

    lane = lax.broadcasted_iota(I32, (CHUNK, LANES), 1)
    band_chunk = lax.broadcasted_iota(I32, (BAND, LANES), 0) // CHUNK
    bias = bias_ref[0]
    ones = jnp.ones((BAND, LANES), BF16)
    for g in range(group):
        c = i * group + g
        qs = (q_ref[g * CHUNK:(g + 1) * CHUNK, :].astype(F32) * (HEAD_DIM ** -0.5 * LOG2E)).astype(BF16)
        zero = jnp.zeros_like(qs)
        q2 = jnp.concatenate([jnp.where(lane < HEAD_DIM, qs, zero),
                              jnp.where(lane >= HEAD_DIM, qs, zero)], axis=0)
        start = pl.multiple_of(c * CHUNK, CHUNK)
        kw = kpad[pl.ds(start, BAND), :]
        vw = vpad[pl.ds(start, BAND), :]
        st = _nt_dot(kw, q2) + bias
        st = jnp.where(band_chunk >= LEFT_CHUNKS - c, st, NEG)
        pt = jnp.exp2(st - jnp.max(st, axis=0, keepdims=True))
        o2 = lax.dot_general(pt.astype(BF16), jnp.concatenate([vw, ones], axis=1),
                             (((0,), (0,)), ((), ())), preferred_element_type=F32)
        o = o2[:, 0:LANES] / o2[:, LANES:]
        o_ref[g * CHUNK:(g + 1) * CHUNK, :] = jnp.where(lane < HEAD_DIM, o[0:CHUNK, :],
                                                        o[CHUNK:, :]).astype(o_ref.dtype)


def chunk_rel_attention(proj, bias, batch, seq, group=16):
    t = proj.shape[0]
    pairs = D_HEADS // 2
    q_off = (2 * C_CHANNELS) // LANES
    k_off = q_off + pairs
    v_off = k_off + pairs
    tq = group * CHUNK
    nq = seq // tq
    return pl.pallas_call(
        functools.partial(_chunk_attn_kernel, group=group, seq=seq),
        out_shape=jax.ShapeDtypeStruct((t, D_HEADS * HEAD_DIM), BF16),
        grid=(batch, pairs, nq),
        in_specs=[
            pl.BlockSpec((tq, LANES), lambda b, p, i: (b * nq + i, q_off + p)),
            pl.BlockSpec((seq, LANES), lambda b, p, i: (b, k_off + p)),
            pl.BlockSpec((seq, LANES), lambda b, p, i: (b, v_off + p)),
            pl.BlockSpec((1, BAND, LANES), lambda b, p, i: (p, 0, 0)),
        ],
        out_specs=pl.BlockSpec((tq, LANES), lambda b, p, i: (b * nq + i, p)),
        scratch_shapes=[pltpu.VMEM((seq + LEFT_CHUNKS * CHUNK, LANES), BF16),
                        pltpu.VMEM((seq + LEFT_CHUNKS * CHUNK, LANES), BF16)],
        compiler_params=_cparams(("parallel", "parallel", "arbitrary")),
        name="chunk_attn",
    )(proj, proj, proj, bias)


def _store_row_tiles(ref, val):
    n, d = val.shape
    sub = d // LANES
    for k in range(sub):
        ref[pl.ds(k, n, stride=sub), :] = val[:, k * LANES:(k + 1) * LANES]


def _load_row_tiles(ref, n, d):
    sub = d // LANES
    return jnp.concatenate([ref[pl.ds(k, n, stride=sub), :] for k in range(sub)], axis=1)


def _router_kernel(x_ref, a_ref, b_ref, wo_ref, g_ref, wr_ref, x1_ref, h_ref, meta_ref, cnt_ref, carry_scr, tri_scr,
                   *, tr):
    i = pl.program_id(0)

    @pl.when(i == 0)
    def _():
        carry_scr[...] = jnp.zeros(carry_scr.shape, F32)
        r_i = lax.broadcasted_iota(I32, (tr, tr), 0)
        c_i = lax.broadcasted_iota(I32, (tr, tr), 1)
        tri_scr[...] = jnp.where(r_i < c_i, 1.0, 0.0).astype(BF16)

    x = _mixer_residual(x_ref, a_ref, b_ref, wo_ref)
    x1_ref[...] = x
    hn = x * lax.rsqrt(jnp.mean(x * x, axis=-1, keepdims=True) + EPS) * g_ref[...]
    _store_row_tiles(h_ref, hn)
    h_hi = hn.astype(BF16)
    h_lo = (hn - h_hi.astype(F32)).astype(BF16)
    w = wr_ref[...]
    w_hi = w.astype(BF16)
    w_lo = (w - w_hi.astype(F32)).astype(BF16)
    logits = _nt_dot(w_hi, h_hi) + (_nt_dot(w_hi, h_lo) + _nt_dot(w_lo, h_hi))
    eidx = lax.broadcasted_iota(I32, logits.shape, 0).astype(F32)
    m1 = jnp.max(logits, axis=0, keepdims=True)
    i1 = jnp.min(jnp.where(logits == m1, eidx, float(N_EXPERTS)), axis=0, keepdims=True)
    rest = jnp.where(eidx == i1, -jnp.inf, logits)
    m2 = jnp.max(rest, axis=0, keepdims=True)
    i2 = jnp.min(jnp.where(rest == m2, eidx, float(N_EXPERTS)), axis=0, keepdims=True)
    e = jnp.exp(m2 - m1)
    g1 = 1.0 / (1.0 + e)
    g2 = e / (1.0 + e)
    sel1 = eidx == i1
    sel2 = eidx == i2
    onehot = jnp.where(sel1 | sel2, 1.0, 0.0)
    rank = jnp.dot(onehot.astype(BF16), tri_scr[...], preferred_element_type=F32) + carry_scr[...]
    r1 = jnp.sum(jnp.where(sel1, rank, 0.0), axis=0, keepdims=True)
    r2 = jnp.sum(jnp.where(sel2, rank, 0.0), axis=0, keepdims=True)
    carry = carry_scr[...] + jnp.sum(onehot, axis=1, keepdims=True)
    carry_scr[...] = carry
    zero = jnp.zeros_like(g1)
    meta_ref[0] = jnp.concatenate([i1, i2, r1, r2, g1, g2, zero, zero], axis=0)
    cnt_ref[...] = jnp.broadcast_to(carry, cnt_ref.shape)


def router(x2, a, b, w_out, g, w_router, tr=512):
    t, d = x2.shape
    tr = min(tr, t)
    sub = d // LANES
    half = a.shape[1]
    return pl.pallas_call(
        functools.partial(_router_kernel, tr=tr),
        out_shape=(jax.ShapeDtypeStruct((t, d), F32),
                   jax.ShapeDtypeStruct((t * sub, LANES), F32),
                   jax.ShapeDtypeStruct((t // tr, 8, tr), F32),
                   jax.ShapeDtypeStruct((N_EXPERTS, LANES), F32)),
        grid=(t // tr,),
        in_specs=[
            pl.BlockSpec((tr, d), lambda i: (i, 0)),
            pl.BlockSpec((tr, half), lambda i: (i, 0)),
            pl.BlockSpec((tr, half), lambda i: (i, 0)),
            pl.BlockSpec((2 * half, d), lambda i: (0, 0)),
            pl.BlockSpec((1, d), lambda i: (0, 0)),
            pl.BlockSpec((N_EXPERTS, d), lambda i: (0, 0)),
        ],
        out_specs=(pl.BlockSpec((tr, d), lambda i: (i, 0)),
                   pl.BlockSpec((tr * sub, LANES), lambda i: (i, 0)),
                   pl.BlockSpec((1, 8, tr), lambda i: (i, 0, 0)),
                   pl.BlockSpec((N_EXPERTS, LANES), lambda i: (0, 0))),
        scratch_shapes=[pltpu.VMEM((N_EXPERTS, 1), F32), pltpu.VMEM((tr, tr), BF16)],
        compiler_params=_cparams(("arbitrary",)),
        name="router",
    )(x2, a, b, w_out, g.reshape(1, d), w_router.T.astype(F32))


ROW_DMA_UNROLL = 8


def _rows(ref, row, sub):
    return ref.at[pl.ds(pl.multiple_of(row * sub, sub), sub)]


def _scatter_rows_kernel(ends_ref, idx1_ref, idx2_ref, src_ref, dst_ref, zero_scr, sem, *, rows, sub, pad_rows):
    @pl.when(pl.program_id(0) == 0)
    def _():
        zero_scr[...] = jnp.zeros(zero_scr.shape, zero_scr.dtype)

        def tail(e):
            start = jnp.maximum(ends_ref[e] - pad_rows, 0)
            return pltpu.make_async_copy(zero_scr, dst_ref.at[pl.ds(pl.multiple_of(start * sub, sub),
                                                                  pad_rows * sub)], sem)

        n_exp = ends_ref.shape[0]
        for e in range(n_exp):
            tail(e).start()
        for e in range(n_exp):
            tail(e).wait()
        n_rows = dst_ref.shape[0] // sub
        for t in range(n_exp):
            start = ends_ref[n_exp - 1] + t * pad_rows

            @pl.when(start < n_rows)
            def _():
                spare = pltpu.make_async_copy(
                    zero_scr, dst_ref.at[pl.ds(pl.multiple_of(start * sub, sub), pad_rows * sub)], sem)
                spare.start()
                spare.wait()

    def issue(r, carry):
        pltpu.make_async_copy(_rows(src_ref, r, sub), _rows(dst_ref, idx1_ref[0, 0, r], sub), sem).start()
        pltpu.make_async_copy(_rows(src_ref, r, sub), _rows(dst_ref, idx2_ref[0, 0, r], sub), sem).start(priority=1)
        return carry

    lax.fori_loop(0, rows, issue, 0, unroll=ROW_DMA_UNROLL)
    for _ in range(2):
        pltpu.make_async_copy(src_ref, dst_ref.at[pl.ds(0, rows * sub)], sem).wait()


def scatter_rows(src, idx1, idx2, ends, n_rows, pad_rows, sub, rows=512):
    n = idx1.shape[0]
    rows = min(rows, n)
    assert n % rows == 0 and rows % ROW_DMA_UNROLL == 0 and n_rows >= max(rows, pad_rows)
    idx_spec = pl.BlockSpec((1, 1, rows), lambda i, ends: (i, 0, 0), memory_space=pltpu.SMEM)
    grid_spec = pltpu.PrefetchScalarGridSpec(
        num_scalar_prefetch=1,
        grid=(n // rows,),
        in_specs=[idx_spec, idx_spec, pl.BlockSpec((rows * sub, LANES), lambda i, ends: (i, 0))],
        out_specs=pl.BlockSpec(memory_space=pl.ANY),
        scratch_shapes=[pltpu.VMEM((pad_rows * sub, LANES), src.dtype), pltpu.SemaphoreType.DMA(())],
    )
    return pl.pallas_call(
        functools.partial(_scatter_rows_kernel, rows=rows, sub=sub, pad_rows=pad_rows),
        out_shape=jax.ShapeDtypeStruct((n_rows * sub, LANES), src.dtype),
        grid_spec=grid_spec,
        compiler_params=_cparams(("arbitrary",)),
        name="row_scatter",
    )(ends, idx1.reshape(n // rows, 1, rows), idx2.reshape(n // rows, 1, rows), src)


def _expert_ffn_kernel(te_ref, nt_ref, x_ref, wg_ref, wu_ref, wd_ref, o_ref, xn_scr, h_scr, acc_scr, *, ff_chunk):
    n = pl.program_id(0)
    f = pl.program_id(1)
    tm, d = acc_scr.shape

    @pl.when(n < nt_ref[0])
    def _():
        @pl.when(f == 0)
        def _():
            xn_scr[...] = _load_row_tiles(x_ref, tm, d).astype(BF16)

        _swiglu_chunks(xn_scr[...], wg_ref.at[0], wu_ref.at[0], h_scr, ff_chunk)
        part = jnp.dot(h_scr[...], wd_ref[0], preferred_element_type=F32)

        @pl.when(f == 0)
        def _():
            acc_scr[...] = part

        @pl.when(f > 0)
        def _():
            acc_scr[...] += part

        @pl.when(f == pl.num_programs(1) - 1)
        def _():
            _store_row_tiles(o_ref, acc_scr[...])

    @pl.when((n >= nt_ref[0]) & (f == pl.num_programs(1) - 1))
    def _():
        o_ref[...] = jnp.zeros(o_ref.shape, o_ref.dtype)


def expert_ffn(xs, tile_expert, n_tiles_used, wg, wu, wd, tm, ff_split=2):
    d, d_ff = wg.shape[1], wg.shape[2]
    sub = d // LANES
    nr = xs.shape[0] // sub
    ff_blk = d_ff // ff_split
    ff_chunk = 256
    assert nr % tm == 0 and ff_blk % ff_chunk == 0
    grid_spec = pltpu.PrefetchScalarGridSpec(
        num_scalar_prefetch=2,
        grid=(nr // tm, ff_split),
        in_specs=[
            pl.BlockSpec((tm * sub, LANES), lambda n, f, te, nt: (jnp.minimum(n, nt[0] - 1), 0)),
            pl.BlockSpec((1, d, ff_blk), lambda n, f, te, nt: (te[n], 0, f)),
            pl.BlockSpec((1, d, ff_blk), lambda n, f, te, nt: (te[n], 0, f)),
            pl.BlockSpec((1, ff_blk, d), lambda n, f, te, nt: (te[n], f, 0)),
        ],
        out_specs=pl.BlockSpec((tm * sub, LANES), lambda n, f, te, nt: (n, 0)),
        scratch_shapes=[pltpu.VMEM((tm, d), BF16), pltpu.VMEM((tm, ff_blk), BF16), pltpu.VMEM((tm, d), F32)],
    )
    return pl.pallas_call(
        functools.partial(_expert_ffn_kernel, ff_chunk=ff_chunk),
        out_shape=jax.ShapeDtypeStruct((nr * sub, LANES), F32),
        grid_spec=grid_spec,
        compiler_params=_cparams(("arbitrary", "arbitrary")),
        name="expert_ffn",
    )(tile_expert, n_tiles_used, xs, wg, wu, wd)


def _combine_kernel(idx1_ref, idx2_ref, x_ref, ys_ref, g1_ref, g2_ref, gn_ref, o_ref, y1_buf, y2_buf, sem, *,
                    final_norm):
    tm, d = x_ref.shape
    sub = d // LANES

    def issue(r, carry):
        pltpu.make_async_copy(_rows(ys_ref, idx1_ref[0, 0, r], sub), _rows(y1_buf, r, sub), sem).start()
        pltpu.make_async_copy(_rows(ys_ref, idx2_ref[0, 0, r], sub), _rows(y2_buf, r, sub), sem).start(priority=1)
        return carry

    lax.fori_loop(0, tm, issue, 0, unroll=ROW_DMA_UNROLL)
    for buf in (y1_buf, y2_buf):
        pltpu.make_async_copy(ys_ref.at[pl.ds(0, tm * sub)], buf, sem).wait()
    y = (x_ref[...] + g1_ref[...] * _load_row_tiles(y1_buf, tm, d)
         + g2_ref[...] * _load_row_tiles(y2_buf, tm, d))
    if final_norm:
        y = y * lax.rsqrt(jnp.mean(y * y, axis=-1, keepdims=True) + EPS) * gn_ref[...]
    o_ref[...] = y


def combine(x2, ys, idx1, idx2, g1, g2, gn, final_norm, tm=512):
    t, d = x2.shape
    tm = min(tm, t)
    sub = d // LANES
    assert t % tm == 0 and tm % ROW_DMA_UNROLL == 0 and ys.shape[0] >= tm * sub
    big = pl.BlockSpec((tm, d), lambda i: (i, 0))
    col = pl.BlockSpec((tm, 1), lambda i: (i, 0))
    idx_spec = pl.BlockSpec((1, 1, tm), lambda i: (i, 0, 0), memory_space=pltpu.SMEM)
    return pl.pallas_call(
        functools.partial(_combine_kernel, final_norm=final_norm),
        out_shape=jax.ShapeDtypeStruct((t, d), F32),
        grid=(t // tm,),
        in_specs=[idx_spec, idx_spec, big, pl.BlockSpec(memory_space=pl.ANY), col, col,
                  pl.BlockSpec((1, d), lambda i: (0, 0))],
        out_specs=big,
        scratch_shapes=[pltpu.VMEM((tm * sub, LANES), F32), pltpu.VMEM((tm * sub, LANES), F32),
                        pltpu.SemaphoreType.DMA(())],
        compiler_params=_cparams(("arbitrary",)),
        name="moe_combine",
    )(idx1.reshape(t // tm, 1, tm), idx2.reshape(t // tm, 1, tm), x2, ys, g1, g2, gn.reshape(1, d))


def moe_residual(x2, a, b, w_out, g, w_router, wg, wu, wd, expert_base, final_gain, final_norm, tm=512):
    t, d = x2.shape
    tm = min(tm, t)
    x2, hn, meta, counts = router(x2, a, b, w_out, g, w_router)
    tr = meta.shape[2]
    field = lambda r: meta[:, r, :].reshape(t)
    idx1, idx2 = field(0).astype(I32), field(1).astype(I32)
    rank1, rank2 = field(2).astype(I32), field(3).astype(I32)
    gate1, gate2 = field(4), field(5)
    cnt = counts[:, 0].astype(I32)
    padded = ((cnt + tm - 1) // tm) * tm
    ends = jnp.cumsum(padded)
    offs = ends - padded
    pos1 = offs[idx1] + rank1
    pos2 = offs[idx2] + rank2
    n_rows = TOP_K * t + N_EXPERTS * tm
    tile_start = jnp.arange(n_rows // tm, dtype=I32) * tm
    tile_expert = jnp.minimum(jnp.sum(tile_start[:, None] >= ends[None, :], axis=1), N_EXPERTS - 1).astype(I32)
    tile_expert = tile_expert + expert_base
    n_tiles_used = (ends[-1:] // tm).astype(I32)

    sub = d // LANES
    xs = scatter_rows(hn, pos1, pos2, ends.astype(I32), n_rows, tm, sub)
    ys = expert_ffn(xs, tile_expert, n_tiles_used, wg, wu, wd, tm)
    return combine(x2, ys, pos1, pos2, gate1.reshape(t, 1), gate2.reshape(t, 1), final_gain, final_norm)


def kernel(x, ev_norm_mix, ev_w_in, ev_lambda_q1, ev_lambda_k1, ev_lambda_q2, ev_lambda_k2, ev_subln, ev_w_out,
           ev_norm_ffn, ev_ffn_gate, ev_ffn_up, ev_ffn_down, od_norm_mix, od_w_in, od_conv_w, od_conv_b,
           od_conv_ln_g, od_conv_ln_b, od_rel_bias, od_w_out, od_norm_ffn, od_router, od_exp_gate, od_exp_up,
           od_exp_down, final_norm):
    batch, seq, d = x.shape
    depth = 2 * ev_w_in.shape[0]
    assert od_w_in.shape[0] * 2 == depth
    x2 = x.reshape(batch * seq, d)
    bf = lambda a: a.astype(BF16)
    a_qk = A_HEADS * 2 * HEAD_DIM
    a_v = A_HEADS * 2 * HEAD_DIM
    b_w = B_HEADS * HEAD_DIM
    attn_k_block = 256
    all_experts = lambda w: bf(w).reshape((-1,) + w.shape[2:])
    exp_gate, exp_up, exp_down = all_experts(od_exp_gate), all_experts(od_exp_up), all_experts(od_exp_down)
    for layer in range(depth):
        i = layer // 2
        if layer % 2 == 0:
            lambda_init = 0.8 - 0.6 * math.exp(-0.3 * layer)
            w = ev_w_in[i]
            va0, qb0 = 2 * a_qk, 2 * a_qk + a_v
            vb0 = qb0 + 2 * b_w
            w_qk = bf(jnp.concatenate([w[:, 0:va0], w[:, qb0:vb0]], axis=1))
            w_vt = bf(jnp.concatenate([w[:, va0:qb0], w[:, vb0:]], axis=1).T)
            proj, vt = norm_proj(x2, ev_norm_mix[i], w_qk, seq, rope_cols=2 * a_qk, wvt_bf16=w_vt,
                                 tk=attn_k_block)
            oa = diff_attention(proj, vt, ev_lambda_q1[i], ev_lambda_k1[i], ev_lambda_q2[i], ev_lambda_k2[i],
                                ev_subln[i], batch, seq, lambda_init)
            ob = stick_breaking(proj, vt, batch, seq)
            x2 = ffn_residual(x2, oa, ob, bf(ev_w_out[i]), ev_norm_ffn[i], bf(ev_ffn_gate[i]), bf(ev_ffn_up[i]),
                              bf(ev_ffn_down[i]))
        else:
            proj = norm_proj(x2, od_norm_mix[i], bf(od_w_in[i]), seq)
            c = conv_module(proj, od_conv_w[i], od_conv_b[i], od_conv_ln_g[i], od_conv_ln_b[i], seq)
            od = chunk_rel_attention(proj, rel_bias(od_rel_bias[i]), batch, seq)
            x2 = moe_residual(x2, c, od, bf(od_w_out[i]), od_norm_ffn[i], od_router[i], exp_gate, exp_up, exp_down,
                              i * N_EXPERTS, final_norm, final_norm=(layer == depth - 1))
    return x2.reshape(batch, seq, d)
```

```python
import functools
import math

import numpy as np
import jax
import jax.numpy as jnp
from jax import lax
from jax.experimental import pallas as pl
from jax.experimental.pallas import tpu as pltpu

F32 = jnp.float32
BF16 = jnp.bfloat16
I32 = jnp.int32

HEAD_DIM = 64
CHUNK = 64
ROPE_THETA = 10000.0
EPS = 1e-6
A_HEADS = 4
B_HEADS = 8
C_CHANNELS = 512
CONV_WIDTH = 31
D_HEADS = 8
LEFT_CHUNKS = 8
BAND = (LEFT_CHUNKS + 1) * CHUNK
MAX_REL = 128
N_EXPERTS = 8
TOP_K = 2

LANES = 128
SUBLANES = 8
NEG = -1e30
LOG2E = math.log2(math.e)
SIGN_BIT = np.int32(-2 ** 31)
VMEM_LIMIT = 56 * 1024 * 1024


def _cparams(sem):
    return pltpu.CompilerParams(dimension_semantics=sem, vmem_limit_bytes=VMEM_LIMIT)


def _nt_dot(a, b):
    return lax.dot_general(a, b, (((1,), (1,)), ((), ())), preferred_element_type=F32)


def _norm_proj_kernel(x_ref, g_ref, w_ref, cos_ref, sin_ref, *rest, col_chunk, rope_cols, tk):
    if tk:
        wvt_ref, o_ref, vt_ref = rest
    else:
        (o_ref,) = rest
    x = x_ref[...]
    xn = (x * lax.rsqrt(jnp.mean(x * x, axis=-1, keepdims=True) + EPS) * g_ref[...]).astype(BF16)
    n_out = o_ref.shape[1]
    if tk:
        for r0 in range(0, wvt_ref.shape[0], col_chunk):
            res = _nt_dot(wvt_ref[r0:r0 + col_chunk, :], xn)
            for b in range(vt_ref.shape[0]):
                vt_ref[b, r0:r0 + col_chunk, :] = res[:, b * tk:(b + 1) * tk].astype(BF16)
    if rope_cols:
        cos = cos_ref[...]
        sin = sin_ref[...]
        lane = lax.broadcasted_iota(I32, cos.shape, 1)
        first_half = (lane % HEAD_DIM) < (HEAD_DIM // 2)
    for c0 in range(0, n_out, col_chunk):
        r = jnp.dot(xn, w_ref[:, c0:c0 + col_chunk], preferred_element_type=F32)
        if c0 < rope_cols:
            parts = []
            for l0 in range(0, col_chunk, LANES):
                seg = r[:, l0:l0 + LANES]
                partner = jnp.where(first_half,
                                    pltpu.roll(seg, LANES - HEAD_DIM // 2, 1),
                                    pltpu.roll(seg, HEAD_DIM // 2, 1))
                parts.append(seg * cos + partner * sin)
            r = jnp.concatenate(parts, axis=1)
        o_ref[:, c0:c0 + col_chunk] = r.astype(o_ref.dtype)


def _rope_tables(seq):
    half = HEAD_DIM // 2
    inv_freq = ROPE_THETA ** (-jnp.arange(half, dtype=F32) * 2.0 / HEAD_DIM)
    ang = jnp.arange(seq, dtype=F32)[:, None] * inv_freq[None, :]
    cos, sin = jnp.cos(ang), jnp.sin(ang)
    cos_t = jnp.tile(jnp.concatenate([cos, cos], axis=1), (1, LANES // HEAD_DIM))
    sin_t = jnp.tile(jnp.concatenate([-sin, sin], axis=1), (1, LANES // HEAD_DIM))
    return cos_t, sin_t


def norm_proj(x2, g, w_bf16, seq, rope_cols=0, wvt_bf16=None, tk=0, tm=512):
    t, d = x2.shape
    n_out = w_bf16.shape[1]
    tm = min(tm, seq)
    col_chunk = 512
    assert t % tm == 0 and seq % tm == 0 and n_out % col_chunk == 0 and rope_cols % col_chunk == 0
    cos_t, sin_t = _rope_tables(seq)
    tiles_per_seq = seq // tm
    in_specs = [
        pl.BlockSpec((tm, d), lambda i: (i, 0)),
        pl.BlockSpec((1, d), lambda i: (0, 0)),
        pl.BlockSpec((d, n_out), lambda i: (0, 0)),
        pl.BlockSpec((tm, LANES), lambda i: (i % tiles_per_seq, 0)),
        pl.BlockSpec((tm, LANES), lambda i: (i % tiles_per_seq, 0)),
    ]
    args = [x2, g.reshape(1, d), w_bf16, cos_t, sin_t]
    out_shape = jax.ShapeDtypeStruct((t, n_out), BF16)
    out_specs = pl.BlockSpec((tm, n_out), lambda i: (i, 0))
    if wvt_bf16 is not None:
        n_v = wvt_bf16.shape[0]
        tk = min(tk, tm)
        assert tm % tk == 0 and n_v % col_chunk == 0
        in_specs.append(pl.BlockSpec((n_v, d), lambda i: (0, 0)))
        args.append(wvt_bf16)
        out_shape = (out_shape, jax.ShapeDtypeStruct((t // tk, n_v, tk), BF16))
        out_specs = (out_specs, pl.BlockSpec((tm // tk, n_v, tk), lambda i: (i, 0, 0)))
    return pl.pallas_call(
        functools.partial(_norm_proj_kernel, col_chunk=col_chunk, rope_cols=rope_cols,
                          tk=tk if wvt_bf16 is not None else 0),
        out_shape=out_shape,
        grid=(t // tm,),
        in_specs=in_specs,
        out_specs=out_specs,
        compiler_params=_cparams(("parallel",)),
        name="norm_proj",
    )(*args)


def _stack_query_pair(q_ref, q2_scr, tq):
    lane = lax.broadcasted_iota(I32, (tq, LANES), 1)
    qs = (q_ref[...].astype(F32) * (HEAD_DIM ** -0.5 * LOG2E)).astype(BF16)
    zero = jnp.zeros_like(qs)
    q2_scr[0:tq, :] = jnp.where(lane < HEAD_DIM, qs, zero)
    q2_scr[tq:, :] = jnp.where(lane >= HEAD_DIM, qs, zero)


def _diag_visible(tq, tk, rule):
    key = np.arange(tq)[:, None]
    qry = (np.arange(2 * tq) % tq)[None, :]
    return rule(key, qry).reshape(tq // tk, tk, 2 * tq)


def _diff_attn_kernel(q_ref, k_ref, vt_ref, mask_ref, lq1_ref, lk1_ref, lq2_ref, lk2_ref, g_ref, o_ref,
                      q2_scr, m_scr, l_scr, acc_scr, st0_scr, st1_scr, *, tq, tk, lambda_init):
    i = pl.program_id(2)
    assert tq == 2 * tk
    _stack_query_pair(q_ref, q2_scr, tq)
    m_scr[...] = jnp.full(m_scr.shape, NEG, F32)
    l_scr[...] = jnp.zeros(l_scr.shape, F32)
    late_queries = (slice(tk, tq), slice(tq + tk, 2 * tq))
    acc_scr[...] = jnp.zeros(acc_scr.shape, F32)

    def scores(j, st_scr):
        k = k_ref[pl.ds(pl.multiple_of(j * tk, tk), tk), :]
        st_scr[...] = _nt_dot(k, q2_scr[...])

    def consume(j, st_scr, diag=None, cols=(slice(None),)):
        for c in cols:
            st = st_scr[:, c]
            if diag is not None:
                st = st + mask_ref[diag, :, c]
            m_old = m_scr[:, c]
            m_new = jnp.maximum(m_old, jnp.max(st, axis=0, keepdims=True))
            alpha = jnp.exp2(m_old - m_new)
            pt = jnp.exp2(st - m_new)
            l_scr[:, c] = alpha * l_scr[:, c] + jnp.sum(pt, axis=0, keepdims=True)
            acc_scr[:, c] = alpha * acc_scr[:, c] + jnp.dot(vt_ref[j], pt.astype(BF16),
                                                            preferred_element_type=F32)
            m_scr[:, c] = m_new

    scores(0, st0_scr)

    def body(n, carry):
        scores(2 * n + 1, st1_scr)
        consume(2 * n, st0_scr)
        scores(2 * n + 2, st0_scr)
        consume(2 * n + 1, st1_scr)
        return carry

    lax.fori_loop(0, i, body, 0)
    scores(2 * i + 1, st1_scr)
    consume(2 * i, st0_scr, diag=0)
    consume(2 * i + 1, st1_scr, diag=1, cols=late_queries)

    ot = acc_scr[...] / l_scr[...]
    lam = (jnp.exp(jnp.sum(lq1_ref[...] * lk1_ref[...], axis=1, keepdims=True))
           - jnp.exp(jnp.sum(lq2_ref[...] * lk2_ref[...], axis=1, keepdims=True)) + lambda_init)
    od = ot[:, 0:tq].T - lam * ot[:, tq:].T
    y = od * lax.rsqrt(jnp.mean(od * od, axis=-1, keepdims=True) + EPS) * g_ref[...]
    o_ref[...] = (y * (1.0 - lambda_init)).astype(o_ref.dtype)


def diff_attention(proj, vt, lq1, lk1, lq2, lk2, subln_g, batch, seq, lambda_init):
    t = proj.shape[0]
    tk = vt.shape[2]
    tq = 2 * tk
    nq = seq // tq
    assert seq % tq == 0 and tk % CHUNK == 0
    k_off = (A_HEADS * 2 * HEAD_DIM) // LANES
    vec = lambda a: a.reshape(1, -1).astype(F32)
    small = lambda n: pl.BlockSpec((1, n), lambda b, h, i: (0, 0))
    mask = jnp.asarray(np.where(_diag_visible(tq, tk, lambda key, qry: key // CHUNK <= qry // CHUNK), 0.0, NEG), F32)
    return pl.pallas_call(
        functools.partial(_diff_attn_kernel, tq=tq, tk=tk, lambda_init=lambda_init),
        out_shape=jax.ShapeDtypeStruct((t, A_HEADS * 2 * HEAD_DIM), BF16),
        grid=(batch, A_HEADS, nq),
        in_specs=[
            pl.BlockSpec((tq, LANES), lambda b, h, i: (b * nq + i, h)),
            pl.BlockSpec((seq, LANES), lambda b, h, i: (b, k_off + h)),
            pl.BlockSpec((seq // tk, LANES, tk), lambda b, h, i: (b, h, 0)),
            pl.BlockSpec((tq // tk, tk, 2 * tq), lambda b, h, i: (0, 0, 0)),
            small(HEAD_DIM), small(HEAD_DIM), small(HEAD_DIM), small(HEAD_DIM), small(2 * HEAD_DIM),
        ],
        out_specs=pl.BlockSpec((tq, LANES), lambda b, h, i: (b * nq + i, h)),
        scratch_shapes=[
            pltpu.VMEM((2 * tq, LANES), BF16),
            pltpu.VMEM((1, 2 * tq), F32),
            pltpu.VMEM((1, 2 * tq), F32),
            pltpu.VMEM((LANES, 2 * tq), F32),
            pltpu.VMEM((tk, 2 * tq), F32),
            pltpu.VMEM((tk, 2 * tq), F32),
        ],
        compiler_params=_cparams(("parallel", "parallel", "arbitrary")),
        name="diff_attn",
    )(proj, proj, vt, mask, vec(lq1), vec(lk1), vec(lq2), vec(lk2), vec(subln_g))


def _stick_kernel(q_ref, k_ref, vt_ref, mask_ref, o_ref, q2_scr, tri_scr, c_scr, acc_scr, st0_scr, st1_scr, *, tq, tk):
    i = pl.program_id(2)
    assert tq == 2 * tk
    _stack_query_pair(q_ref, q2_scr, tq)
    r_i = lax.broadcasted_iota(I32, (tk, tk), 0)
    c_i = lax.broadcasted_iota(I32, (tk, tk), 1)
    tri_scr[...] = jnp.where(c_i > r_i, 1.0, 0.0).astype(BF16)
    c_scr[...] = jnp.zeros(c_scr.shape, F32)
    acc_scr[...] = jnp.zeros(acc_scr.shape, F32)

    def scores(j, st_scr):
        k = k_ref[pl.ds(pl.multiple_of(j * tk, tk), tk), :]
        st_scr[...] = _nt_dot(k, q2_scr[...])

    def consume(j, st_scr, diag=None, cols=(slice(None),)):
        for c in cols:
            zt = st_scr[:, c]
            neg_abs = lax.bitcast_convert_type(lax.bitcast_convert_type(zt, I32) | SIGN_BIT, F32)
            sp = jnp.maximum(zt, 0.0) + jnp.log2(1.0 + jnp.exp2(neg_abs))
            if diag is not None:
                visible = mask_ref[diag, :, c]
                spm = sp * visible
            else:
                spm = sp
            after = jnp.dot(tri_scr[...], spm.astype(BF16), preferred_element_type=F32)
            w = jnp.exp2(zt - sp - after - c_scr[:, c])
            if diag is not None:
                w = w * visible
            acc_scr[:, c] += jnp.dot(vt_ref[j], w.astype(BF16), preferred_element_type=F32)
            c_scr[:, c] += jnp.sum(spm, axis=0, keepdims=True)

    scores(2 * i + 1, st0_scr)
    scores(2 * i, st1_scr)
    consume(2 * i + 1, st0_scr, diag=1, cols=(slice(tk, tq), slice(tq + tk, 2 * tq)))
    scores(jnp.maximum(2 * i - 1, 0), st0_scr)
    consume(2 * i, st1_scr, diag=0)

    def body(n, carry):
        j = 2 * (i - n) - 1
        scores(j - 1, st1_scr)
        consume(j, st0_scr)
        scores(jnp.maximum(j - 2, 0), st0_scr)
        consume(j - 1, st1_scr)
        return carry

    lax.fori_loop(0, i, body, 0)
    acc = acc_scr[...]
    vrow = lax.broadcasted_iota(I32, (LANES, tq), 0)
    ot = jnp.where(vrow < HEAD_DIM, acc[:, 0:tq], acc[:, tq:])
    o_ref[...] = ot.T.astype(o_ref.dtype)


def stick_breaking(proj, vt, batch, seq):
    t = proj.shape[0]
    tk = vt.shape[2]
    tq = 2 * tk
    nq = seq // tq
    assert seq % tq == 0
    pairs = (B_HEADS * HEAD_DIM) // LANES
    a_blocks = (A_HEADS * 2 * HEAD_DIM) // LANES
    q_off = 2 * a_blocks
    k_off = q_off + pairs
    v_off = a_blocks
    mask = jnp.asarray(_diag_visible(tq, tk, lambda key, qry: key < qry), F32)
    return pl.pallas_call(
        functools.partial(_stick_kernel, tq=tq, tk=tk),
        out_shape=jax.ShapeDtypeStruct((t, B_HEADS * HEAD_DIM), BF16),
        grid=(batch, pairs, nq),
        in_specs=[
            pl.BlockSpec((tq, LANES), lambda b, p, i: (b * nq + i, q_off + p)),
            pl.BlockSpec((seq, LANES), lambda b, p, i: (b, k_off + p)),
            pl.BlockSpec((seq // tk, LANES, tk), lambda b, p, i: (b, v_off + p, 0)),
            pl.BlockSpec((tq // tk, tk, 2 * tq), lambda b, p, i: (0, 0, 0)),
        ],
        out_specs=pl.BlockSpec((tq, LANES), lambda b, p, i: (b * nq + i, p)),
        scratch_shapes=[
            pltpu.VMEM((2 * tq, LANES), BF16),
            pltpu.VMEM((tk, tk), BF16),
            pltpu.VMEM((1, 2 * tq), F32),
            pltpu.VMEM((LANES, 2 * tq), F32),
            pltpu.VMEM((tk, 2 * tq), F32),
            pltpu.VMEM((tk, 2 * tq), F32),
        ],
        compiler_params=_cparams(("parallel", "parallel", "arbitrary")),
        name="stick_attn",
    )(proj, proj, vt, mask)


def _mixer_residual(x_ref, a_ref, b_ref, wo_ref):
    half = a_ref.shape[1]
    return (x_ref[...]
            + jnp.dot(a_ref[...], wo_ref[0:half, :], preferred_element_type=F32)
            + jnp.dot(b_ref[...], wo_ref[half:, :], preferred_element_type=F32))


def _swiglu_chunks(xn, wg_ref, wu_ref, h_scr, ff_chunk):
    d_ff = h_scr.shape[1]
    for c0 in range(0, d_ff, ff_chunk):
        g = jnp.dot(xn, wg_ref[:, c0:c0 + ff_chunk], preferred_element_type=F32)
        u = jnp.dot(xn, wu_ref[:, c0:c0 + ff_chunk], preferred_element_type=F32)
        h_scr[:, c0:c0 + ff_chunk] = (g * jax.nn.sigmoid(g) * u).astype(BF16)


def _ffn_kernel(x_ref, a_ref, b_ref, wo_ref, g_ref, wg_ref, wu_ref, wd_ref, o_ref, h_scr, *, ff_chunk):
    x = _mixer_residual(x_ref, a_ref, b_ref, wo_ref)
    xn = (x * lax.rsqrt(jnp.mean(x * x, axis=-1, keepdims=True) + EPS) * g_ref[...]).astype(BF16)
    _swiglu_chunks(xn, wg_ref, wu_ref, h_scr, ff_chunk)
    o_ref[...] = x + jnp.dot(h_scr[...], wd_ref[...], preferred_element_type=F32)


def ffn_residual(x2, a, b, w_out, g, wg, wu, wd, tm=512):
    t, d = x2.shape
    half = a.shape[1]
    d_ff = wg.shape[1]
    tm = min(tm, t)
    ff_chunk = 256
    assert d_ff % ff_chunk == 0
    resident = lambda shape: pl.BlockSpec(shape, lambda i: (0, 0), pipeline_mode=pl.Buffered(1))
    return pl.pallas_call(
        functools.partial(_ffn_kernel, ff_chunk=ff_chunk),
        out_shape=jax.ShapeDtypeStruct((t, d), F32),
        grid=(t // tm,),
        in_specs=[
            pl.BlockSpec((tm, d), lambda i: (i, 0)),
            pl.BlockSpec((tm, half), lambda i: (i, 0)),
            pl.BlockSpec((tm, half), lambda i: (i, 0)),
            resident((2 * half, d)),
            pl.BlockSpec((1, d), lambda i: (0, 0)),
            resident((d, d_ff)), resident((d, d_ff)), resident((d_ff, d)),
        ],
        out_specs=pl.BlockSpec((tm, d), lambda i: (i, 0)),
        scratch_shapes=[pltpu.VMEM((tm, d_ff), BF16)],
        compiler_params=_cparams(("parallel",)),
        name="ffn",
    )(x2, a, b, w_out, g.reshape(1, d), wg, wu, wd)


CONV_HALO = 32


def _conv_kernel(val_ref, gate_ref, pval_ref, pgate_ref, w_ref, b_ref, lg_ref, lb_ref, o_ref,
                 u_scr, ush_scr, c_scr, *, ts, tiles_per_seq, row_blk):
    i = pl.program_id(0)
    glu = lambda v, g: v.astype(F32) * jax.nn.sigmoid(g.astype(F32))
    keep = jnp.where(i % tiles_per_seq == 0, 0.0, 1.0)
    u_scr[0:CONV_HALO, :] = glu(pval_ref[...], pgate_ref[...]) * keep
    u_scr[CONV_HALO:, :] = glu(val_ref[...], gate_ref[...])
    shift = CONV_HALO - (CONV_WIDTH - 1)
    n_ch = val_ref.shape[1]
    n_sh = ush_scr.shape[1]
    for b in range(1, SUBLANES):
        ush_scr[b - 1] = u_scr[b:b + n_sh, :]
    for c0 in range(0, n_ch, LANES):
        for r0 in range(0, ts, row_blk):
            acc = jnp.broadcast_to(b_ref[:, c0:c0 + LANES], (row_blk, LANES))
            for j in range(CONV_WIDTH):
                a, b = divmod(shift + j, SUBLANES)
                src = u_scr if b == 0 else ush_scr.at[b - 1]
                r = r0 + a * SUBLANES
                acc = acc + w_ref[j:j + 1, c0:c0 + LANES] * src[r:r + row_blk, c0:c0 + LANES]
            c_scr[r0:r0 + row_blk, c0:c0 + LANES] = acc
    c = c_scr[...]
    mu = jnp.mean(c, axis=-1, keepdims=True)
    xc = c - mu
    var = jnp.mean(xc * xc, axis=-1, keepdims=True)
    y = xc * lax.rsqrt(var + EPS) * lg_ref[...] + lb_ref[...]
    o_ref[...] = (y * jax.nn.sigmoid(y)).astype(o_ref.dtype)


def conv_module(proj, conv_w, conv_b, ln_g, ln_b, seq, ts=256):
    t = proj.shape[0]
    ts = min(ts, seq)
    n_ch = C_CHANNELS
    halo_blocks = ts // CONV_HALO
    row = lambda a: a.reshape(1, n_ch).astype(F32)
    small = pl.BlockSpec((1, n_ch), lambda i: (0, 0))
    prev = lambda col: pl.BlockSpec((CONV_HALO, n_ch), lambda i: (jnp.maximum(i * halo_blocks - 1, 0), col))
    return pl.pallas_call(
        functools.partial(_conv_kernel, ts=ts, tiles_per_seq=seq // ts, row_blk=64),
        out_shape=jax.ShapeDtypeStruct((t, n_ch), BF16),
        grid=(t // ts,),
        in_specs=[
            pl.BlockSpec((ts, n_ch), lambda i: (i, 0)),
            pl.BlockSpec((ts, n_ch), lambda i: (i, 1)),
            prev(0), prev(1),
            pl.BlockSpec((CONV_WIDTH, n_ch), lambda i: (0, 0)),
            small, small, small,
        ],
        out_specs=pl.BlockSpec((ts, n_ch), lambda i: (i, 0)),
        scratch_shapes=[pltpu.VMEM((ts + CONV_HALO, n_ch), F32),
                        pltpu.VMEM((SUBLANES - 1, ts + CONV_HALO - SUBLANES, n_ch), F32),
                        pltpu.VMEM((ts, n_ch), F32)],
        compiler_params=_cparams(("parallel",)),
        name="conv_module",
    )(proj, proj, proj, proj, conv_w.astype(F32), row(conv_b), row(ln_g), row(ln_b))


def _rel_bias_kernel(tbl_ref, o_ref):
    p = pl.program_id(0)
    kk = lax.broadcasted_iota(I32, (BAND, LANES), 0)
    lane = lax.broadcasted_iota(I32, (BAND, LANES), 1)
    idx = jnp.clip(lane % CHUNK - (kk - LEFT_CHUNKS * CHUNK), -MAX_REL, MAX_REL) + MAX_REL
    first = lane < CHUNK

    def body(j, acc):
        return acc + jnp.where(idx == j, jnp.where(first, tbl_ref[2 * p, j], tbl_ref[2 * p + 1, j]), 0.0)

    o_ref[0] = lax.fori_loop(0, 2 * MAX_REL + 1, body, jnp.zeros((BAND, LANES), F32)) * LOG2E


def rel_bias(rel_table):
    pairs = D_HEADS // 2
    assert 2 * CHUNK == LANES
    return pl.pallas_call(
        _rel_bias_kernel,
        out_shape=jax.ShapeDtypeStruct((pairs, BAND, LANES), F32),
        grid=(pairs,),
        in_specs=[pl.BlockSpec(memory_space=pltpu.SMEM)],
        out_specs=pl.BlockSpec((1, BAND, LANES), lambda p: (p, 0, 0)),
        compiler_params=_cparams(("parallel",)),
        name="rel_bias",
    )(rel_table.astype(F32))


def _chunk_attn_kernel(q_ref, k_ref, v_ref, bias_ref, o_ref, kpad, vpad, *, group, seq):
    i = pl.program_id(2)
    pad = LEFT_CHUNKS * CHUNK

    @pl.when(i == 0)
    def _():
        kpad[0:pad, :] = jnp.zeros((pad, LANES), BF16)
        vpad[0:pad, :] = jnp.zeros((pad, LANES), BF16)
        kpad[pad:, :] = k_ref[...]
        vpad[pad:, :] = v_ref[...]

    lane = lax.broadcasted_iota(I32, (CHUNK, LANES), 1)
    band_chunk = lax.broadcasted_iota(I32, (BAND, LANES), 0) // CHUNK
    bias = bias_ref[0]
    ones = jnp.ones((BAND, LANES), BF16)
    for g in range(group):
        c = i * group + g
        qs = (q_ref[g * CHUNK:(g + 1) * CHUNK, :].astype(F32) * (HEAD_DIM ** -0.5 * LOG2E)).astype(BF16)
        zero = jnp.zeros_like(qs)
        q2 = jnp.concatenate([jnp.where(lane < HEAD_DIM, qs, zero),
                              jnp.where(lane >= HEAD_DIM, qs, zero)], axis=0)
        start = pl.multiple_of(c * CHUNK, CHUNK)
        kw = kpad[pl.ds(start, BAND), :]
        vw = vpad[pl.ds(start, BAND), :]
        st = _nt_dot(kw, q2) + bias
        st = jnp.where(band_chunk >= LEFT_CHUNKS - c, st, NEG)
        pt = jnp.exp2(st - jnp.max(st, axis=0, keepdims=True))
        o2 = lax.dot_general(pt.astype(BF16), jnp.concatenate([vw, ones], axis=1),
                             (((0,), (0,)), ((), ())), preferred_element_type=F32)
        o = o2[:, 0:LANES] / o2[:, LANES:]
        o_ref[g * CHUNK:(g + 1) * CHUNK, :] = jnp.where(lane < HEAD_DIM, o[0:CHUNK, :],
                                                        o[CHUNK:, :]).astype(o_ref.dtype)


def chunk_rel_attention(proj, bias, batch, seq, group=16):
    t = proj.shape[0]
    pairs = D_HEADS // 2
    q_off = (2 * C_CHANNELS) // LANES
    k_off = q_off + pairs
    v_off = k_off + pairs
    tq = group * CHUNK
    nq = seq // tq
    return pl.pallas_call(
        functools.partial(_chunk_attn_kernel, group=group, seq=seq),
        out_shape=jax.ShapeDtypeStruct((t, D_HEADS * HEAD_DIM), BF16),
        grid=(batch, pairs, nq),
        in_specs=[
            pl.BlockSpec((tq, LANES), lambda b, p, i: (b * nq + i, q_off + p)),
            pl.BlockSpec((seq, LANES), lambda b, p, i: (b, k_off + p)),
            pl.BlockSpec((seq, LANES), lambda b, p, i: (b, v_off + p)),
            pl.BlockSpec((1, BAND, LANES), lambda b, p, i: (p, 0, 0)),
        ],
        out_specs=pl.BlockSpec((tq, LANES), lambda b, p, i: (b * nq + i, p)),
        scratch_shapes=[pltpu.VMEM((seq + LEFT_CHUNKS * CHUNK, LANES), BF16),
                        pltpu.VMEM((seq + LEFT_CHUNKS * CHUNK, LANES), BF16)],
        compiler_params=_cparams(("parallel", "parallel", "arbitrary")),
        name="chunk_attn",
    )(proj, proj, proj, bias)


def _store_row_tiles(ref, val):
    n, d = val.shape
    sub = d // LANES
    for k in range(sub):
        ref[pl.ds(k, n, stride=sub), :] = val[:, k * LANES:(k + 1) * LANES]


def _load_row_tiles(ref, n, d):
    sub = d // LANES
    return jnp.concatenate([ref[pl.ds(k, n, stride=sub), :] for k in range(sub)], axis=1)


def _router_kernel(x_ref, a_ref, b_ref, wo_ref, g_ref, wr_ref, x1_ref, h_ref, meta_ref, cnt_ref, carry_scr, tri_scr,
                   *, tr):
    i = pl.program_id(0)

    @pl.when(i == 0)
    def _():
        carry_scr[...] = jnp.zeros(carry_scr.shape, F32)
        r_i = lax.broadcasted_iota(I32, (tr, tr), 0)
        c_i = lax.broadcasted_iota(I32, (tr, tr), 1)
        tri_scr[...] = jnp.where(r_i < c_i, 1.0, 0.0).astype(BF16)

    x = _mixer_residual(x_ref, a_ref, b_ref, wo_ref)
    x1_ref[...] = x
    hn = x * lax.rsqrt(jnp.mean(x * x, axis=-1, keepdims=True) + EPS) * g_ref[...]
    _store_row_tiles(h_ref, hn)
    h_hi = hn.astype(BF16)
    h_lo = (hn - h_hi.astype(F32)).astype(BF16)
    w = wr_ref[...]
    w_hi = w.astype(BF16)
    w_lo = (w - w_hi.astype(F32)).astype(BF16)
    logits = _nt_dot(w_hi, h_hi) + (_nt_dot(w_hi, h_lo) + _nt_dot(w_lo, h_hi))
    eidx = lax.broadcasted_iota(I32, logits.shape, 0).astype(F32)
    m1 = jnp.max(logits, axis=0, keepdims=True)
    i1 = jnp.min(jnp.where(logits == m1, eidx, float(N_EXPERTS)), axis=0, keepdims=True)
    rest = jnp.where(eidx == i1, -jnp.inf, logits)
    m2 = jnp.max(rest, axis=0, keepdims=True)
    i2 = jnp.min(jnp.where(rest == m2, eidx, float(N_EXPERTS)), axis=0, keepdims=True)
    e = jnp.exp(m2 - m1)
    g1 = 1.0 / (1.0 + e)
    g2 = e / (1.0 + e)
    sel1 = eidx == i1
    sel2 = eidx == i2
    onehot = jnp.where(sel1 | sel2, 1.0, 0.0)
    rank = jnp.dot(onehot.astype(BF16), tri_scr[...], preferred_element_type=F32) + carry_scr[...]
    r1 = jnp.sum(jnp.where(sel1, rank, 0.0), axis=0, keepdims=True)
    r2 = jnp.sum(jnp.where(sel2, rank, 0.0), axis=0, keepdims=True)
    carry = carry_scr[...] + jnp.sum(onehot, axis=1, keepdims=True)
    carry_scr[...] = carry
    zero = jnp.zeros_like(g1)
    meta_ref[0] = jnp.concatenate([i1, i2, r1, r2, g1, g2, zero, zero], axis=0)
    cnt_ref[...] = jnp.broadcast_to(carry, cnt_ref.shape)


def router(x2, a, b, w_out, g, w_router, tr=512):
    t, d = x2.shape
    tr = min(tr, t)
    sub = d // LANES
    half = a.shape[1]
    return pl.pallas_call(
        functools.partial(_router_kernel, tr=tr),
        out_shape=(jax.ShapeDtypeStruct((t, d), F32),
                   jax.ShapeDtypeStruct((t * sub, LANES), F32),
                   jax.ShapeDtypeStruct((t // tr, 8, tr), F32),
                   jax.ShapeDtypeStruct((N_EXPERTS, LANES), F32)),
        grid=(t // tr,),
        in_specs=[
            pl.BlockSpec((tr, d), lambda i: (i, 0)),
            pl.BlockSpec((tr, half), lambda i: (i, 0)),
            pl.BlockSpec((tr, half), lambda i: (i, 0)),
            pl.BlockSpec((2 * half, d), lambda i: (0, 0)),
            pl.BlockSpec((1, d), lambda i: (0, 0)),
            pl.BlockSpec((N_EXPERTS, d), lambda i: (0, 0)),
        ],
        out_specs=(pl.BlockSpec((tr, d), lambda i: (i, 0)),
                   pl.BlockSpec((tr * sub, LANES), lambda i: (i, 0)),
                   pl.BlockSpec((1, 8, tr), lambda i: (i, 0, 0)),
                   pl.BlockSpec((N_EXPERTS, LANES), lambda i: (0, 0))),
        scratch_shapes=[pltpu.VMEM((N_EXPERTS, 1), F32), pltpu.VMEM((tr, tr), BF16)],
        compiler_params=_cparams(("arbitrary",)),
        name="router",
    )(x2, a, b, w_out, g.reshape(1, d), w_router.T.astype(F32))


ROW_DMA_UNROLL = 8


def _rows(ref, row, sub):
    return ref.at[pl.ds(pl.multiple_of(row * sub, sub), sub)]


def _scatter_rows_kernel(ends_ref, idx1_ref, idx2_ref, src_ref, dst_ref, zero_scr, sem, *, rows, sub, pad_rows):
    @pl.when(pl.program_id(0) == 0)
    def _():
        zero_scr[...] = jnp.zeros(zero_scr.shape, zero_scr.dtype)

        def tail(e):
            start = jnp.maximum(ends_ref[e] - pad_rows, 0)
            return pltpu.make_async_copy(zero_scr, dst_ref.at[pl.ds(pl.multiple_of(start * sub, sub),
                                                                  pad_rows * sub)], sem)

        n_exp = ends_ref.shape[0]
        for e in range(n_exp):
            tail(e).start()
        for e in range(n_exp):
            tail(e).wait()
        n_rows = dst_ref.shape[0] // sub
        for t in range(n_exp):
            start = ends_ref[n_exp - 1] + t * pad_rows

            @pl.when(start < n_rows)
            def _():
                spare = pltpu.make_async_copy(
                    zero_scr, dst_ref.at[pl.ds(pl.multiple_of(start * sub, sub), pad_rows * sub)], sem)
                spare.start()
                spare.wait()

    def issue(r, carry):
        pltpu.make_async_copy(_rows(src_ref, r, sub), _rows(dst_ref, idx1_ref[0, 0, r], sub), sem).start()
        pltpu.make_async_copy(_rows(src_ref, r, sub), _rows(dst_ref, idx2_ref[0, 0, r], sub), sem).start(priority=1)
        return carry

    lax.fori_loop(0, rows, issue, 0, unroll=ROW_DMA_UNROLL)
    for _ in range(2):
        pltpu.make_async_copy(src_ref, dst_ref.at[pl.ds(0, rows * sub)], sem).wait()


def scatter_rows(src, idx1, idx2, ends, n_rows, pad_rows, sub, rows=512):
    n = idx1.shape[0]
    rows = min(rows, n)
    assert n % rows == 0 and rows % ROW_DMA_UNROLL == 0 and n_rows >= max(rows, pad_rows)
    idx_spec = pl.BlockSpec((1, 1, rows), lambda i, ends: (i, 0, 0), memory_space=pltpu.SMEM)
    grid_spec = pltpu.PrefetchScalarGridSpec(
        num_scalar_prefetch=1,
        grid=(n // rows,),
        in_specs=[idx_spec, idx_spec, pl.BlockSpec((rows * sub, LANES), lambda i, ends: (i, 0))],
        out_specs=pl.BlockSpec(memory_space=pl.ANY),
        scratch_shapes=[pltpu.VMEM((pad_rows * sub, LANES), src.dtype), pltpu.SemaphoreType.DMA(())],
    )
    return pl.pallas_call(
        functools.partial(_scatter_rows_kernel, rows=rows, sub=sub, pad_rows=pad_rows),
        out_shape=jax.ShapeDtypeStruct((n_rows * sub, LANES), src.dtype),
        grid_spec=grid_spec,
        compiler_params=_cparams(("arbitrary",)),
        name="row_scatter",
    )(ends, idx1.reshape(n // rows, 1, rows), idx2.reshape(n // rows, 1, rows), src)


def _expert_ffn_kernel(te_ref, nt_ref, x_ref, wg_ref, wu_ref, wd_ref, o_ref, xn_scr, h_scr, acc_scr, *, ff_chunk, n_f):
    n = pl.program_id(0)
    f = pl.program_id(1)
    tm, d = acc_scr.shape

    @pl.when(n < nt_ref[0])
    def _():
        @pl.when(f == 0)
        def _():
            xn_scr[...] = _load_row_tiles(x_ref, tm, d).astype(BF16)

        _swiglu_chunks(xn_scr[...], wg_ref.at[0], wu_ref.at[0], h_scr, ff_chunk)
        part = jnp.dot(h_scr[...], wd_ref[0], preferred_element_type=F32)

        if n_f == 1:
            _store_row_tiles(o_ref, part)
        else:
            @pl.when(f == 0)
            def _():
                acc_scr[...] = part

            @pl.when((f > 0) & (f < n_f - 1))
            def _():
                acc_scr[...] += part

            @pl.when(f == n_f - 1)
            def _():
                _store_row_tiles(o_ref, acc_scr[...] + part)

    @pl.when((n >= nt_ref[0]) & (f == n_f - 1))
    def _():
        o_ref[...] = jnp.zeros(o_ref.shape, o_ref.dtype)


def expert_ffn(xs, tile_expert, n_tiles_used, wg, wu, wd, tm, ff_split=2):
    d, d_ff = wg.shape[1], wg.shape[2]
    sub = d // LANES
    nr = xs.shape[0] // sub
    ff_blk = d_ff // ff_split
    ff_chunk = 256
    assert nr % tm == 0 and ff_blk % ff_chunk == 0
    grid_spec = pltpu.PrefetchScalarGridSpec(
        num_scalar_prefetch=2,
        grid=(nr // tm, ff_split),
        in_specs=[
            pl.BlockSpec((tm * sub, LANES), lambda n, f, te, nt: (jnp.minimum(n, nt[0] - 1), 0)),
            pl.BlockSpec((1, d, ff_blk), lambda n, f, te, nt: (te[n], 0, f)),
            pl.BlockSpec((1, d, ff_blk), lambda n, f, te, nt: (te[n], 0, f)),
            pl.BlockSpec((1, ff_blk, d), lambda n, f, te, nt: (te[n], f, 0)),
        ],
        out_specs=pl.BlockSpec((tm * sub, LANES), lambda n, f, te, nt: (n, 0)),
        scratch_shapes=[pltpu.VMEM((tm, d), BF16), pltpu.VMEM((tm, ff_blk), BF16), pltpu.VMEM((tm, d), F32)],
    )
    return pl.pallas_call(
        functools.partial(_expert_ffn_kernel, ff_chunk=ff_chunk, n_f=ff_split),
        out_shape=jax.ShapeDtypeStruct((nr * sub, LANES), F32),
        grid_spec=grid_spec,
        compiler_params=_cparams(("arbitrary", "arbitrary")),
        name="expert_ffn",
    )(tile_expert, n_tiles_used, xs, wg, wu, wd)


def _combine_kernel(idx1_ref, idx2_ref, x_ref, ys_ref, g1_ref, g2_ref, gn_ref, o_ref, y1_buf, y2_buf, sem, *,
                    final_norm):
    tm, d = x_ref.shape
    sub = d // LANES

    def issue(r, carry):
        pltpu.make_async_copy(_rows(ys_ref, idx1_ref[0, 0, r], sub), _rows(y1_buf, r, sub), sem).start()
        pltpu.make_async_copy(_rows(ys_ref, idx2_ref[0, 0, r], sub), _rows(y2_buf, r, sub), sem).start(priority=1)
        return carry

    lax.fori_loop(0, tm, issue, 0, unroll=ROW_DMA_UNROLL)
    for buf in (y1_buf, y2_buf):
        pltpu.make_async_copy(ys_ref.at[pl.ds(0, tm * sub)], buf, sem).wait()
    y = (x_ref[...] + g1_ref[...] * _load_row_tiles(y1_buf, tm, d)
         + g2_ref[...] * _load_row_tiles(y2_buf, tm, d))
    if final_norm:
        y = y * lax.rsqrt(jnp.mean(y * y, axis=-1, keepdims=True) + EPS) * gn_ref[...]
    o_ref[...] = y


def combine(x2, ys, idx1, idx2, g1, g2, gn, final_norm, tm=512):
    t, d = x2.shape
    tm = min(tm, t)
    sub = d // LANES
    assert t % tm == 0 and tm % ROW_DMA_UNROLL == 0 and ys.shape[0] >= tm * sub
    big = pl.BlockSpec((tm, d), lambda i: (i, 0))
    col = pl.BlockSpec((tm, 1), lambda i: (i, 0))
    idx_spec = pl.BlockSpec((1, 1, tm), lambda i: (i, 0, 0), memory_space=pltpu.SMEM)
    return pl.pallas_call(
        functools.partial(_combine_kernel, final_norm=final_norm),
        out_shape=jax.ShapeDtypeStruct((t, d), F32),
        grid=(t // tm,),
        in_specs=[idx_spec, idx_spec, big, pl.BlockSpec(memory_space=pl.ANY), col, col,
                  pl.BlockSpec((1, d), lambda i: (0, 0))],
        out_specs=big,
        scratch_shapes=[pltpu.VMEM((tm * sub, LANES), F32), pltpu.VMEM((tm * sub, LANES), F32),
                        pltpu.SemaphoreType.DMA(())],
        compiler_params=_cparams(("arbitrary",)),
        name="moe_combine",
    )(idx1.reshape(t // tm, 1, tm), idx2.reshape(t // tm, 1, tm), x2, ys, g1, g2, gn.reshape(1, d))


def moe_residual(x2, a, b, w_out, g, w_router, wg, wu, wd, expert_base, final_gain, final_norm, tm=512):
    t, d = x2.shape
    tm = min(tm, t)
    x2, hn, meta, counts = router(x2, a, b, w_out, g, w_router)
    tr = meta.shape[2]
    field = lambda r: meta[:, r, :].reshape(t)
    idx1, idx2 = field(0).astype(I32), field(1).astype(I32)
    rank1, rank2 = field(2).astype(I32), field(3).astype(I32)
    gate1, gate2 = field(4), field(5)
    cnt = counts[:, 0].astype(I32)
    padded = ((cnt + tm - 1) // tm) * tm
    ends = jnp.cumsum(padded)
    offs = ends - padded
    pos1 = offs[idx1] + rank1
    pos2 = offs[idx2] + rank2
    n_rows = TOP_K * t + N_EXPERTS * tm
    tile_start = jnp.arange(n_rows // tm, dtype=I32) * tm
    tile_expert = jnp.minimum(jnp.sum(tile_start[:, None] >= ends[None, :], axis=1), N_EXPERTS - 1).astype(I32)
    tile_expert = tile_expert + expert_base
    n_tiles_used = (ends[-1:] // tm).astype(I32)

    sub = d // LANES
    xs = scatter_rows(hn, pos1, pos2, ends.astype(I32), n_rows, tm, sub)
    ys = expert_ffn(xs, tile_expert, n_tiles_used, wg, wu, wd, tm)
    return combine(x2, ys, pos1, pos2, gate1.reshape(t, 1), gate2.reshape(t, 1), final_gain, final_norm)


def kernel(x, ev_norm_mix, ev_w_in, ev_lambda_q1, ev_lambda_k1, ev_lambda_q2, ev_lambda_k2, ev_subln, ev_w_out,
           ev_norm_ffn, ev_ffn_gate, ev_ffn_up, ev_ffn_down, od_norm_mix, od_w_in, od_conv_w, od_conv_b,
           od_conv_ln_g, od_conv_ln_b, od_rel_bias, od_w_out, od_norm_ffn, od_router, od_exp_gate, od_exp_up,
           od_exp_down, final_norm):
    batch, seq, d = x.shape
    depth = 2 * ev_w_in.shape[0]
    assert od_w_in.shape[0] * 2 == depth
    x2 = x.reshape(batch * seq, d)
    bf = lambda a: a.astype(BF16)
    a_qk = A_HEADS * 2 * HEAD_DIM
    a_v = A_HEADS * 2 * HEAD_DIM
    b_w = B_HEADS * HEAD_DIM
    attn_k_block = 256
    all_experts = lambda w: bf(w).reshape((-1,) + w.shape[2:])
    exp_gate, exp_up, exp_down = all_experts(od_exp_gate), all_experts(od_exp_up), all_experts(od_exp_down)
    for layer in range(depth):
        i = layer // 2
        if layer % 2 == 0:
            lambda_init = 0.8 - 0.6 * math.exp(-0.3 * layer)
            w = ev_w_in[i]
            va0, qb0 = 2 * a_qk, 2 * a_qk + a_v
            vb0 = qb0 + 2 * b_w
            w_qk = bf(jnp.concatenate([w[:, 0:va0], w[:, qb0:vb0]], axis=1))
            w_vt = bf(jnp.concatenate([w[:, va0:qb0], w[:, vb0:]], axis=1).T)
            proj, vt = norm_proj(x2, ev_norm_mix[i], w_qk, seq, rope_cols=2 * a_qk, wvt_bf16=w_vt,
                                 tk=attn_k_block)
            oa = diff_attention(proj, vt, ev_lambda_q1[i], ev_lambda_k1[i], ev_lambda_q2[i], ev_lambda_k2[i],
                                ev_subln[i], batch, seq, lambda_init)
            ob = stick_breaking(proj, vt, batch, seq)
            x2 = ffn_residual(x2, oa, ob, bf(ev_w_out[i]), ev_norm_ffn[i], bf(ev_ffn_gate[i]), bf(ev_ffn_up[i]),
                              bf(ev_ffn_down[i]))
        else:
            proj = norm_proj(x2, od_norm_mix[i], bf(od_w_in[i]), seq)
            c = conv_module(proj, od_conv_w[i], od_conv_b[i], od_conv_ln_g[i], od_conv_ln_b[i], seq)
            od = chunk_rel_attention(proj, rel_bias(od_rel_bias[i]), batch, seq)
            x2 = moe_residual(x2, c, od, bf(od_w_out[i]), od_norm_ffn[i], od_router[i], exp_gate, exp_up, exp_down,
                              i * N_EXPERTS, final_norm, final_norm=(layer == depth - 1))
    return x2.reshape(batch, seq, d)
```

```python
import functools
import math

import numpy as np
import jax
import jax.numpy as jnp
from jax import lax
from jax.experimental import pallas as pl
from jax.experimental.pallas import tpu as pltpu

F32 = jnp.float32
BF16 = jnp.bfloat16
I32 = jnp.int32

HEAD_DIM = 64
CHUNK = 64
ROPE_THETA = 10000.0
EPS = 1e-6
A_HEADS = 4
B_HEADS = 8
C_CHANNELS = 512
CONV_WIDTH = 31
D_HEADS = 8
LEFT_CHUNKS = 8
BAND = (LEFT_CHUNKS + 1) * CHUNK
MAX_REL = 128
N_EXPERTS = 8
TOP_K = 2

LANES = 128
SUBLANES = 8
NEG = -1e30
LOG2E = math.log2(math.e)
SIGN_BIT = np.int32(-2 ** 31)
VMEM_LIMIT = 56 * 1024 * 1024


def _cparams(sem):
    return pltpu.CompilerParams(dimension_semantics=sem, vmem_limit_bytes=VMEM_LIMIT)


def _nt_dot(a, b):
    return lax.dot_general(a, b, (((1,), (1,)), ((), ())), preferred_element_type=F32)


def _norm_proj_kernel(x_ref, g_ref, w_ref, cos_ref, sin_ref, *rest, col_chunk, rope_cols, tk):
    if tk:
        wvt_ref, o_ref, vt_ref = rest
    else:
        (o_ref,) = rest
    x = x_ref[...]
    xn = (x * lax.rsqrt(jnp.mean(x * x, axis=-1, keepdims=True) + EPS) * g_ref[...]).astype(BF16)
    n_out = o_ref.shape[1]
    if tk:
        for r0 in range(0, wvt_ref.shape[0], col_chunk):
            res = _nt_dot(wvt_ref[r0:r0 + col_chunk, :], xn)
            for b in range(vt_ref.shape[0]):
                vt_ref[b, r0:r0 + col_chunk, :] = res[:, b * tk:(b + 1) * tk].astype(BF16)
    if rope_cols:
        cos = cos_ref[...]
        sin = sin_ref[...]
        lane = lax.broadcasted_iota(I32, cos.shape, 1)
        first_half = (lane % HEAD_DIM) < (HEAD_DIM // 2)
    for c0 in range(0, n_out, col_chunk):
        r = jnp.dot(xn, w_ref[:, c0:c0 + col_chunk], preferred_element_type=F32)
        if c0 < rope_cols:
            parts = []
            for l0 in range(0, col_chunk, LANES):
                seg = r[:, l0:l0 + LANES]
                partner = jnp.where(first_half,
                                    pltpu.roll(seg, LANES - HEAD_DIM // 2, 1),
                                    pltpu.roll(seg, HEAD_DIM // 2, 1))
                parts.append(seg * cos + partner * sin)
            r = jnp.concatenate(parts, axis=1)
        o_ref[:, c0:c0 + col_chunk] = r.astype(o_ref.dtype)


def _rope_tables(seq):
    half = HEAD_DIM // 2
    inv_freq = ROPE_THETA ** (-jnp.arange(half, dtype=F32) * 2.0 / HEAD_DIM)
    ang = jnp.arange(seq, dtype=F32)[:, None] * inv_freq[None, :]
    cos, sin = jnp.cos(ang), jnp.sin(ang)
    cos_t = jnp.tile(jnp.concatenate([cos, cos], axis=1), (1, LANES // HEAD_DIM))
    sin_t = jnp.tile(jnp.concatenate([-sin, sin], axis=1), (1, LANES // HEAD_DIM))
    return cos_t, sin_t


def norm_proj(x2, g, w_bf16, seq, rope_cols=0, wvt_bf16=None, tk=0, tm=512):
    t, d = x2.shape
    n_out = w_bf16.shape[1]
    tm = min(tm, seq)
    col_chunk = 512
    assert t % tm == 0 and seq % tm == 0 and n_out % col_chunk == 0 and rope_cols % col_chunk == 0
    cos_t, sin_t = _rope_tables(seq)
    tiles_per_seq = seq // tm
    in_specs = [
        pl.BlockSpec((tm, d), lambda i: (i, 0)),
        pl.BlockSpec((1, d), lambda i: (0, 0)),
        pl.BlockSpec((d, n_out), lambda i: (0, 0)),
        pl.BlockSpec((tm, LANES), lambda i: (i % tiles_per_seq, 0)),
        pl.BlockSpec((tm, LANES), lambda i: (i % tiles_per_seq, 0)),
    ]
    args = [x2, g.reshape(1, d), w_bf16, cos_t, sin_t]
    out_shape = jax.ShapeDtypeStruct((t, n_out), BF16)
    out_specs = pl.BlockSpec((tm, n_out), lambda i: (i, 0))
    if wvt_bf16 is not None:
        n_v = wvt_bf16.shape[0]
        tk = min(tk, tm)
        assert tm % tk == 0 and n_v % col_chunk == 0
        in_specs.append(pl.BlockSpec((n_v, d), lambda i: (0, 0)))
        args.append(wvt_bf16)
        out_shape = (out_shape, jax.ShapeDtypeStruct((t // tk, n_v, tk), BF16))
        out_specs = (out_specs, pl.BlockSpec((tm // tk, n_v, tk), lambda i: (i, 0, 0)))
    return pl.pallas_call(
        functools.partial(_norm_proj_kernel, col_chunk=col_chunk, rope_cols=rope_cols,
                          tk=tk if wvt_bf16 is not None else 0),
        out_shape=out_shape,
        grid=(t // tm,),
        in_specs=in_specs,
        out_specs=out_specs,
        compiler_params=_cparams(("parallel",)),
        name="norm_proj",
    )(*args)


def _stack_query_pair(q_ref, q2_scr, tq):
    lane = lax.broadcasted_iota(I32, (tq, LANES), 1)
    qs = (q_ref[...].astype(F32) * (HEAD_DIM ** -0.5 * LOG2E)).astype(BF16)
    zero = jnp.zeros_like(qs)
    q2_scr[0:tq, :] = jnp.where(lane < HEAD_DIM, qs, zero)
    q2_scr[tq:, :] = jnp.where(lane >= HEAD_DIM, qs, zero)


def _diag_visible(tq, tk, rule):
    key = np.arange(tq)[:, None]
    qry = (np.arange(2 * tq) % tq)[None, :]
    return rule(key, qry).reshape(tq // tk, tk, 2 * tq)


def _diff_attn_kernel(q_ref, k_ref, vt_ref, mask_ref, lq1_ref, lk1_ref, lq2_ref, lk2_ref, g_ref, o_ref,
                      q2_scr, m_scr, l_scr, acc_scr, st0_scr, st1_scr, *, tq, tk, lambda_init):
    i = pl.program_id(2)
    assert tq == 2 * tk
    _stack_query_pair(q_ref, q2_scr, tq)
    m_scr[...] = jnp.full(m_scr.shape, NEG, F32)
    l_scr[...] = jnp.zeros(l_scr.shape, F32)
    late_queries = (slice(tk, tq), slice(tq + tk, 2 * tq))
    acc_scr[...] = jnp.zeros(acc_scr.shape, F32)

    def scores(j, st_scr):
        k = k_ref[pl.ds(pl.multiple_of(j * tk, tk), tk), :]
        st_scr[...] = _nt_dot(k, q2_scr[...])

    def consume(j, st_scr, diag=None, cols=(slice(None),)):
        for c in cols:
            st = st_scr[:, c]
            if diag is not None:
                st = st + mask_ref[diag, :, c]
            m_old = m_scr[:, c]
            m_new = jnp.maximum(m_old, jnp.max(st, axis=0, keepdims=True))
            alpha = jnp.exp2(m_old - m_new)
            pt = jnp.exp2(st - m_new)
            l_scr[:, c] = alpha * l_scr[:, c] + jnp.sum(pt, axis=0, keepdims=True)
            acc_scr[:, c] = alpha * acc_scr[:, c] + jnp.dot(vt_ref[j], pt.astype(BF16),
                                                            preferred_element_type=F32)
            m_scr[:, c] = m_new

    scores(0, st0_scr)

    def body(n, carry):
        scores(2 * n + 1, st1_scr)
        consume(2 * n, st0_scr)
        scores(2 * n + 2, st0_scr)
        consume(2 * n + 1, st1_scr)
        return carry

    lax.fori_loop(0, i, body, 0)
    scores(2 * i + 1, st1_scr)
    consume(2 * i, st0_scr, diag=0)
    consume(2 * i + 1, st1_scr, diag=1, cols=late_queries)

    ot = acc_scr[...] / l_scr[...]
    lam = (jnp.exp(jnp.sum(lq1_ref[...] * lk1_ref[...], axis=1, keepdims=True))
           - jnp.exp(jnp.sum(lq2_ref[...] * lk2_ref[...], axis=1, keepdims=True)) + lambda_init)
    od = ot[:, 0:tq].T - lam * ot[:, tq:].T
    y = od * lax.rsqrt(jnp.mean(od * od, axis=-1, keepdims=True) + EPS) * g_ref[...]
    o_ref[...] = (y * (1.0 - lambda_init)).astype(o_ref.dtype)


def diff_attention(proj, vt, lq1, lk1, lq2, lk2, subln_g, batch, seq, lambda_init):
    t = proj.shape[0]
    tk = vt.shape[2]
    tq = 2 * tk
    nq = seq // tq
    assert seq % tq == 0 and tk % CHUNK == 0
    k_off = (A_HEADS * 2 * HEAD_DIM) // LANES
    vec = lambda a: a.reshape(1, -1).astype(F32)
    small = lambda n: pl.BlockSpec((1, n), lambda b, h, i: (0, 0))
    mask = jnp.asarray(np.where(_diag_visible(tq, tk, lambda key, qry: key // CHUNK <= qry // CHUNK), 0.0, NEG), F32)
    return pl.pallas_call(
        functools.partial(_diff_attn_kernel, tq=tq, tk=tk, lambda_init=lambda_init),
        out_shape=jax.ShapeDtypeStruct((t, A_HEADS * 2 * HEAD_DIM), BF16),
        grid=(batch, A_HEADS, nq),
        in_specs=[
            pl.BlockSpec((tq, LANES), lambda b, h, i: (b * nq + i, h)),
            pl.BlockSpec((seq, LANES), lambda b, h, i: (b, k_off + h)),
            pl.BlockSpec((seq // tk, LANES, tk), lambda b, h, i: (b, h, 0)),
            pl.BlockSpec((tq // tk, tk, 2 * tq), lambda b, h, i: (0, 0, 0)),
            small(HEAD_DIM), small(HEAD_DIM), small(HEAD_DIM), small(HEAD_DIM), small(2 * HEAD_DIM),
        ],
        out_specs=pl.BlockSpec((tq, LANES), lambda b, h, i: (b * nq + i, h)),
        scratch_shapes=[
            pltpu.VMEM((2 * tq, LANES), BF16),
            pltpu.VMEM((1, 2 * tq), F32),
            pltpu.VMEM((1, 2 * tq), F32),
            pltpu.VMEM((LANES, 2 * tq), F32),
            pltpu.VMEM((tk, 2 * tq), F32),
            pltpu.VMEM((tk, 2 * tq), F32),
        ],
        compiler_params=_cparams(("parallel", "parallel", "arbitrary")),
        name="diff_attn",
    )(proj, proj, vt, mask, vec(lq1), vec(lk1), vec(lq2), vec(lk2), vec(subln_g))


def _stick_kernel(q_ref, k_ref, vt_ref, mask_ref, o_ref, q2_scr, tri_scr, c_scr, acc_scr, st0_scr, st1_scr, *, tq, tk):
    i = pl.program_id(2)
    assert tq == 2 * tk
    _stack_query_pair(q_ref, q2_scr, tq)
    r_i = lax.broadcasted_iota(I32, (tk, tk), 0)
    c_i = lax.broadcasted_iota(I32, (tk, tk), 1)
    tri_scr[...] = jnp.where(c_i > r_i, 1.0, 0.0).astype(BF16)
    c_scr[...] = jnp.zeros(c_scr.shape, F32)
    acc_scr[...] = jnp.zeros(acc_scr.shape, F32)

    def scores(j, st_scr):
        k = k_ref[pl.ds(pl.multiple_of(j * tk, tk), tk), :]
        st_scr[...] = _nt_dot(k, q2_scr[...])

    def consume(j, st_scr, diag=None, cols=(slice(None),)):
        for c in cols:
            zt = st_scr[:, c]
            neg_abs = lax.bitcast_convert_type(lax.bitcast_convert_type(zt, I32) | SIGN_BIT, F32)
            sp = jnp.maximum(zt, 0.0) + jnp.log2(1.0 + jnp.exp2(neg_abs))
            if diag is not None:
                visible = mask_ref[diag, :, c]
                spm = sp * visible
            else:
                spm = sp
            after = jnp.dot(tri_scr[...], spm.astype(BF16), preferred_element_type=F32)
            w = jnp.exp2(zt - sp - after - c_scr[:, c])
            if diag is not None:
                w = w * visible
            acc_scr[:, c] += jnp.dot(vt_ref[j], w.astype(BF16), preferred_element_type=F32)
            c_scr[:, c] += jnp.sum(spm, axis=0, keepdims=True)

    scores(2 * i + 1, st0_scr)
    scores(2 * i, st1_scr)
    consume(2 * i + 1, st0_scr, diag=1, cols=(slice(tk, tq), slice(tq + tk, 2 * tq)))
    scores(jnp.maximum(2 * i - 1, 0), st0_scr)
    consume(2 * i, st1_scr, diag=0)

    def body(n, carry):
        j = 2 * (i - n) - 1
        scores(j - 1, st1_scr)
        consume(j, st0_scr)
        scores(jnp.maximum(j - 2, 0), st0_scr)
        consume(j - 1, st1_scr)
        return carry

    lax.fori_loop(0, i, body, 0)
    acc = acc_scr[...]
    vrow = lax.broadcasted_iota(I32, (LANES, tq), 0)
    ot = jnp.where(vrow < HEAD_DIM, acc[:, 0:tq], acc[:, tq:])
    o_ref[...] = ot.T.astype(o_ref.dtype)


def stick_breaking(proj, vt, batch, seq):
    t = proj.shape[0]
    tk = vt.shape[2]
    tq = 2 * tk
    nq = seq // tq
    assert seq % tq == 0
    pairs = (B_HEADS * HEAD_DIM) // LANES
    a_blocks = (A_HEADS * 2 * HEAD_DIM) // LANES
    q_off = 2 * a_blocks
    k_off = q_off + pairs
    v_off = a_blocks
    mask = jnp.asarray(_diag_visible(tq, tk, lambda key, qry: key < qry), F32)
    return pl.pallas_call(
        functools.partial(_stick_kernel, tq=tq, tk=tk),
        out_shape=jax.ShapeDtypeStruct((t, B_HEADS * HEAD_DIM), BF16),
        grid=(batch, pairs, nq),
        in_specs=[
            pl.BlockSpec((tq, LANES), lambda b, p, i: (b * nq + i, q_off + p)),
            pl.BlockSpec((seq, LANES), lambda b, p, i: (b, k_off + p)),
            pl.BlockSpec((seq // tk, LANES, tk), lambda b, p, i: (b, v_off + p, 0)),
            pl.BlockSpec((tq // tk, tk, 2 * tq), lambda b, p, i: (0, 0, 0)),
        ],
        out_specs=pl.BlockSpec((tq, LANES), lambda b, p, i: (b * nq + i, p)),
        scratch_shapes=[
            pltpu.VMEM((2 * tq, LANES), BF16),
            pltpu.VMEM((tk, tk), BF16),
            pltpu.VMEM((1, 2 * tq), F32),
            pltpu.VMEM((LANES, 2 * tq), F32),
            pltpu.VMEM((tk, 2 * tq), F32),
            pltpu.VMEM((tk, 2 * tq), F32),
        ],
        compiler_params=_cparams(("parallel", "parallel", "arbitrary")),
        name="stick_attn",
    )(proj, proj, vt, mask)


def _mixer_residual(x_ref, a_ref, b_ref, wo_ref):
    half = a_ref.shape[1]
    return (x_ref[...]
            + jnp.dot(a_ref[...], wo_ref[0:half, :], preferred_element_type=F32)
            + jnp.dot(b_ref[...], wo_ref[half:, :], preferred_element_type=F32))


def _swiglu_chunks(xn, wg_ref, wu_ref, h_scr, ff_chunk):
    d_ff = h_scr.shape[1]
    for c0 in range(0, d_ff, ff_chunk):
        g = jnp.dot(xn, wg_ref[:, c0:c0 + ff_chunk], preferred_element_type=F32)
        u = jnp.dot(xn, wu_ref[:, c0:c0 + ff_chunk], preferred_element_type=F32)
        h_scr[:, c0:c0 + ff_chunk] = (g * jax.nn.sigmoid(g) * u).astype(BF16)


def _ffn_kernel(x_ref, a_ref, b_ref, wo_ref, g_ref, wg_ref, wu_ref, wd_ref, o_ref, h_scr, *, ff_chunk):
    x = _mixer_residual(x_ref, a_ref, b_ref, wo_ref)
    xn = (x * lax.rsqrt(jnp.mean(x * x, axis=-1, keepdims=True) + EPS) * g_ref[...]).astype(BF16)
    _swiglu_chunks(xn, wg_ref, wu_ref, h_scr, ff_chunk)
    o_ref[...] = x + jnp.dot(h_scr[...], wd_ref[...], preferred_element_type=F32)


def ffn_residual(x2, a, b, w_out, g, wg, wu, wd, tm=512):
    t, d = x2.shape
    half = a.shape[1]
    d_ff = wg.shape[1]
    tm = min(tm, t)
    ff_chunk = 256
    assert d_ff % ff_chunk == 0
    resident = lambda shape: pl.BlockSpec(shape, lambda i: (0, 0), pipeline_mode=pl.Buffered(1))
    return pl.pallas_call(
        functools.partial(_ffn_kernel, ff_chunk=ff_chunk),
        out_shape=jax.ShapeDtypeStruct((t, d), F32),
        grid=(t // tm,),
        in_specs=[
            pl.BlockSpec((tm, d), lambda i: (i, 0)),
            pl.BlockSpec((tm, half), lambda i: (i, 0)),
            pl.BlockSpec((tm, half), lambda i: (i, 0)),
            resident((2 * half, d)),
            pl.BlockSpec((1, d), lambda i: (0, 0)),
            resident((d, d_ff)), resident((d, d_ff)), resident((d_ff, d)),
        ],
        out_specs=pl.BlockSpec((tm, d), lambda i: (i, 0)),
        scratch_shapes=[pltpu.VMEM((tm, d_ff), BF16)],
        compiler_params=_cparams(("parallel",)),
        name="ffn",
    )(x2, a, b, w_out, g.reshape(1, d), wg, wu, wd)


CONV_HALO = 32


def _conv_kernel(val_ref, gate_ref, pval_ref, pgate_ref, w_ref, b_ref, lg_ref, lb_ref, o_ref,
                 u_scr, ush_scr, c_scr, *, ts, tiles_per_seq, row_blk):
    i = pl.program_id(0)
    glu = lambda v, g: v.astype(F32) * jax.nn.sigmoid(g.astype(F32))
    keep = jnp.where(i % tiles_per_seq == 0, 0.0, 1.0)
    u_scr[0:CONV_HALO, :] = glu(pval_ref[...], pgate_ref[...]) * keep
    u_scr[CONV_HALO:, :] = glu(val_ref[...], gate_ref[...])
    shift = CONV_HALO - (CONV_WIDTH - 1)
    n_ch = val_ref.shape[1]
    n_sh = ush_scr.shape[1]
    for b in range(1, SUBLANES):
        ush_scr[b - 1] = u_scr[b:b + n_sh, :]
    for c0 in range(0, n_ch, LANES):
        for r0 in range(0, ts, row_blk):
            acc = jnp.broadcast_to(b_ref[:, c0:c0 + LANES], (row_blk, LANES))
            for j in range(CONV_WIDTH):
                a, b = divmod(shift + j, SUBLANES)
                src = u_scr if b == 0 else ush_scr.at[b - 1]
                r = r0 + a * SUBLANES
                acc = acc + w_ref[j:j + 1, c0:c0 + LANES] * src[r:r + row_blk, c0:c0 + LANES]
            c_scr[r0:r0 + row_blk, c0:c0 + LANES] = acc
    c = c_scr[...]
    mu = jnp.mean(c, axis=-1, keepdims=True)
    xc = c - mu
    var = jnp.mean(xc * xc, axis=-1, keepdims=True)
    y = xc * lax.rsqrt(var + EPS) * lg_ref[...] + lb_ref[...]
    o_ref[...] = (y * jax.nn.sigmoid(y)).astype(o_ref.dtype)


def conv_module(proj, conv_w, conv_b, ln_g, ln_b, seq, ts=256):
    t = proj.shape[0]
    ts = min(ts, seq)
    n_ch = C_CHANNELS
    halo_blocks = ts // CONV_HALO
    row = lambda a: a.reshape(1, n_ch).astype(F32)
    small = pl.BlockSpec((1, n_ch), lambda i: (0, 0))
    prev = lambda col: pl.BlockSpec((CONV_HALO, n_ch), lambda i: (jnp.maximum(i * halo_blocks - 1, 0), col))
    return pl.pallas_call(
        functools.partial(_conv_kernel, ts=ts, tiles_per_seq=seq // ts, row_blk=64),
        out_shape=jax.ShapeDtypeStruct((t, n_ch), BF16),
        grid=(t // ts,),
        in_specs=[
            pl.BlockSpec((ts, n_ch), lambda i: (i, 0)),
            pl.BlockSpec((ts, n_ch), lambda i: (i, 1)),
            prev(0), prev(1),
            pl.BlockSpec((CONV_WIDTH, n_ch), lambda i: (0, 0)),
            small, small, small,
        ],
        out_specs=pl.BlockSpec((ts, n_ch), lambda i: (i, 0)),
        scratch_shapes=[pltpu.VMEM((ts + CONV_HALO, n_ch), F32),
                        pltpu.VMEM((SUBLANES - 1, ts + CONV_HALO - SUBLANES, n_ch), F32),
                        pltpu.VMEM((ts, n_ch), F32)],
        compiler_params=_cparams(("parallel",)),
        name="conv_module",
    )(proj, proj, proj, proj, conv_w.astype(F32), row(conv_b), row(ln_g), row(ln_b))


def _rel_bias_kernel(tbl_ref, o_ref):
    p = pl.program_id(0)
    kk = lax.broadcasted_iota(I32, (BAND, LANES), 0)
    lane = lax.broadcasted_iota(I32, (BAND, LANES), 1)
    idx = jnp.clip(lane % CHUNK - (kk - LEFT_CHUNKS * CHUNK), -MAX_REL, MAX_REL) + MAX_REL
    first = lane < CHUNK

    def body(j, acc):
        return acc + jnp.where(idx == j, jnp.where(first, tbl_ref[2 * p, j], tbl_ref[2 * p + 1, j]), 0.0)

    o_ref[0] = lax.fori_loop(0, 2 * MAX_REL + 1, body, jnp.zeros((BAND, LANES), F32)) * LOG2E


def rel_bias(rel_table):
    pairs = D_HEADS // 2
    assert 2 * CHUNK == LANES
    return pl.pallas_call(
        _rel_bias_kernel,
        out_shape=jax.ShapeDtypeStruct((pairs, BAND, LANES), F32),
        grid=(pairs,),
        in_specs=[pl.BlockSpec(memory_space=pltpu.SMEM)],
        out_specs=pl.BlockSpec((1, BAND, LANES), lambda p: (p, 0, 0)),
        compiler_params=_cparams(("parallel",)),
        name="rel_bias",
    )(rel_table.astype(F32))


def _chunk_attn_kernel(q_ref, k_ref, v_ref, bias_ref, o_ref, kpad, vpad, *, group, seq):
    i = pl.program_id(2)
    pad = LEFT_CHUNKS * CHUNK

    @pl.when(i == 0)
    def _():
        kpad[0:pad, :] = jnp.zeros((pad, LANES), BF16)
        vpad[0:pad, :] = jnp.zeros((pad, LANES), BF16)
        kpad[pad:, :] = k_ref[...]
        vpad[pad:, :] = v_ref[...]

    lane = lax.broadcasted_iota(I32, (CHUNK, LANES), 1)
    band_chunk = lax.broadcasted_iota(I32, (BAND, LANES), 0) // CHUNK
    bias = bias_ref[0]
    ones = jnp.ones((BAND, LANES), BF16)
    for g in range(group):
        c = i * group + g
        qs = (q_ref[g * CHUNK:(g + 1) * CHUNK, :].astype(F32) * (HEAD_DIM ** -0.5 * LOG2E)).astype(BF16)
        zero = jnp.zeros_like(qs)
        q2 = jnp.concatenate([jnp.where(lane < HEAD_DIM, qs, zero),
                              jnp.where(lane >= HEAD_DIM, qs, zero)], axis=0)
        start = pl.multiple_of(c * CHUNK, CHUNK)
        kw = kpad[pl.ds(start, BAND), :]
        vw = vpad[pl.ds(start, BAND), :]
        st = _nt_dot(kw, q2) + bias
        st = jnp.where(band_chunk >= LEFT_CHUNKS - c, st, NEG)
        pt = jnp.exp2(st - jnp.max(st, axis=0, keepdims=True))
        o2 = lax.dot_general(pt.astype(BF16), jnp.concatenate([vw, ones], axis=1),
                             (((0,), (0,)), ((), ())), preferred_element_type=F32)
        o = o2[:, 0:LANES] / o2[:, LANES:]
        o_ref[g * CHUNK:(g + 1) * CHUNK, :] = jnp.where(lane < HEAD_DIM, o[0:CHUNK, :],
                                                        o[CHUNK:, :]).astype(o_ref.dtype)


def chunk_rel_attention(proj, bias, batch, seq, group=16):
    t = proj.shape[0]
    pairs = D_HEADS // 2
    q_off = (2 * C_CHANNELS) // LANES
    k_off = q_off + pairs
    v_off = k_off + pairs
    tq = group * CHUNK
    nq = seq // tq
    return pl.pallas_call(
        functools.partial(_chunk_attn_kernel, group=group, seq=seq),
        out_shape=jax.ShapeDtypeStruct((t, D_HEADS * HEAD_DIM), BF16),
        grid=(batch, pairs, nq),
        in_specs=[
            pl.BlockSpec((tq, LANES), lambda b, p, i: (b * nq + i, q_off + p)),
            pl.BlockSpec((seq, LANES), lambda b, p, i: (b, k_off + p)),
            pl.BlockSpec((seq, LANES), lambda b, p, i: (b, v_off + p)),
            pl.BlockSpec((1, BAND, LANES), lambda b, p, i: (p, 0, 0)),
        ],
        out_specs=pl.BlockSpec((tq, LANES), lambda b, p, i: (b * nq + i, p)),
        scratch_shapes=[pltpu.VMEM((seq + LEFT_CHUNKS * CHUNK, LANES), BF16),
                        pltpu.VMEM((seq + LEFT_CHUNKS * CHUNK, LANES), BF16)],
        compiler_params=_cparams(("parallel", "parallel", "arbitrary")),
        name="chunk_attn",
    )(proj, proj, proj, bias)


def _store_row_tiles(ref, val):
    n, d = val.shape
    sub = d // LANES
    for k in range(sub):
        ref[pl.ds(k, n, stride=sub), :] = val[:, k * LANES:(k + 1) * LANES]


def _load_row_tiles(ref, n, d):
    sub = d // LANES
    return jnp.concatenate([ref[pl.ds(k, n, stride=sub), :] for k in range(sub)], axis=1)


def _router_kernel(x_ref, a_ref, b_ref, wo_ref, g_ref, wr_ref, x1_ref, h_ref, meta_ref, cnt_ref, carry_scr, tri_scr,
                   *, tr):
    i = pl.program_id(0)

    @pl.when(i == 0)
    def _():
        carry_scr[...] = jnp.zeros(carry_scr.shape, F32)
        r_i = lax.broadcasted_iota(I32, (tr, tr), 0)
        c_i = lax.broadcasted_iota(I32, (tr, tr), 1)
        tri_scr[...] = jnp.where(r_i < c_i, 1.0, 0.0).astype(BF16)

    x = _mixer_residual(x_ref, a_ref, b_ref, wo_ref)
    x1_ref[...] = x
    hn = x * lax.rsqrt(jnp.mean(x * x, axis=-1, keepdims=True) + EPS) * g_ref[...]
    _store_row_tiles(h_ref, hn)
    h_hi = hn.astype(BF16)
    h_lo = (hn - h_hi.astype(F32)).astype(BF16)
    w = wr_ref[...]
    w_hi = w.astype(BF16)
    w_lo = (w - w_hi.astype(F32)).astype(BF16)
    logits = _nt_dot(w_hi, h_hi) + (_nt_dot(w_hi, h_lo) + _nt_dot(w_lo, h_hi))
    eidx = lax.broadcasted_iota(I32, logits.shape, 0).astype(F32)
    m1 = jnp.max(logits, axis=0, keepdims=True)
    i1 = jnp.min(jnp.where(logits == m1, eidx, float(N_EXPERTS)), axis=0, keepdims=True)
    rest = jnp.where(eidx == i1, -jnp.inf, logits)
    m2 = jnp.max(rest, axis=0, keepdims=True)
    i2 = jnp.min(jnp.where(rest == m2, eidx, float(N_EXPERTS)), axis=0, keepdims=True)
    e = jnp.exp(m2 - m1)
    g1 = 1.0 / (1.0 + e)
    g2 = e / (1.0 + e)
    sel1 = eidx == i1
    sel2 = eidx == i2
    onehot = jnp.where(sel1 | sel2, 1.0, 0.0)
    rank = jnp.dot(onehot.astype(BF16), tri_scr[...], preferred_element_type=F32) + carry_scr[...]
    r1 = jnp.sum(jnp.where(sel1, rank, 0.0), axis=0, keepdims=True)
    r2 = jnp.sum(jnp.where(sel2, rank, 0.0), axis=0, keepdims=True)
    carry = carry_scr[...] + jnp.sum(onehot, axis=1, keepdims=True)
    carry_scr[...] = carry
    zero = jnp.zeros_like(g1)
    meta_ref[0] = jnp.concatenate([i1, i2, r1, r2, g1, g2, zero, zero], axis=0)
    cnt_ref[...] = jnp.broadcast_to(carry, cnt_ref.shape)


def router(x2, a, b, w_out, g, w_router, tr=512):
    t, d = x2.shape
    tr = min(tr, t)
    sub = d // LANES
    half = a.shape[1]
    return pl.pallas_call(
        functools.partial(_router_kernel, tr=tr),
        out_shape=(jax.ShapeDtypeStruct((t, d), F32),
                   jax.ShapeDtypeStruct((t * sub, LANES), F32),
                   jax.ShapeDtypeStruct((t // tr, 8, tr), F32),
                   jax.ShapeDtypeStruct((N_EXPERTS, LANES), F32)),
        grid=(t // tr,),
        in_specs=[
            pl.BlockSpec((tr, d), lambda i: (i, 0)),
            pl.BlockSpec((tr, half), lambda i: (i, 0)),
            pl.BlockSpec((tr, half), lambda i: (i, 0)),
            pl.BlockSpec((2 * half, d), lambda i: (0, 0)),
            pl.BlockSpec((1, d), lambda i: (0, 0)),
            pl.BlockSpec((N_EXPERTS, d), lambda i: (0, 0)),
        ],
        out_specs=(pl.BlockSpec((tr, d), lambda i: (i, 0)),
                   pl.BlockSpec((tr * sub, LANES), lambda i: (i, 0)),
                   pl.BlockSpec((1, 8, tr), lambda i: (i, 0, 0)),
                   pl.BlockSpec((N_EXPERTS, LANES), lambda i: (0, 0))),
        scratch_shapes=[pltpu.VMEM((N_EXPERTS, 1), F32), pltpu.VMEM((tr, tr), BF16)],
        compiler_params=_cparams(("arbitrary",)),
        name="router",
    )(x2, a, b, w_out, g.reshape(1, d), w_router.T.astype(F32))


ROW_DMA_UNROLL = 8


def _rows(ref, row, sub):
    return ref.at[pl.ds(pl.multiple_of(row * sub, sub), sub)]


def _scatter_rows_kernel(ends_ref, idx1_ref, idx2_ref, src_ref, dst_ref, zero_scr, sem, *, rows, sub, pad_rows):
    @pl.when(pl.program_id(0) == 0)
    def _():
        zero_scr[...] = jnp.zeros(zero_scr.shape, zero_scr.dtype)

        def tail(e):
            start = jnp.maximum(ends_ref[e] - pad_rows, 0)
            return pltpu.make_async_copy(zero_scr, dst_ref.at[pl.ds(pl.multiple_of(start * sub, sub),
                                                                  pad_rows * sub)], sem)

        n_exp = ends_ref.shape[0]
        for e in range(n_exp):
            tail(e).start()
        for e in range(n_exp):
            tail(e).wait()
        n_rows = dst_ref.shape[0] // sub
        for t in range(n_exp):
            start = ends_ref[n_exp - 1] + t * pad_rows

            @pl.when(start < n_rows)
            def _():
                spare = pltpu.make_async_copy(
                    zero_scr, dst_ref.at[pl.ds(pl.multiple_of(start * sub, sub), pad_rows * sub)], sem)
                spare.start()
                spare.wait()

    def issue(r, carry):
        pltpu.make_async_copy(_rows(src_ref, r, sub), _rows(dst_ref, idx1_ref[0, 0, r], sub), sem).start()
        pltpu.make_async_copy(_rows(src_ref, r, sub), _rows(dst_ref, idx2_ref[0, 0, r], sub), sem).start(priority=1)
        return carry

    lax.fori_loop(0, rows, issue, 0, unroll=ROW_DMA_UNROLL)
    for _ in range(2):
        pltpu.make_async_copy(src_ref, dst_ref.at[pl.ds(0, rows * sub)], sem).wait()


def scatter_rows(src, idx1, idx2, ends, n_rows, pad_rows, sub, rows=512):
    n = idx1.shape[0]
    rows = min(rows, n)
    assert n % rows == 0 and rows % ROW_DMA_UNROLL == 0 and n_rows >= max(rows, pad_rows)
    idx_spec = pl.BlockSpec((1, 1, rows), lambda i, ends: (i, 0, 0), memory_space=pltpu.SMEM)
    grid_spec = pltpu.PrefetchScalarGridSpec(
        num_scalar_prefetch=1,
        grid=(n // rows,),
        in_specs=[idx_spec, idx_spec, pl.BlockSpec((rows * sub, LANES), lambda i, ends: (i, 0))],
        out_specs=pl.BlockSpec(memory_space=pl.ANY),
        scratch_shapes=[pltpu.VMEM((pad_rows * sub, LANES), src.dtype), pltpu.SemaphoreType.DMA(())],
    )
    return pl.pallas_call(
        functools.partial(_scatter_rows_kernel, rows=rows, sub=sub, pad_rows=pad_rows),
        out_shape=jax.ShapeDtypeStruct((n_rows * sub, LANES), src.dtype),
        grid_spec=grid_spec,
        compiler_params=_cparams(("arbitrary",)),
        name="row_scatter",
    )(ends, idx1.reshape(n // rows, 1, rows), idx2.reshape(n // rows, 1, rows), src)


def _expert_ffn_kernel(te_ref, nt_ref, x_ref, wg_ref, wu_ref, wd_ref, o_ref, xn_scr, h_scr, acc_scr, *, ff_chunk):
    n = pl.program_id(0)
    f = pl.program_id(1)
    tm, d = acc_scr.shape

    @pl.when(n < nt_ref[0])
    def _():
        @pl.when(f == 0)
        def _():
            xn_scr[...] = _load_row_tiles(x_ref, tm, d).astype(BF16)

        _swiglu_chunks(xn_scr[...], wg_ref.at[0], wu_ref.at[0], h_scr, ff_chunk)
        part = jnp.dot(h_scr[...], wd_ref[0], preferred_element_type=F32)

        @pl.when(f == 0)
        def _():
            acc_scr[...] = part

        @pl.when(f > 0)
        def _():
            acc_scr[...] += part

        @pl.when(f == pl.num_programs(1) - 1)
        def _():
            _store_row_tiles(o_ref, acc_scr[...])

    @pl.when((n >= nt_ref[0]) & (f == pl.num_programs(1) - 1))
    def _():
        o_ref[...] = jnp.zeros(o_ref.shape, o_ref.dtype)


def expert_ffn(xs, tile_expert, n_tiles_used, wg, wu, wd, tm, ff_split=2):
    d, d_ff = wg.shape[1], wg.shape[2]
    sub = d // LANES
    nr = xs.shape[0] // sub
    ff_blk = d_ff // ff_split
    ff_chunk = 256
    assert nr % tm == 0 and ff_blk % ff_chunk == 0
    grid_spec = pltpu.PrefetchScalarGridSpec(
        num_scalar_prefetch=2,
        grid=(nr // tm, ff_split),
        in_specs=[
            pl.BlockSpec((tm * sub, LANES), lambda n, f, te, nt: (jnp.minimum(n, nt[0] - 1), 0)),
            pl.BlockSpec((1, d, ff_blk), lambda n, f, te, nt: (te[n], 0, f)),
            pl.BlockSpec((1, d, ff_blk), lambda n, f, te, nt: (te[n], 0, f)),
            pl.BlockSpec((1, ff_blk, d), lambda n, f, te, nt: (te[n], f, 0)),
        ],
        out_specs=pl.BlockSpec((tm * sub, LANES), lambda n, f, te, nt: (n, 0)),
        scratch_shapes=[pltpu.VMEM((tm, d), BF16), pltpu.VMEM((tm, ff_blk), BF16), pltpu.VMEM((tm, d), F32)],
    )
    return pl.pallas_call(
        functools.partial(_expert_ffn_kernel, ff_chunk=ff_chunk),
        out_shape=jax.ShapeDtypeStruct((nr * sub, LANES), F32),
        grid_spec=grid_spec,
        compiler_params=_cparams(("arbitrary", "arbitrary")),
        name="expert_ffn",
    )(tile_expert, n_tiles_used, xs, wg, wu, wd)


def _combine_kernel(idx1_ref, idx2_ref, nidx1_ref, nidx2_ref, x_ref, ys_ref, g1_ref, g2_ref, gn_ref, o_ref,
                    y1_buf, y2_buf, sems, *, final_norm):
    i = pl.program_id(0)
    tm, d = x_ref.shape
    sub = d // LANES
    slot = i % 2

    def gather(i1_ref, i2_ref, s):
        def issue(r, carry):
            pltpu.make_async_copy(_rows(ys_ref, i1_ref[0, 0, r], sub), _rows(y1_buf.at[s], r, sub),
                                  sems.at[s]).start()
            pltpu.make_async_copy(_rows(ys_ref, i2_ref[0, 0, r], sub), _rows(y2_buf.at[s], r, sub),
                                  sems.at[s]).start(priority=1)
            return carry

        lax.fori_loop(0, tm, issue, 0, unroll=ROW_DMA_UNROLL)

    @pl.when(i == 0)
    def _():
        gather(idx1_ref, idx2_ref, 0)

    @pl.when(i + 1 < pl.num_programs(0))
    def _():
        gather(nidx1_ref, nidx2_ref, 1 - slot)

    for buf in (y1_buf, y2_buf):
        pltpu.make_async_copy(ys_ref.at[pl.ds(0, tm * sub)], buf.at[slot], sems.at[slot]).wait()
    y = (x_ref[...] + g1_ref[...] * _load_row_tiles(y1_buf.at[slot], tm, d)
         + g2_ref[...] * _load_row_tiles(y2_buf.at[slot], tm, d))
    if final_norm:
        y = y * lax.rsqrt(jnp.mean(y * y, axis=-1, keepdims=True) + EPS) * gn_ref[...]
    o_ref[...] = y


def combine(x2, ys, idx1, idx2, g1, g2, gn, final_norm, tm=512):
    t, d = x2.shape
    tm = min(tm, t)
    sub = d // LANES
    assert t % tm == 0 and tm % ROW_DMA_UNROLL == 0 and ys.shape[0] >= tm * sub
    big = pl.BlockSpec((tm, d), lambda i: (i, 0))
    col = pl.BlockSpec((tm, 1), lambda i: (i, 0))
    n_steps = t // tm
    idx_spec = pl.BlockSpec((1, 1, tm), lambda i: (i, 0, 0), memory_space=pltpu.SMEM)
    next_spec = pl.BlockSpec((1, 1, tm), lambda i: (jnp.minimum(i + 1, n_steps - 1), 0, 0), memory_space=pltpu.SMEM)
    idx1_b, idx2_b = idx1.reshape(n_steps, 1, tm), idx2.reshape(n_steps, 1, tm)
    return pl.pallas_call(
        functools.partial(_combine_kernel, final_norm=final_norm),
        out_shape=jax.ShapeDtypeStruct((t, d), F32),
        grid=(n_steps,),
        in_specs=[idx_spec, idx_spec, next_spec, next_spec, big, pl.BlockSpec(memory_space=pl.ANY), col, col,
                  pl.BlockSpec((1, d), lambda i: (0, 0))],
        out_specs=big,
        scratch_shapes=[pltpu.VMEM((2, tm * sub, LANES), F32), pltpu.VMEM((2, tm * sub, LANES), F32),
                        pltpu.SemaphoreType.DMA((2,))],
        compiler_params=_cparams(("arbitrary",)),
        name="moe_combine",
    )(idx1_b, idx2_b, idx1_b, idx2_b, x2, ys, g1, g2, gn.reshape(1, d))


def moe_residual(x2, a, b, w_out, g, w_router, wg, wu, wd, expert_base, final_gain, final_norm, tm=512):
    t, d = x2.shape
    tm = min(tm, t)
    x2, hn, meta, counts = router(x2, a, b, w_out, g, w_router)
    tr = meta.shape[2]
    field = lambda r: meta[:, r, :].reshape(t)
    idx1, idx2 = field(0).astype(I32), field(1).astype(I32)
    rank1, rank2 = field(2).astype(I32), field(3).astype(I32)
    gate1, gate2 = field(4), field(5)
    cnt = counts[:, 0].astype(I32)
    padded = ((cnt + tm - 1) // tm) * tm
    ends = jnp.cumsum(padded)
    offs = ends - padded
    pos1 = offs[idx1] + rank1
    pos2 = offs[idx2] + rank2
    n_rows = TOP_K * t + N_EXPERTS * tm
    tile_start = jnp.arange(n_rows // tm, dtype=I32) * tm
    tile_expert = jnp.minimum(jnp.sum(tile_start[:, None] >= ends[None, :], axis=1), N_EXPERTS - 1).astype(I32)
    tile_expert = tile_expert + expert_base
    n_tiles_used = (ends[-1:] // tm).astype(I32)

    sub = d // LANES
    xs = scatter_rows(hn, pos1, pos2, ends.astype(I32), n_rows, tm, sub)
    ys = expert_ffn(xs, tile_expert, n_tiles_used, wg, wu, wd, tm)
    return combine(x2, ys, pos1, pos2, gate1.reshape(t, 1), gate2.reshape(t, 1), final_gain, final_norm)


def kernel(x, ev_norm_mix, ev_w_in, ev_lambda_q1, ev_lambda_k1, ev_lambda_q2, ev_lambda_k2, ev_subln, ev_w_out,
           ev_norm_ffn, ev_ffn_gate, ev_ffn_up, ev_ffn_down, od_norm_mix, od_w_in, od_conv_w, od_conv_b,
           od_conv_ln_g, od_conv_ln_b, od_rel_bias, od_w_out, od_norm_ffn, od_router, od_exp_gate, od_exp_up,
           od_exp_down, final_norm):
    batch, seq, d = x.shape
    depth = 2 * ev_w_in.shape[0]
    assert od_w_in.shape[0] * 2 == depth
    x2 = x.reshape(batch * seq, d)
    bf = lambda a: a.astype(BF16)
    a_qk = A_HEADS * 2 * HEAD_DIM
    a_v = A_HEADS * 2 * HEAD_DIM
    b_w = B_HEADS * HEAD_DIM
    attn_k_block = 256
    all_experts = lambda w: bf(w).reshape((-1,) + w.shape[2:])
    exp_gate, exp_up, exp_down = all_experts(od_exp_gate), all_experts(od_exp_up), all_experts(od_exp_down)
    for layer in range(depth):
        i = layer // 2
        if layer % 2 == 0:
            lambda_init = 0.8 - 0.6 * math.exp(-0.3 * layer)
            w = ev_w_in[i]
            va0, qb0 = 2 * a_qk, 2 * a_qk + a_v
            vb0 = qb0 + 2 * b_w
            w_qk = bf(jnp.concatenate([w[:, 0:va0], w[:, qb0:vb0]], axis=1))
            w_vt = bf(jnp.concatenate([w[:, va0:qb0], w[:, vb0:]], axis=1).T)
            proj, vt = norm_proj(x2, ev_norm_mix[i], w_qk, seq, rope_cols=2 * a_qk, wvt_bf16=w_vt,
                                 tk=attn_k_block)
            oa = diff_attention(proj, vt, ev_lambda_q1[i], ev_lambda_k1[i], ev_lambda_q2[i], ev_lambda_k2[i],
                                ev_subln[i], batch, seq, lambda_init)
            ob = stick_breaking(proj, vt, batch, seq)
            x2 = ffn_residual(x2, oa, ob, bf(ev_w_out[i]), ev_norm_ffn[i], bf(ev_ffn_gate[i]), bf(ev_ffn_up[i]),
                              bf(ev_ffn_down[i]))
        else:
            proj = norm_proj(x2, od_norm_mix[i], bf(od_w_in[i]), seq)
            c = conv_module(proj, od_conv_w[i], od_conv_b[i], od_conv_ln_g[i], od_conv_ln_b[i], seq)
            od = chunk_rel_attention(proj, rel_bias(od_rel_bias[i]), batch, seq)
            x2 = moe_residual(x2, c, od, bf(od_w_out[i]), od_norm_ffn[i], od_router[i], exp_gate, exp_up, exp_down,
                              i * N_EXPERTS, final_norm, final_norm=(layer == depth - 1))
    return x2.reshape(batch, seq, d)
```
